```python
import math
import jax
import jax.numpy as jnp
from jax import lax
import numpy as np

D_MODEL = 2048
BATCH = 4
SEQ = 8192
DEPTH = 1

D_FF = 5632
EPS = 1e-6
SSM_WIDTH = 1024
SSM_GROUP = 16
SSM_GROUPS = SSM_WIDTH // SSM_GROUP
SSM_STATE = 64
DT_MIN = 1e-3
DT_MAX = 1e-1
N_HEADS = 16
N_KV = 4
HEADS_PER_KV = N_HEADS // N_KV
HEAD_DIM = 64
CMP_LEN = 32
CMP_STRIDE = 16
CMP_HID = 256
SEL_BLOCK = 64
N_SEL = 16
WINDOW = 512
Q_BLOCK = 128
SEL_FORCE = 1e4
NEG_INF = -1e30
N_BUCKETS = 32
MAX_DIST = 1024
Q_COLS = N_HEADS * HEAD_DIM
KV_COLS = N_KV * HEAD_DIM
NSA_GATE_COLS = 3 * N_HEADS
IN_COLS = SSM_WIDTH + Q_COLS + 6 * KV_COLS + NSA_GATE_COLS + 2 * D_MODEL

kernel_name = "hybrid_s5_nsa_macaron_block"


def _rmsnorm(x, g):
    xf = x.astype(jnp.float32)
    ms = jnp.mean(xf * xf, axis=-1, keepdims=True)
    return (xf * lax.rsqrt(ms + EPS)).astype(x.dtype) * g


def _swiglu(x, wg, wu, wd):
    return (jax.nn.silu(x @ wg) * (x @ wu)) @ wd


def _rel_bucket(dist):
    dist = jnp.maximum(dist, 0)
    max_exact = N_BUCKETS // 2
    d_f = jnp.maximum(dist, 1).astype(jnp.float32)
    large = max_exact + (jnp.log(d_f / max_exact) / math.log(MAX_DIST / max_exact)
                         * (N_BUCKETS - max_exact)).astype(jnp.int32)
    large = jnp.minimum(large, N_BUCKETS - 1)
    return jnp.where(dist < max_exact, dist, large)


def _masked_softmax(logits_f32, valid):
    return jax.nn.softmax(jnp.where(valid, logits_f32, NEG_INF), axis=-1)


def _ssm_combine(e1, e2):
    a1r, a1i, b1r, b1i = e1
    a2r, a2i, b2r, b2i = e2
    ar = a1r * a2r - a1i * a2i
    ai = a1r * a2i + a1i * a2r
    br = a2r * b1r - a2i * b1i + b2r
    bi = a2r * b1i + a2i * b1r + b2i
    return (ar, ai, br, bi)


def _s5(u, a_re, a_im, log_dt, b_re, b_im, c_re, c_im, d_skip):
    b_, t_, _ = u.shape
    ug = u.reshape(b_, t_, SSM_GROUPS, SSM_GROUP)
    dt = jnp.exp(log_dt)[:, None]
    lam_re = jnp.minimum(a_re, -1e-4)
    lam_im = a_im
    mag = jnp.exp(lam_re * dt)
    ab_re = mag * jnp.cos(lam_im * dt)
    ab_im = mag * jnp.sin(lam_im * dt)
    den = lam_re * lam_re + lam_im * lam_im
    n_re = ab_re - 1.0
    n_im = ab_im
    co_re = (n_re * lam_re + n_im * lam_im) / den
    co_im = (n_im * lam_re - n_re * lam_im) / den
    bb_re = co_re[..., None] * b_re - co_im[..., None] * b_im
    bb_im = co_re[..., None] * b_im + co_im[..., None] * b_re
    bu_re = jnp.einsum('btgc,gpc->tbgp', ug, bb_re)
    bu_im = jnp.einsum('btgc,gpc->tbgp', ug, bb_im)
    a_shape = (t_, 1, SSM_GROUPS, SSM_STATE)
    a_r = jnp.broadcast_to(ab_re[None, None], a_shape)
    a_i = jnp.broadcast_to(ab_im[None, None], a_shape)
    _, _, x_re, x_im = lax.associative_scan(_ssm_combine, (a_r, a_i, bu_re, bu_im), axis=0)
    y = (jnp.einsum('tbgp,gcp->btgc', x_re, c_re)
         - jnp.einsum('tbgp,gcp->btgc', x_im, c_im)
         + d_skip.reshape(SSM_GROUPS, SSM_GROUP) * ug)
    return y.reshape(b_, t_, SSM_WIDTH)


def _compress(raw, pos, w1, w2):
    b_, t_, _ = raw.shape
    n_sub = CMP_LEN // CMP_STRIDE
    nh = t_ // CMP_STRIDE
    nc = nh - n_sub + 1
    kk = raw.reshape(b_, nh, CMP_STRIDE, N_KV, HEAD_DIM)
    blocks = jnp.concatenate([kk[:, i:i + nc] for i in range(n_sub)], axis=2)
    blocks = blocks + pos[None, None, :, None, :]
    flat = blocks.transpose(0, 1, 3, 2, 4).reshape(b_, nc, N_KV, CMP_LEN * HEAD_DIM)
    return jax.nn.gelu(flat @ w1) @ w2


def _nsa(q, kc_raw, vc_raw, ks_raw, vs_raw, kw_raw, vw_raw, g_logits,
         cmp_pos, ck_w1, ck_w2, cv_w1, cv_w2, rel_bias):
    b_, t_, _ = q.shape
    G, HPG, dh = N_KV, HEADS_PER_KV, HEAD_DIM
    scale = dh ** -0.5
    f32 = jnp.float32
    qh = q.reshape(b_, t_, G, HPG, dh)
    gates = jax.nn.sigmoid(g_logits.reshape(b_, t_, G, HPG, 3))
    k_cmp = _compress(kc_raw, cmp_pos, ck_w1, ck_w2)
    v_cmp = _compress(vc_raw, cmp_pos, cv_w1, cv_w2)
    nc = k_cmp.shape[1]
    ns = t_ // SEL_BLOCK
    k_sel = min(N_SEL, ns)
    kblk = ks_raw.reshape(b_, ns, SEL_BLOCK, G, dh).transpose(0, 3, 1, 2, 4)
    vblk = vs_raw.reshape(b_, ns, SEL_BLOCK, G, dh).transpose(0, 3, 1, 2, 4)
    pad = ((0, 0), (WINDOW, 0), (0, 0), (0, 0))
    kwp = jnp.pad(kw_raw.reshape(b_, t_, G, dh), pad)
    vwp = jnp.pad(vw_raw.reshape(b_, t_, G, dh), pad)
    cmp_start = jnp.arange(nc) * CMP_STRIDE
    cmp_end = cmp_start + CMP_LEN - 1
    blk_ids = jnp.arange(ns)
    sel_start = blk_ids * SEL_BLOCK
    overlap = ((cmp_start[:, None] < sel_start[None, :] + SEL_BLOCK)
               & (cmp_start[:, None] + CMP_LEN > sel_start[None, :])).astype(f32)
    table_g = rel_bias.reshape(N_BUCKETS, G, HPG).transpose(1, 0, 2)
    bi = jnp.arange(b_)[:, None, None, None]
    gi = jnp.arange(G)[None, :, None, None]

    def dense_bias(dist):
        bias = rel_bias[_rel_bucket(dist)].reshape(dist.shape + (G, HPG))
        return bias.transpose(2, 3, 0, 1).astype(f32)

    def block(n):
        q0 = n * Q_BLOCK
        t = q0 + jnp.arange(Q_BLOCK)
        qb = lax.dynamic_slice_in_dim(qh, q0, Q_BLOCK, axis=1)
        gb = lax.dynamic_slice_in_dim(gates, q0, Q_BLOCK, axis=1)
        dist_c = t[:, None] - cmp_end[None, :]
        s_c = jnp.einsum('bqghd,bkgd->bghqk', qb, k_cmp).astype(f32) * scale + dense_bias(dist_c)
        p_c = _masked_softmax(s_c, dist_c >= 0) * (t >= CMP_LEN - 1).astype(f32)[:, None]
        o_c = jnp.einsum('bghqk,bkgd->bqghd', p_c.astype(v_cmp.dtype), v_cmp)
        imp = jnp.einsum('bghqk,kj->bgqj', p_c, overlap)
        valid_b = sel_start[None, :] <= t[:, None]
        cur = (t // SEL_BLOCK)[:, None]
        forced = valid_b & ((blk_ids[None, :] == 0) | (blk_ids[None, :] == cur)
                            | (blk_ids[None, :] == cur - 1))
        score = jnp.where(forced, SEL_FORCE, jnp.where(valid_b, imp, -SEL_FORCE))
        _, idx = lax.top_k(score, k_sel)
        ks_len = k_sel * SEL_BLOCK
        kg = kblk[bi, gi, idx].reshape(b_, G, Q_BLOCK, ks_len, dh)
        vg = vblk[bi, gi, idx].reshape(b_, G, Q_BLOCK, ks_len, dh)
        pos_s = (idx[..., None] * SEL_BLOCK + jnp.arange(SEL_BLOCK)).reshape(b_, G, Q_BLOCK, ks_len)
        dist_s = t[None, None, :, None] - pos_s
        bias_s = jnp.moveaxis(table_g[gi, _rel_bucket(dist_s)], -1, 2).astype(f32)
        s_s = jnp.einsum('bqghd,bgqkd->bghqk', qb, kg).astype(f32) * scale + bias_s
        p_s = _masked_softmax(s_s, (dist_s >= 0)[:, :, None])
        o_s = jnp.einsum('bghqk,bgqkd->bqghd', p_s.astype(vg.dtype), vg)
        kwb = lax.dynamic_slice_in_dim(kwp, q0, Q_BLOCK + WINDOW, axis=1)
        vwb = lax.dynamic_slice_in_dim(vwp, q0, Q_BLOCK + WINDOW, axis=1)
        pos_w = q0 - WINDOW + jnp.arange(Q_BLOCK + WINDOW)
        dist_w = t[:, None] - pos_w[None, :]
        valid_w = (dist_w >= 0) & (dist_w < WINDOW) & (pos_w >= 0)[None, :]
        s_w = jnp.einsum('bqghd,bkgd->bghqk', qb, kwb).astype(f32) * scale + dense_bias(dist_w)
        p_w = _masked_softmax(s_w, valid_w)
        o_w = jnp.einsum('bghqk,bkgd->bqghd', p_w.astype(vwb.dtype), vwb)
        return gb[..., 0:1] * o_c + gb[..., 1:2] * o_s + gb[..., 2:3] * o_w

    outs = lax.map(block, jnp.arange(t_ // Q_BLOCK))
    return jnp.moveaxis(outs, 0, 1).reshape(b_, t_, N_HEADS * dh)


def setup_inputs(seed: int = 0) -> dict:
    key = jax.random.key(seed)
    ks = iter(jax.random.split(key, 48))
    L = DEPTH

    def nrm(shape, scale):
        return jax.random.normal(next(ks), shape, jnp.float32) * scale

    def gain():
        return 1.0 + nrm((L, D_MODEL), 0.05)

    G, P, C = SSM_GROUPS, SSM_STATE, SSM_GROUP
    return {
        "x": nrm((BATCH, SEQ, D_MODEL), 1.0),
        "ffn1_pre_g": gain(),
        "ffn1_w_gate": nrm((L, D_MODEL, D_FF), D_MODEL ** -0.5),
        "ffn1_w_up": nrm((L, D_MODEL, D_FF), D_MODEL ** -0.5),
        "ffn1_w_down": nrm((L, D_FF, D_MODEL), D_FF ** -0.5),
        "ffn1_post_g": gain(),
        "mix_pre_g": gain(),
        "w_in": nrm((L, D_MODEL, IN_COLS), D_MODEL ** -0.5),
        "ssm_a_re": -0.5 + nrm((L, G, P), 0.01),
        "ssm_a_im": math.pi * jnp.arange(P, dtype=jnp.float32)[None, None, :] + nrm((L, G, P), 0.01),
        "ssm_log_dt": jax.random.uniform(next(ks), (L, G), jnp.float32,
                                         minval=math.log(DT_MIN), maxval=math.log(DT_MAX)),
        "ssm_b_re": nrm((L, G, P, C), (2 * C) ** -0.5),
        "ssm_b_im": nrm((L, G, P, C), (2 * C) ** -0.5),
        "ssm_c_re": nrm((L, G, C, P), (2 * P) ** -0.5),
        "ssm_c_im": nrm((L, G, C, P), (2 * P) ** -0.5),
        "ssm_d": nrm((L, SSM_WIDTH), 1.0),
        "ssm_glu_w1": nrm((L, SSM_WIDTH, D_MODEL), SSM_WIDTH ** -0.5),
        "ssm_glu_w2": nrm((L, SSM_WIDTH, D_MODEL), SSM_WIDTH ** -0.5),
        "cmp_pos": nrm((L, CMP_LEN, HEAD_DIM), 0.02),
        "cmp_k_w1": nrm((L, CMP_LEN * HEAD_DIM, CMP_HID), (CMP_LEN * HEAD_DIM) ** -0.5),
        "cmp_k_w2": nrm((L, CMP_HID, HEAD_DIM), CMP_HID ** -0.5),
        "cmp_v_w1": nrm((L, CMP_LEN * HEAD_DIM, CMP_HID), (CMP_LEN * HEAD_DIM) ** -0.5),
        "cmp_v_w2": nrm((L, CMP_HID, HEAD_DIM), CMP_HID ** -0.5),
        "nsa_w_o": nrm((L, N_HEADS * HEAD_DIM, D_MODEL), (N_HEADS * HEAD_DIM) ** -0.5),
        "w_out": nrm((L, D_MODEL, D_MODEL), D_MODEL ** -0.5),
        "mix_post_g": gain(),
        "ffn2_pre_g": gain(),
        "ffn2_w_gate": nrm((L, D_MODEL, D_FF), D_MODEL ** -0.5),
        "ffn2_w_up": nrm((L, D_MODEL, D_FF), D_MODEL ** -0.5),
        "ffn2_w_down": nrm((L, D_FF, D_MODEL), D_FF ** -0.5),
        "ffn2_post_g": gain(),
        "rel_bias": nrm((N_BUCKETS, N_HEADS), 0.5),
    }


def reference(x, ffn1_pre_g, ffn1_w_gate, ffn1_w_up, ffn1_w_down, ffn1_post_g,
              mix_pre_g, w_in, ssm_a_re, ssm_a_im, ssm_log_dt, ssm_b_re, ssm_b_im,
              ssm_c_re, ssm_c_im, ssm_d, ssm_glu_w1, ssm_glu_w2, cmp_pos,
              cmp_k_w1, cmp_k_w2, cmp_v_w1, cmp_v_w2, nsa_w_o, w_out, mix_post_g,
              ffn2_pre_g, ffn2_w_gate, ffn2_w_up, ffn2_w_down, ffn2_post_g, rel_bias):
    widths = [SSM_WIDTH, Q_COLS] + [KV_COLS] * 6 + [NSA_GATE_COLS, D_MODEL, D_MODEL]
    split_at = np.cumsum(widths)[:-1].tolist()
    h = x
    for l in range(DEPTH):
        f1 = _swiglu(_rmsnorm(h, ffn1_pre_g[l]), ffn1_w_gate[l], ffn1_w_up[l], ffn1_w_down[l])
        h = h + 0.5 * _rmsnorm(f1, ffn1_post_g[l])
        u = _rmsnorm(h, mix_pre_g[l])
        proj = u @ w_in[l]
        (u_ssm, q, kc, vc, ksl, vsl, kw, vw, g_nsa, g_a, g_b) = jnp.split(proj, split_at, axis=-1)
        y_ssm = jax.nn.gelu(_s5(u_ssm, ssm_a_re[l], ssm_a_im[l], ssm_log_dt[l], ssm_b_re[l],
                                ssm_b_im[l], ssm_c_re[l], ssm_c_im[l], ssm_d[l]))
        y_a = (y_ssm @ ssm_glu_w1[l]) * jax.nn.sigmoid(y_ssm @ ssm_glu_w2[l])
        o_nsa = _nsa(q, kc, vc, ksl, vsl, kw, vw, g_nsa, cmp_pos[l], cmp_k_w1[l], cmp_k_w2[l],
                     cmp_v_w1[l], cmp_v_w2[l], rel_bias)
        y_b = o_nsa @ nsa_w_o[l]
        mixed = (jax.nn.sigmoid(g_a) * y_a + jax.nn.sigmoid(g_b) * y_b) @ w_out[l]
        h = h + _rmsnorm(mixed, mix_post_g[l])
        f2 = _swiglu(_rmsnorm(h, ffn2_pre_g[l]), ffn2_w_gate[l], ffn2_w_up[l], ffn2_w_down[l])
        h = h + 0.5 * _rmsnorm(f2, ffn2_post_g[l])
    return h
```

```python
import functools
import math

import numpy as np
import jax
import jax.numpy as jnp
from jax import lax
from jax.experimental import pallas as pl
from jax.experimental.pallas import tpu as pltpu

F32 = jnp.float32
BF16 = jnp.bfloat16

D_MODEL = 2048
D_FF = 5632
EPS = 1e-6
SSM_WIDTH = 1024
SSM_GROUP = 16
SSM_GROUPS = 64
SSM_STATE = 64
N_HEADS = 16
N_KV = 4
HPG = 4
HEAD_DIM = 64
CMP_LEN = 32
CMP_STRIDE = 16
CMP_HID = 256
SEL_BLOCK = 64
N_SEL = 16
WINDOW = 512
SEL_FORCE = 1e4
NEG_INF = -1e30
N_BUCKETS = 32
MAX_DIST = 1024

VMEM_LIMIT = 56 * 1024 * 1024

FFN_TM = 512
FFN_TF = 512
PROJ_TM = 512
S5_LC = 64
S5_CW = 512
TQ = 256
LQ = HPG * TQ
CMP_FRONT = 128
CMP_NEAR = 72
CMP_BAND = CMP_NEAR + TQ // CMP_STRIDE
N_SEL_BAND = MAX_DIST // TQ + 1
N_WIN = WINDOW // TQ + 1
MERGE_TM = 1024
MERGE_TN = 512
OUT_TM = 512


def _cparams(sem):
    return pltpu.CompilerParams(dimension_semantics=sem, vmem_limit_bytes=VMEM_LIMIT)


def _rms(x):
    ms = jnp.mean(x * x, axis=-1, keepdims=True)
    return x * lax.rsqrt(ms + EPS)


def _sigmoid(x):
    return 1.0 / (1.0 + jnp.exp(-x))


def _gelu(x):
    return 0.5 * x * (1.0 + jnp.tanh(math.sqrt(2.0 / math.pi) * (x + 0.044715 * (x * x * x))))


def _dot(a, b):
    return jnp.dot(a, b, preferred_element_type=F32)


def _dot_nt(a, b):
    return lax.dot_general(a, b, (((1,), (1,)), ((), ())), preferred_element_type=F32)


def _ffn_kernel(x_ref, pre_ref, wg_ref, wu_ref, wd_ref, post_ref, nxt_ref, *rest, emit_next):
    if emit_next:
        o_ref, u_ref, xn_ref, acc_ref = rest
    else:
        o_ref, xn_ref, acc_ref = rest
    f = pl.program_id(1)

    @pl.when(f == 0)
    def _():
        xn_ref[...] = (_rms(x_ref[...]) * pre_ref[...]).astype(BF16)
        acc_ref[...] = jnp.zeros_like(acc_ref)

    xn = xn_ref[...]
    g = _dot(xn, wg_ref[...])
    u = _dot(xn, wu_ref[...])
    a = (g * _sigmoid(g) * u).astype(BF16)
    acc_ref[...] += _dot(a, wd_ref[...])

    @pl.when(f == pl.num_programs(1) - 1)
    def _():
        h = x_ref[...] + 0.5 * (_rms(acc_ref[...]) * post_ref[...])
        o_ref[...] = h
        if emit_next:
            u_ref[...] = (_rms(h) * nxt_ref[...]).astype(BF16)


def _ffn(x, pre_g, wg, wu, wd, post_g, nxt_g, emit_next):
    n, d = x.shape
    nf = wg.shape[1]
    tm, tf = min(FFN_TM, n), FFN_TF
    out_shape = [jax.ShapeDtypeStruct((n, d), F32)]
    out_specs = [pl.BlockSpec((tm, d), lambda i, f: (i, 0))]
    if emit_next:
        out_shape.append(jax.ShapeDtypeStruct((n, d), BF16))
        out_specs.append(pl.BlockSpec((tm, d), lambda i, f: (i, 0)))
    vec = pl.BlockSpec((1, d), lambda i, f: (0, 0))
    res = pl.pallas_call(
        functools.partial(_ffn_kernel, emit_next=emit_next),
        out_shape=out_shape,
        grid=(n // tm, nf // tf),
        in_specs=[
            pl.BlockSpec((tm, d), lambda i, f: (i, 0)),
            vec,
            pl.BlockSpec((d, tf), lambda i, f: (0, f)),
            pl.BlockSpec((d, tf), lambda i, f: (0, f)),
            pl.BlockSpec((tf, d), lambda i, f: (f, 0)),
            vec,
            vec,
        ],
        out_specs=out_specs,
        scratch_shapes=[pltpu.VMEM((tm, d), BF16), pltpu.VMEM((tm, d), F32)],
        compiler_params=_cparams(("parallel", "arbitrary")),
        name="ffn",
    )(x, pre_g, wg, wu, wd, post_g, nxt_g)
    return res if emit_next else (res[0], None)


def _proj_kernel(u_ref, wn_ref, wt_ref, pn_ref, pt_ref):
    u = u_ref[0]
    pn_ref[0] = _dot(u, wn_ref[...]).astype(pn_ref.dtype)
    pt_ref[0] = _dot_nt(wt_ref[...], u).astype(pt_ref.dtype)


def _proj(u, wn, wt):
    b, t, d = u.shape
    tm = min(PROJ_TM, t)
    return pl.pallas_call(
        _proj_kernel,
        out_shape=[jax.ShapeDtypeStruct((b, t, wn.shape[1]), BF16),
                   jax.ShapeDtypeStruct((b, wt.shape[0], t), BF16)],
        grid=(b, t // tm),
        in_specs=[
            pl.BlockSpec((1, tm, d), lambda i, j: (i, j, 0)),
            pl.BlockSpec(wn.shape, lambda i, j: (0, 0)),
            pl.BlockSpec(wt.shape, lambda i, j: (0, 0)),
        ],
        out_specs=[
            pl.BlockSpec((1, tm, wn.shape[1]), lambda i, j: (i, j, 0)),
            pl.BlockSpec((1, wt.shape[0], tm), lambda i, j: (i, 0, j)),
        ],
        compiler_params=_cparams(("parallel", "parallel")),
        name="proj",
    )(u, wn, wt)


def _s5_kernel(lhs_ref, wbu_ref, wc_ref, are_ref, aim_ref, dmat_ref, y_ref,
               xre_ref, xim_ref, bre_ref, bim_ref):
    lc = lhs_ref.shape[0] // 8

    @pl.when(pl.program_id(0) == 0)
    def _():
        xre_ref[...] = jnp.zeros_like(xre_ref)
        xim_ref[...] = jnp.zeros_like(xim_ref)

    for j in range(16):
        jq = j // 4
        lj = jnp.concatenate([lhs_ref[:, 128 * jq:128 * (jq + 1)],
                              lhs_ref[:, 512 + 128 * jq:512 + 128 * (jq + 1)]], axis=1)
        out = _dot(lj, wbu_ref[j])
        bre_ref[:, 128 * j:128 * (j + 1)] = out[:, :128]
        bim_ref[:, 128 * j:128 * (j + 1)] = out[:, 128:]

    for cc in range(bre_ref.shape[1] // S5_CW):
        cs = slice(cc * S5_CW, (cc + 1) * S5_CW)
        ar = are_ref[:, cs]
        ai = aim_ref[:, cs]

        def step(i, carry, cs=cs, ar=ar, ai=ai):
            xr, xi = carry
            rows = pl.ds(pl.multiple_of(i * 8, 8), 8)
            nr = ar * xr - ai * xi + bre_ref[rows, cs]
            ni = ar * xi + ai * xr + bim_ref[rows, cs]
            bre_ref[rows, cs] = nr
            bim_ref[rows, cs] = ni
            return nr, ni

        xr, xi = lax.fori_loop(0, lc, step, (xre_ref[:, cs], xim_ref[:, cs]))
        xre_ref[:, cs] = xr
        xim_ref[:, cs] = xi

    odd = (lax.broadcasted_iota(jnp.int32, (1, 8, 1), 1) % 2) == 1
    for jq in range(4):
        acc = None
        for jm in range(4):
            j = 4 * jq + jm
            xj = jnp.concatenate([bre_ref[:, 128 * j:128 * (j + 1)],
                                  bim_ref[:, 128 * j:128 * (j + 1)]], axis=1).astype(BF16)
            part = _dot(xj, wc_ref[j])
            acc = part if acc is None else acc + part
        acc3 = acc.reshape(lc, 8, 256)
        y = jnp.where(odd, acc3[:, :, 128:], acc3[:, :, :128])
        u_own = (lhs_ref[:, 128 * jq:128 * (jq + 1)].astype(F32)
                 + lhs_ref[:, 512 + 128 * jq:512 + 128 * (jq + 1)].astype(F32)).reshape(lc, 8, 128)
        y = y + dmat_ref[:, 128 * jq:128 * (jq + 1)][None] * u_own
        y_ref[:, 128 * jq:128 * (jq + 1)] = _gelu(y).reshape(lc * 8, 128).astype(y_ref.dtype)


def _s5(lhs, wbu, wc, a_re, a_im, dmat):
    rows = lhs.shape[0]
    r = 8 * S5_LC
    nl = a_re.shape[1]
    return pl.pallas_call(
        _s5_kernel,
        out_shape=jax.ShapeDtypeStruct((rows, 512), BF16),
        grid=(rows // r,),
        in_specs=[
            pl.BlockSpec((r, 1024), lambda c: (c, 0)),
            pl.BlockSpec(wbu.shape, lambda c: (0, 0, 0)),
            pl.BlockSpec(wc.shape, lambda c: (0, 0, 0)),
            pl.BlockSpec(a_re.shape, lambda c: (0, 0)),
            pl.BlockSpec(a_im.shape, lambda c: (0, 0)),
            pl.BlockSpec(dmat.shape, lambda c: (0, 0)),
        ],
        out_specs=pl.BlockSpec((r, 512), lambda c: (c, 0)),
        scratch_shapes=[pltpu.VMEM((8, nl), F32), pltpu.VMEM((8, nl), F32),
                        pltpu.VMEM((r, nl), F32), pltpu.VMEM((r, nl), F32)],
        compiler_params=_cparams(("arbitrary",)),
        name="s5",
    )(lhs, wbu, wc, a_re, a_im, dmat)


def _compress_kernel(rk_ref, rv_ref, pos_ref, w1k_ref, w2k_ref, w1v_ref, w2vt_ref, k_ref, vt_ref):
    m = rk_ref.shape[2]

    def hidden(r_ref, w1_ref):
        z = _dot(r_ref[0, 0], w1_ref[...])
        pz = _dot(pos_ref[...], w1_ref[...])
        posvec = pz[0:1, :CMP_HID] + pz[1:2, CMP_HID:]
        zhi = pltpu.roll(z[:, CMP_HID:], m - 1, 0)
        return _gelu(z[:, :CMP_HID] + zhi + posvec).astype(BF16)

    hk = hidden(rk_ref, w1k_ref)
    k_ref[0, 0, :CMP_FRONT, :] = jnp.zeros((CMP_FRONT, HEAD_DIM), k_ref.dtype)
    k_ref[0, 0, CMP_FRONT:, :] = _dot(hk, w2k_ref[...]).astype(k_ref.dtype)
    hv = hidden(rv_ref, w1v_ref)
    vt_ref[0, 0, :, :CMP_FRONT] = jnp.zeros((HEAD_DIM, CMP_FRONT), vt_ref.dtype)
    vt_ref[0, 0, :, CMP_FRONT:] = _dot_nt(w2vt_ref[...], hv).astype(vt_ref.dtype)


def _compress(rk, rv, pos8, w1k, w2k, w1v, w2vt):
    b, g, m, kd = rk.shape
    ncp = CMP_FRONT + m
    full = lambda a: pl.BlockSpec(a.shape, lambda i, j: (0,) * a.ndim)
    return pl.pallas_call(
        _compress_kernel,
        out_shape=[jax.ShapeDtypeStruct((b, g, ncp, HEAD_DIM), BF16),
                   jax.ShapeDtypeStruct((b, g, HEAD_DIM, ncp), BF16)],
        grid=(b, g),
        in_specs=[
            pl.BlockSpec((1, 1, m, kd), lambda i, j: (i, j, 0, 0)),
            pl.BlockSpec((1, 1, m, kd), lambda i, j: (i, j, 0, 0)),
            full(pos8), full(w1k), full(w2k), full(w1v), full(w2vt),
        ],
        out_specs=[
            pl.BlockSpec((1, 1, ncp, HEAD_DIM), lambda i, j: (i, j, 0, 0)),
            pl.BlockSpec((1, 1, HEAD_DIM, ncp), lambda i, j: (i, j, 0, 0)),
        ],
        compiler_params=_cparams(("parallel", "parallel")),
        name="compress",
    )(rk, rv, pos8, w1k, w2k, w1v, w2vt)


def _nsa_kernel(qt_ref, gt_ref, kc_ref, vct_ref, ksl_ref, vslt_ref,
                kw0_ref, kw1_ref, kw2_ref, vw0_ref, vw1_ref, vw2_ref,
                bsel_ref, bwin_ref, bcmp_ref, cvec_ref, ovl_ref,
                o_ref,
                qpad_ref, sc_ref, nm_ref, m_ref, l_ref, acc_ref, oc_ref, os_ref):
    g = pl.program_id(1)
    nq = pl.program_id(2)
    t0 = nq * TQ
    ncp = kc_ref.shape[2]
    ns = nm_ref.shape[0]
    cpq = TQ // CMP_STRIDE
    bpt = TQ // SEL_BLOCK

    q4 = qt_ref[0]
    qt = jnp.concatenate([q4[HEAD_DIM * h:HEAD_DIM * (h + 1)] for h in range(HPG)], axis=1)
    qt = qt * jnp.asarray(HEAD_DIM ** -0.5, BF16)
    qpad_ref[...] = jnp.zeros_like(qpad_ref)
    qpad_ref[pl.ds(pl.multiple_of(g * HEAD_DIM, HEAD_DIM), HEAD_DIM), :] = qt
    cvec = cvec_ref[0]
    lane = lax.broadcasted_iota(jnp.int32, (1, LQ), 1)
    t_lane = t0 + lane % TQ

    s = _dot(kc_ref[0, 0], qt)
    row = lax.broadcasted_iota(jnp.int32, (ncp, 1), 0)
    live = (row >= CMP_FRONT) & (row < cpq * nq + CMP_FRONT + cpq)
    sc_ref[...] = s + jnp.where(live, cvec, NEG_INF)
    band = pl.ds(pl.multiple_of(cpq * nq + CMP_FRONT - CMP_NEAR, 8), CMP_BAND)
    sc_ref[band, :] = sc_ref[band, :] + bcmp_ref[0]
    s = sc_ref[...]
    e = jnp.exp(s - jnp.max(s, axis=0, keepdims=True))
    p = e / jnp.sum(e, axis=0, keepdims=True)
    p = p * (t_lane >= CMP_LEN - 1).astype(F32)
    oc_ref[...] = _dot(vct_ref[0, 0], p.astype(BF16))

    ps = p[:, 0:TQ] + p[:, TQ:2 * TQ] + p[:, 2 * TQ:3 * TQ] + p[:, 3 * TQ:4 * TQ]
    ps_hi = ps.astype(BF16)
    ps_lo = (ps - ps_hi.astype(F32)).astype(BF16)
    imp = _dot(ovl_ref[...], ps_hi) + _dot(ovl_ref[...], ps_lo)
    blk = lax.broadcasted_iota(jnp.int32, (ns, TQ), 0)
    tt = t0 + lax.broadcasted_iota(jnp.int32, (ns, TQ), 1)
    valid_b = blk * SEL_BLOCK <= tt
    cur = tt // SEL_BLOCK
    forced = valid_b & ((blk == 0) | (blk == cur) | (blk == cur - 1))
    score = jnp.where(forced, SEL_FORCE, jnp.where(valid_b, imp, -SEL_FORCE))
    blkf = blk.astype(F32)

    def pick_one(_, carry):
        sc, chosen = carry
        mx = jnp.max(sc, axis=0, keepdims=True)
        first = jnp.min(jnp.where(sc == mx, blkf, float(ns)), axis=0, keepdims=True)
        hit = blkf == first
        return jnp.where(hit, -3e38, sc), jnp.where(hit, 1.0, chosen)

    _, chosen = lax.fori_loop(0, min(N_SEL, ns), pick_one, (score, jnp.zeros((ns, TQ), F32)))
    nm_ref[...] = jnp.where(chosen > 0.0, 0.0, NEG_INF)

    def reset():
        m_ref[...] = jnp.full_like(m_ref, NEG_INF)
        l_ref[...] = jnp.zeros_like(l_ref)
        acc_ref[...] = jnp.zeros_like(acc_ref)

    def online(s, vt):
        m_old = m_ref[...]
        m_new = jnp.maximum(m_old, jnp.max(s, axis=0, keepdims=True))
        alpha = jnp.exp(m_old - m_new)
        p = jnp.exp(s - m_new)
        l_ref[...] = alpha * l_ref[...] + jnp.sum(p, axis=0, keepdims=True)
        acc_ref[...] = alpha * acc_ref[...] + _dot(vt, p.astype(BF16))
        m_ref[...] = m_new

    def sel_tile(kt, bias):
        k0 = pl.multiple_of(kt * TQ, TQ)
        s = _dot(ksl_ref[0, pl.ds(k0, TQ), :], qpad_ref[...]) + bias
        parts = []
        for jj in range(bpt):
            r = nm_ref[pl.ds(kt * bpt + jj, 1), :]
            parts.append(s[SEL_BLOCK * jj:SEL_BLOCK * (jj + 1), :] + jnp.concatenate([r] * HPG, axis=1))
        online(jnp.concatenate(parts, axis=0), vslt_ref[0, :, pl.ds(k0, TQ)])

    reset()
    n_far = jnp.maximum(nq - (N_SEL_BAND - 1), 0)

    def far_body(kt, c):
        sel_tile(kt, cvec)
        return c

    lax.fori_loop(0, n_far, far_body, 0)

    def near_body(kt, c):
        r0 = pl.multiple_of(MAX_DIST - TQ * (nq - kt), TQ)
        sel_tile(kt, bsel_ref[0, pl.ds(r0, TQ), :])
        return c

    lax.fori_loop(n_far, nq + 1, near_body, 0)
    os_ref[...] = acc_ref[...] / l_ref[...]

    reset()
    kws = (kw0_ref, kw1_ref, kw2_ref)
    vws = (vw0_ref, vw1_ref, vw2_ref)
    for d in range(N_WIN):
        @pl.when(nq >= d)
        def _(d=d):
            s = _dot(kws[d][0], qpad_ref[...]) + bwin_ref[0, TQ * d:TQ * (d + 1), :]
            online(s, vws[d][0])
    o_w = acc_ref[...] / l_ref[...]

    sg = _sigmoid(gt_ref[0].astype(F32))

    def gate(br):
        return jnp.concatenate([sg[HPG * br + h:HPG * br + h + 1] for h in range(HPG)], axis=1)

    ot = gate(0) * oc_ref[...] + gate(1) * os_ref[...] + gate(2) * o_w
    stacked = jnp.concatenate([ot[:, TQ * h:TQ * (h + 1)] for h in range(HPG)], axis=0)
    o_ref[0] = stacked.T.astype(o_ref.dtype)


def _nsa(pn, pt, kc, vct, bsel, bwin, bcmp, cvec, ovl):
    b, t, _ = pn.shape
    nq = t // TQ
    ncp = kc.shape[2]
    ns = t // SEL_BLOCK
    q_rows = N_HEADS * HEAD_DIM
    gate_blk0 = (q_rows + 2 * N_KV * HEAD_DIM) // 16
    vsl_blk0 = q_rows // HEAD_DIM
    vw_blk0 = vsl_blk0 + N_KV
    ksl_col, kw_col = 6, 7

    def kw_spec(d):
        return pl.BlockSpec((1, TQ, 256), lambda i, j, n: (i, jnp.maximum(n - d, 0), kw_col))

    def vw_spec(d):
        return pl.BlockSpec((1, HEAD_DIM, TQ), lambda i, j, n: (i, vw_blk0 + j, jnp.maximum(n - d, 0)))

    per_g = lambda a: pl.BlockSpec((1,) + a.shape[1:], lambda i, j, n: (j,) + (0,) * (a.ndim - 1))
    return pl.pallas_call(
        _nsa_kernel,
        out_shape=jax.ShapeDtypeStruct((b, t, q_rows), BF16),
        grid=(b, N_KV, nq),
        in_specs=[
            pl.BlockSpec((1, HPG * HEAD_DIM, TQ), lambda i, j, n: (i, j, n)),
            pl.BlockSpec((1, 16, TQ), lambda i, j, n: (i, gate_blk0 + j, n)),
            pl.BlockSpec((1, 1, ncp, HEAD_DIM), lambda i, j, n: (i, j, 0, 0)),
            pl.BlockSpec((1, 1, HEAD_DIM, ncp), lambda i, j, n: (i, j, 0, 0)),
            pl.BlockSpec((1, t, 256), lambda i, j, n: (i, 0, ksl_col)),
            pl.BlockSpec((1, HEAD_DIM, t), lambda i, j, n: (i, vsl_blk0 + j, 0)),
            kw_spec(0), kw_spec(1), kw_spec(2),
            vw_spec(0), vw_spec(1), vw_spec(2),
            per_g(bsel), per_g(bwin), per_g(bcmp), per_g(cvec),
            pl.BlockSpec(ovl.shape, lambda i, j, n: (0, 0)),
        ],
        out_specs=pl.BlockSpec((1, TQ, HPG * HEAD_DIM), lambda i, j, n: (i, n, j)),
        scratch_shapes=[
            pltpu.VMEM((N_KV * HEAD_DIM, LQ), BF16),
            pltpu.VMEM((ncp, LQ), F32),
            pltpu.VMEM((ns, TQ), F32),
            pltpu.VMEM((1, LQ), F32), pltpu.VMEM((1, LQ), F32), pltpu.VMEM((HEAD_DIM, LQ), F32),
            pltpu.VMEM((HEAD_DIM, LQ), F32), pltpu.VMEM((HEAD_DIM, LQ), F32),
        ],
        compiler_params=_cparams(("parallel", "parallel", "arbitrary")),
        name="nsa",
    )(pt, pt, kc, vct, pn, pt, pn, pn, pn, pt, pt, pt, bsel, bwin, bcmp, cvec, ovl)


def _merge_kernel(ys_ref, o_ref, u_ref, w1_ref, w2_ref, wo_ref, wga_ref, wgb_ref, z_ref):
    ys = ys_ref[...]
    ya = _dot(ys, w1_ref[...]) * _sigmoid(_dot(ys, w2_ref[...]))
    yb = _dot(o_ref[...], wo_ref[...])
    u = u_ref[...]
    z = _sigmoid(_dot(u, wga_ref[...])) * ya + _sigmoid(_dot(u, wgb_ref[...])) * yb
    z_ref[...] = z.astype(z_ref.dtype)


def _merge(ys, o, u, w1, w2, wo, wga, wgb):
    n = ys.shape[0]
    d = w1.shape[1]
    tm, tn = min(MERGE_TM, n), MERGE_TN
    row = lambda a: pl.BlockSpec((tm, a.shape[1]), lambda i, j: (i, 0))
    col = lambda a: pl.BlockSpec((a.shape[0], tn), lambda i, j: (0, j))
    return pl.pallas_call(
        _merge_kernel,
        out_shape=jax.ShapeDtypeStruct((n, d), BF16),
        grid=(n // tm, d // tn),
        in_specs=[row(ys), row(o), row(u), col(w1), col(w2), col(wo), col(wga), col(wgb)],
        out_specs=pl.BlockSpec((tm, tn), lambda i, j: (i, j)),
        compiler_params=_cparams(("parallel", "parallel")),
        name="merge",
    )(ys, o, u, w1, w2, wo, wga, wgb)


def _outproj_kernel(z_ref, h_ref, w_ref, g_ref, o_ref):
    mixed = _dot(z_ref[...], w_ref[...])
    o_ref[...] = h_ref[...] + _rms(mixed) * g_ref[...]


def _outproj(z, h, w, g):
    n, d = h.shape
    tm = min(OUT_TM, n)
    return pl.pallas_call(
        _outproj_kernel,
        out_shape=jax.ShapeDtypeStruct((n, d), F32),
        grid=(n // tm,),
        in_specs=[
            pl.BlockSpec((tm, d), lambda i: (i, 0)),
            pl.BlockSpec((tm, d), lambda i: (i, 0)),
            pl.BlockSpec(w.shape, lambda i: (0, 0)),
            pl.BlockSpec((1, d), lambda i: (0, 0)),
        ],
        out_specs=pl.BlockSpec((tm, d), lambda i: (i, 0)),
        compiler_params=_cparams(("parallel",)),
        name="outproj",
    )(z, h, w, g)


def _rel_bucket(dist):
    dist = jnp.maximum(dist, 0)
    max_exact = N_BUCKETS // 2
    d_f = jnp.maximum(dist, 1).astype(jnp.float32)
    large = max_exact + (jnp.log(d_f / max_exact) / math.log(MAX_DIST / max_exact)
                         * (N_BUCKETS - max_exact)).astype(jnp.int32)
    large = jnp.minimum(large, N_BUCKETS - 1)
    return jnp.where(dist < max_exact, dist, large)


def _bias_tables(rel_bias, t):
    by_dist = rel_bias[_rel_bucket(jnp.arange(t))]
    far = rel_bias[_rel_bucket(jnp.asarray(MAX_DIST))]

    def expand(dist, ok):
        v = by_dist[jnp.clip(dist, 0, t - 1)]
        v = jnp.where(ok[..., None], v, NEG_INF)
        v = v.reshape(dist.shape + (N_KV, HPG)).transpose(2, 0, 3, 1)
        return v.reshape(N_KV, dist.shape[0], LQ)

    tl = jnp.arange(TQ)[None, :]
    r = jnp.arange(MAX_DIST + TQ)[:, None]
    dist = tl + MAX_DIST - r
    bsel = expand(dist, dist >= 0)
    rr = jnp.arange(N_WIN * TQ)[:, None]
    dist = tl + TQ * (rr // TQ) - (rr % TQ)
    bwin = expand(dist, (dist >= 0) & (dist < WINDOW))
    i2 = jnp.arange(CMP_BAND)[:, None]
    dist = tl - CMP_STRIDE * (i2 - CMP_NEAR) - (CMP_LEN - 1)
    cvec = jnp.broadcast_to(far.reshape(N_KV, 1, HPG, 1), (N_KV, 1, HPG, TQ)).reshape(N_KV, 1, LQ)
    bcmp = expand(dist, dist >= 0)
    bcmp = jnp.where(bcmp > 0.5 * NEG_INF, bcmp - cvec, NEG_INF)
    return bsel, bwin, bcmp, cvec


def _overlap_t(t):
    nc = t // CMP_STRIDE
    cmp_start = np.arange(nc) * CMP_STRIDE
    sel_start = np.arange(t // SEL_BLOCK) * SEL_BLOCK
    ov = ((cmp_start[None, :] < sel_start[:, None] + SEL_BLOCK)
          & (cmp_start[None, :] + CMP_LEN > sel_start[:, None]))
    ov[:, nc - 1] = False
    out = np.zeros((t // SEL_BLOCK, CMP_FRONT + nc), np.float32)
    out[:, CMP_FRONT:] = ov
    return jnp.asarray(out, BF16)


def _s5_tables(a_re, a_im, log_dt, b_re, b_im, c_re, c_im, d_skip, batch):
    dt = jnp.exp(log_dt)[:, None]
    lam_re = jnp.minimum(a_re, -1e-4)
    lam_im = a_im
    mag = jnp.exp(lam_re * dt)
    ab_re = mag * jnp.cos(lam_im * dt)
    ab_im = mag * jnp.sin(lam_im * dt)
    den = lam_re * lam_re + lam_im * lam_im
    n_re = ab_re - 1.0
    n_im = ab_im
    co_re = (n_re * lam_re + n_im * lam_im) / den
    co_im = (n_im * lam_re - n_re * lam_im) / den
    bb_re = co_re[..., None] * b_re - co_im[..., None] * b_im
    bb_im = co_re[..., None] * b_im + co_im[..., None] * b_re
    sel = jnp.asarray(np.eye(4, dtype=np.float32)[np.arange(16) % 4])
    eye2 = jnp.eye(2, dtype=F32)
    bb = jnp.stack([bb_re, bb_im], 0).reshape(2, 2, 16, 2, SSM_STATE, SSM_GROUP)
    wbu = jnp.einsum('rhjgpc,jm,gk->jhmgcrkp', bb, sel, eye2).reshape(16, 256, 256)
    cc = jnp.stack([c_re, -c_im], 0).reshape(2, 2, 16, 2, SSM_GROUP, SSM_STATE)
    wc = jnp.einsum('rhjgcp,jm,gk->jrkphmgc', cc, sel, eye2).reshape(16, 256, 256)

    def lanes(a):
        a = a.reshape(2, 1, 16 * 2 * SSM_STATE)
        return jnp.broadcast_to(a.transpose(1, 0, 2), (batch, 2, 2048)).reshape(2 * batch, 2048)

    dmat = jnp.broadcast_to(d_skip.reshape(1, 2, 512), (batch, 2, 512)).reshape(2 * batch, 512)
    return wbu.astype(BF16), wc.astype(BF16), lanes(ab_re), lanes(ab_im), dmat


def kernel(x, ffn1_pre_g, ffn1_w_gate, ffn1_w_up, ffn1_w_down, ffn1_post_g, mix_pre_g, w_in, ssm_a_re, ssm_a_im, ssm_log_dt, ssm_b_re, ssm_b_im, ssm_c_re, ssm_c_im, ssm_d, ssm_glu_w1, ssm_glu_w2, cmp_pos, cmp_k_w1, cmp_k_w2, cmp_v_w1, cmp_v_w2, nsa_w_o, w_out, mix_post_g, ffn2_pre_g, ffn2_w_gate, ffn2_w_up, ffn2_w_down, ffn2_post_g, rel_bias):
    b, t, d = x.shape
    n = b * t
    assert b == 4 and t % TQ == 0 and t >= MAX_DIST + TQ and d == D_MODEL
    bf = lambda a: a.astype(BF16)
    h = x.reshape(n, d)
    for l in range(ffn1_pre_g.shape[0]):
        h, u = _ffn(h, ffn1_pre_g[l][None], bf(ffn1_w_gate[l]), bf(ffn1_w_up[l]), bf(ffn1_w_down[l]),
                    ffn1_post_g[l][None], mix_pre_g[l][None], True)

        w = w_in[l]
        o_q = SSM_WIDTH
        o_kv = o_q + N_HEADS * HEAD_DIM
        kvw = N_KV * HEAD_DIM
        o_gn = o_kv + 6 * kvw
        o_ga = o_gn + 3 * N_HEADS
        kv = lambda i: w[:, o_kv + i * kvw:o_kv + (i + 1) * kvw]
        wn = bf(jnp.concatenate([w[:, :o_q], kv(0), kv(1), kv(2), kv(4)], axis=1))
        gate_cols = np.full((N_KV, 4, HPG), -1)
        for g in range(N_KV):
            for br in range(3):
                for hh in range(HPG):
                    gate_cols[g, br, hh] = o_gn + 3 * HPG * g + 3 * hh + br
        gate_cols = gate_cols.reshape(-1)
        wg = jnp.where((gate_cols >= 0)[None, :], w[:, np.maximum(gate_cols, 0)], 0.0)
        wt = bf(jnp.concatenate([w[:, o_q:o_kv], kv(3), kv(5), wg], axis=1).T)
        pn, pt = _proj(u.reshape(b, t, d), wn, wt)

        us = pn[:, :, :SSM_WIDTH].reshape(b, t, 2, 1, 512).transpose(1, 0, 2, 3, 4)
        lhs = (us * jnp.eye(2, dtype=BF16)[None, None, :, :, None]).reshape(t * 2 * b, 1024)
        wbu, wc, a_re, a_im, dmat = _s5_tables(ssm_a_re[l], ssm_a_im[l], ssm_log_dt[l], ssm_b_re[l],
                                               ssm_b_im[l], ssm_c_re[l], ssm_c_im[l], ssm_d[l], b)
        ys = _s5(lhs, wbu, wc, a_re, a_im, dmat)
        ys = ys.reshape(t, b, 2, 512).transpose(1, 0, 2, 3).reshape(n, SSM_WIDTH)

        def rows16(a):
            a = a.reshape(b, t // CMP_STRIDE, CMP_STRIDE, N_KV, HEAD_DIM)
            return a.transpose(0, 3, 1, 2, 4).reshape(b, N_KV, t // CMP_STRIDE, CMP_STRIDE * HEAD_DIM)

        half = CMP_STRIDE * HEAD_DIM
        w1cat = lambda w1: bf(jnp.concatenate([w1[:half], w1[half:]], axis=1))
        pos8 = jnp.zeros((8, half), F32).at[0].set(cmp_pos[l][:CMP_STRIDE].reshape(-1))
        pos8 = bf(pos8.at[1].set(cmp_pos[l][CMP_STRIDE:].reshape(-1)))
        kc, vct = _compress(rows16(pn[:, :, 1024:1280]), rows16(pn[:, :, 1280:1536]), pos8,
                            w1cat(cmp_k_w1[l]), bf(cmp_k_w2[l]), w1cat(cmp_v_w1[l]), bf(cmp_v_w2[l].T))
        bsel, bwin, bcmp, cvec = _bias_tables(rel_bias, t)
        o_nsa = _nsa(pn, pt, kc, vct, bsel, bwin, bcmp, cvec, _overlap_t(t))

        z = _merge(ys, o_nsa.reshape(n, N_HEADS * HEAD_DIM), u, bf(ssm_glu_w1[l]), bf(ssm_glu_w2[l]),
                   bf(nsa_w_o[l]), bf(w[:, o_ga:o_ga + d]), bf(w[:, o_ga + d:o_ga + 2 * d]))
        h = _outproj(z, h, bf(w_out[l]), mix_post_g[l][None])

        h, _ = _ffn(h, ffn2_pre_g[l][None], bf(ffn2_w_gate[l]), bf(ffn2_w_up[l]), bf(ffn2_w_down[l]),
                    ffn2_post_g[l][None], ffn2_post_g[l][None], False)
    return h.reshape(b, t, d)
```

```python
import functools
import math

import numpy as np
import jax
import jax.numpy as jnp
from jax import lax
from jax.experimental import pallas as pl
from jax.experimental.pallas import tpu as pltpu

F32 = jnp.float32
BF16 = jnp.bfloat16

D_MODEL = 2048
D_FF = 5632
EPS = 1e-6
SSM_WIDTH = 1024
SSM_GROUP = 16
SSM_GROUPS = 64
SSM_STATE = 64
N_HEADS = 16
N_KV = 4
HPG = 4
HEAD_DIM = 64
CMP_LEN = 32
CMP_STRIDE = 16
CMP_HID = 256
SEL_BLOCK = 64
N_SEL = 16
WINDOW = 512
SEL_FORCE = 1e4
NEG_INF = -1e30
LOG2E = 1.4426950408889634
N_BUCKETS = 32
MAX_DIST = 1024

VMEM_LIMIT = 56 * 1024 * 1024

FFN_TM = 512
FFN_TF = 512
PROJ_TM = 512
S5_LC = 64
S5_CW = 512
TQ = 256
LQ = HPG * TQ
CMP_FRONT = 128
CMP_NEAR = 72
CMP_BAND = CMP_NEAR + TQ // CMP_STRIDE
N_SEL_BAND = MAX_DIST // TQ + 1
N_WIN = WINDOW // TQ + 1
MERGE_TM = 1024
MERGE_TN = 512
OUT_TM = 512


def _cparams(sem):
    return pltpu.CompilerParams(dimension_semantics=sem, vmem_limit_bytes=VMEM_LIMIT)


def _rms(x):
    ms = jnp.mean(x * x, axis=-1, keepdims=True)
    return x * lax.rsqrt(ms + EPS)


def _sigmoid(x):
    return 1.0 / (1.0 + jnp.exp(-x))


def _gelu(x):
    return 0.5 * x * (1.0 + jnp.tanh(math.sqrt(2.0 / math.pi) * (x + 0.044715 * (x * x * x))))


def _dot(a, b):
    return jnp.dot(a, b, preferred_element_type=F32)


def _dot_nt(a, b):
    return lax.dot_general(a, b, (((1,), (1,)), ((), ())), preferred_element_type=F32)


def _ffn_kernel(x_ref, pre_ref, wg_ref, wu_ref, wd_ref, post_ref, nxt_ref, *rest, emit_next):
    if emit_next:
        o_ref, u_ref, xn_ref, acc_ref = rest
    else:
        o_ref, xn_ref, acc_ref = rest
    f = pl.program_id(1)

    @pl.when(f == 0)
    def _():
        xn_ref[...] = (_rms(x_ref[...]) * pre_ref[...]).astype(BF16)
        acc_ref[...] = jnp.zeros_like(acc_ref)

    xn = xn_ref[...]
    g = _dot(xn, wg_ref[...])
    u = _dot(xn, wu_ref[...])
    a = (g * _sigmoid(g) * u).astype(BF16)
    acc_ref[...] += _dot(a, wd_ref[...])

    @pl.when(f == pl.num_programs(1) - 1)
    def _():
        h = x_ref[...] + 0.5 * (_rms(acc_ref[...]) * post_ref[...])
        o_ref[...] = h
        if emit_next:
            u_ref[...] = (_rms(h) * nxt_ref[...]).astype(BF16)


def _ffn(x, pre_g, wg, wu, wd, post_g, nxt_g, emit_next):
    n, d = x.shape
    nf = wg.shape[1]
    tm, tf = min(FFN_TM, n), FFN_TF
    out_shape = [jax.ShapeDtypeStruct((n, d), F32)]
    out_specs = [pl.BlockSpec((tm, d), lambda i, f: (i, 0))]
    if emit_next:
        out_shape.append(jax.ShapeDtypeStruct((n, d), BF16))
        out_specs.append(pl.BlockSpec((tm, d), lambda i, f: (i, 0)))
    vec = pl.BlockSpec((1, d), lambda i, f: (0, 0))
    res = pl.pallas_call(
        functools.partial(_ffn_kernel, emit_next=emit_next),
        out_shape=out_shape,
        grid=(n // tm, nf // tf),
        in_specs=[
            pl.BlockSpec((tm, d), lambda i, f: (i, 0)),
            vec,
            pl.BlockSpec((d, tf), lambda i, f: (0, f)),
            pl.BlockSpec((d, tf), lambda i, f: (0, f)),
            pl.BlockSpec((tf, d), lambda i, f: (f, 0)),
            vec,
            vec,
        ],
        out_specs=out_specs,
        scratch_shapes=[pltpu.VMEM((tm, d), BF16), pltpu.VMEM((tm, d), F32)],
        compiler_params=_cparams(("parallel", "arbitrary")),
        name="ffn",
    )(x, pre_g, wg, wu, wd, post_g, nxt_g)
    return res if emit_next else (res[0], None)


def _proj_kernel(u_ref, wn_ref, wt_ref, us_ref, pn_ref, pt_ref):
    u = u_ref[0]
    nat = _dot(u, wn_ref[...])
    us_ref[...] = nat[:, :SSM_WIDTH].astype(us_ref.dtype)
    pn_ref[0] = nat[:, SSM_WIDTH:].astype(pn_ref.dtype)
    pt_ref[0] = _dot_nt(wt_ref[...], u).astype(pt_ref.dtype)


def _proj(u, wn, wt):
    b, t, d = u.shape
    tm = min(PROJ_TM, t)
    nk = wn.shape[1] - SSM_WIDTH
    return pl.pallas_call(
        _proj_kernel,
        out_shape=[jax.ShapeDtypeStruct((t, b * SSM_WIDTH), BF16),
                   jax.ShapeDtypeStruct((b, t, nk), BF16),
                   jax.ShapeDtypeStruct((b, wt.shape[0], t), BF16)],
        grid=(b, t // tm),
        in_specs=[
            pl.BlockSpec((1, tm, d), lambda i, j: (i, j, 0)),
            pl.BlockSpec(wn.shape, lambda i, j: (0, 0)),
            pl.BlockSpec(wt.shape, lambda i, j: (0, 0)),
        ],
        out_specs=[
            pl.BlockSpec((tm, SSM_WIDTH), lambda i, j: (j, i)),
            pl.BlockSpec((1, tm, nk), lambda i, j: (i, j, 0)),
            pl.BlockSpec((1, wt.shape[0], tm), lambda i, j: (i, 0, j)),
        ],
        compiler_params=_cparams(("parallel", "parallel")),
        name="proj",
    )(u, wn, wt)


def _s5_kernel(lhs_ref, wbu_ref, wc_ref, are_ref, aim_ref, dmat_ref, y_ref,
               xre_ref, xim_ref, bre_ref, bim_ref):
    lc = lhs_ref.shape[0] // 8

    @pl.when(pl.program_id(0) == 0)
    def _():
        xre_ref[...] = jnp.zeros_like(xre_ref)
        xim_ref[...] = jnp.zeros_like(xim_ref)

    row_odd = (lax.broadcasted_iota(jnp.int32, (lhs_ref.shape[0], 1), 0) % 2) == 1
    for j in range(16):
        jq = j // 4
        slab = lhs_ref[:, 128 * jq:128 * (jq + 1)]
        zero = jnp.zeros_like(slab)
        lj = jnp.concatenate([jnp.where(row_odd, zero, slab), jnp.where(row_odd, slab, zero)], axis=1)
        out = _dot(lj, wbu_ref[j])
        bre_ref[:, 128 * j:128 * (j + 1)] = out[:, :128]
        bim_ref[:, 128 * j:128 * (j + 1)] = out[:, 128:]

    for cc in range(bre_ref.shape[1] // S5_CW):
        cs = slice(cc * S5_CW, (cc + 1) * S5_CW)
        ar = are_ref[:, cs]
        ai = aim_ref[:, cs]

        def step(i, carry, cs=cs, ar=ar, ai=ai):
            xr, xi = carry
            rows = pl.ds(pl.multiple_of(i * 8, 8), 8)
            nr = ar * xr - ai * xi + bre_ref[rows, cs]
            ni = ar * xi + ai * xr + bim_ref[rows, cs]
            bre_ref[rows, cs] = nr
            bim_ref[rows, cs] = ni
            return nr, ni

        xr, xi = lax.fori_loop(0, lc, step, (xre_ref[:, cs], xim_ref[:, cs]))
        xre_ref[:, cs] = xr
        xim_ref[:, cs] = xi

    odd = (lax.broadcasted_iota(jnp.int32, (1, 8, 1), 1) % 2) == 1
    for jq in range(4):
        acc = None
        for jm in range(4):
            j = 4 * jq + jm
            xj = jnp.concatenate([bre_ref[:, 128 * j:128 * (j + 1)],
                                  bim_ref[:, 128 * j:128 * (j + 1)]], axis=1).astype(BF16)
            part = _dot(xj, wc_ref[j])
            acc = part if acc is None else acc + part
        acc3 = acc.reshape(lc, 8, 256)
        y = jnp.where(odd, acc3[:, :, 128:], acc3[:, :, :128])
        u_own = lhs_ref[:, 128 * jq:128 * (jq + 1)].astype(F32).reshape(lc, 8, 128)
        y = y + dmat_ref[:, 128 * jq:128 * (jq + 1)][None] * u_own
        y_ref[:, 128 * jq:128 * (jq + 1)] = _gelu(y).reshape(lc * 8, 128).astype(y_ref.dtype)


def _s5(lhs, wbu, wc, a_re, a_im, dmat):
    rows = lhs.shape[0]
    r = 8 * S5_LC
    nl = a_re.shape[1]
    return pl.pallas_call(
        _s5_kernel,
        out_shape=jax.ShapeDtypeStruct((rows, 512), BF16),
        grid=(rows // r,),
        in_specs=[
            pl.BlockSpec((r, 512), lambda c: (c, 0)),
            pl.BlockSpec(wbu.shape, lambda c: (0, 0, 0)),
            pl.BlockSpec(wc.shape, lambda c: (0, 0, 0)),
            pl.BlockSpec(a_re.shape, lambda c: (0, 0)),
            pl.BlockSpec(a_im.shape, lambda c: (0, 0)),
            pl.BlockSpec(dmat.shape, lambda c: (0, 0)),
        ],
        out_specs=pl.BlockSpec((r, 512), lambda c: (c, 0)),
        scratch_shapes=[pltpu.VMEM((8, nl), F32), pltpu.VMEM((8, nl), F32),
                        pltpu.VMEM((r, nl), F32), pltpu.VMEM((r, nl), F32)],
        compiler_params=_cparams(("arbitrary",)),
        name="s5",
    )(lhs, wbu, wc, a_re, a_im, dmat)


def _compress_kernel(rk_ref, rv_ref, pos_ref, w1k_ref, w2k_ref, w1v_ref, w2vt_ref, k_ref, vt_ref):
    m = rk_ref.shape[2]

    def hidden(r_ref, w1_ref):
        z = _dot(r_ref[0, 0], w1_ref[...])
        pz = _dot(pos_ref[...], w1_ref[...])
        posvec = pz[0:1, :CMP_HID] + pz[1:2, CMP_HID:]
        zhi = pltpu.roll(z[:, CMP_HID:], m - 1, 0)
        return _gelu(z[:, :CMP_HID] + zhi + posvec).astype(BF16)

    hk = hidden(rk_ref, w1k_ref)
    k_ref[0, 0, :CMP_FRONT, :] = jnp.zeros((CMP_FRONT, HEAD_DIM), k_ref.dtype)
    k_ref[0, 0, CMP_FRONT:, :] = _dot(hk, w2k_ref[...]).astype(k_ref.dtype)
    hv = hidden(rv_ref, w1v_ref)
    vt_ref[0, 0, :, :CMP_FRONT] = jnp.zeros((HEAD_DIM, CMP_FRONT), vt_ref.dtype)
    vt_ref[0, 0, :, CMP_FRONT:] = _dot_nt(w2vt_ref[...], hv).astype(vt_ref.dtype)


def _compress(rk, rv, pos8, w1k, w2k, w1v, w2vt):
    b, g, m, kd = rk.shape
    ncp = CMP_FRONT + m
    full = lambda a: pl.BlockSpec(a.shape, lambda i, j: (0,) * a.ndim)
    return pl.pallas_call(
        _compress_kernel,
        out_shape=[jax.ShapeDtypeStruct((b, g, ncp, HEAD_DIM), BF16),
                   jax.ShapeDtypeStruct((b, g, HEAD_DIM, ncp), BF16)],
        grid=(b, g),
        in_specs=[
            pl.BlockSpec((1, 1, m, kd), lambda i, j: (i, j, 0, 0)),
            pl.BlockSpec((1, 1, m, kd), lambda i, j: (i, j, 0, 0)),
            full(pos8), full(w1k), full(w2k), full(w1v), full(w2vt),
        ],
        out_specs=[
            pl.BlockSpec((1, 1, ncp, HEAD_DIM), lambda i, j: (i, j, 0, 0)),
            pl.BlockSpec((1, 1, HEAD_DIM, ncp), lambda i, j: (i, j, 0, 0)),
        ],
        compiler_params=_cparams(("parallel", "parallel")),
        name="compress",
    )(rk, rv, pos8, w1k, w2k, w1v, w2vt)


def _nsa_kernel(qt_ref, gt_ref, kc_ref, vct_ref, ksl_ref, vslt_ref,
                kw0_ref, kw1_ref, kw2_ref, vw0_ref, vw1_ref, vw2_ref,
                bsel_ref, bwin_ref, bcmp_ref, cvec_ref, ovl_ref,
                o_ref,
                qpad_ref, sc_ref, nm_ref, m_ref, l_ref, acc_ref, oc_ref, os_ref):
    g = pl.program_id(1)
    nq = pl.program_id(2)
    t0 = nq * TQ
    ncp = kc_ref.shape[2]
    ns = nm_ref.shape[0]
    cpq = TQ // CMP_STRIDE
    bpt = TQ // SEL_BLOCK

    q4 = qt_ref[0]
    qpad_ref[...] = jnp.zeros_like(qpad_ref)
    grow = pl.ds(pl.multiple_of(g * HEAD_DIM, HEAD_DIM), HEAD_DIM)
    for h in range(HPG):
        qpad_ref[grow, TQ * h:TQ * (h + 1)] = q4[HEAD_DIM * h:HEAD_DIM * (h + 1)]
    heads = [slice(TQ * h, TQ * (h + 1)) for h in range(HPG)]
    t_tok = t0 + lax.broadcasted_iota(jnp.int32, (1, TQ), 1)

    row = lax.broadcasted_iota(jnp.int32, (ncp, 1), 0)
    live = (row >= CMP_FRONT) & (row < cpq * nq + CMP_FRONT + cpq)
    for h, hs in enumerate(heads):
        s = _dot(kc_ref[0, 0], q4[HEAD_DIM * h:HEAD_DIM * (h + 1)])
        sc_ref[:, hs] = s + jnp.where(live, cvec_ref[0, :, hs], NEG_INF)
    band = pl.ds(pl.multiple_of(cpq * nq + CMP_FRONT - CMP_NEAR, 8), CMP_BAND)
    sc_ref[band, :] = sc_ref[band, :] + bcmp_ref[0]
    has_cmp = (t_tok >= CMP_LEN - 1).astype(F32)
    ps = None
    for h, hs in enumerate(heads):
        s = sc_ref[:, hs]
        e = jnp.exp2(s - jnp.max(s, axis=0, keepdims=True))
        p = e * (has_cmp / jnp.sum(e, axis=0, keepdims=True))
        oc_ref[:, hs] = _dot(vct_ref[0, 0], p.astype(BF16))
        ps = p if ps is None else ps + p

    ps_hi = ps.astype(BF16)
    ps_lo = (ps - ps_hi.astype(F32)).astype(BF16)
    imp = _dot(ovl_ref[...], ps_hi) + _dot(ovl_ref[...], ps_lo)
    blk = lax.broadcasted_iota(jnp.int32, (ns, TQ), 0)
    tt = t0 + lax.broadcasted_iota(jnp.int32, (ns, TQ), 1)
    valid_b = blk * SEL_BLOCK <= tt
    cur = tt // SEL_BLOCK
    forced = valid_b & ((blk == 0) | (blk == cur) | (blk == cur - 1))
    score = jnp.where(forced, SEL_FORCE, jnp.where(valid_b, imp, -SEL_FORCE))
    blkf = blk.astype(F32)

    def pick_one(_, carry):
        sc, chosen = carry
        mx = jnp.max(sc, axis=0, keepdims=True)
        first = jnp.min(jnp.where(sc == mx, blkf, float(ns)), axis=0, keepdims=True)
        hit = blkf == first
        return jnp.where(hit, -3e38, sc), jnp.where(hit, 1.0, chosen)

    _, chosen = lax.fori_loop(0, min(N_SEL, ns), pick_one, (score, jnp.zeros((ns, TQ), F32)))
    nm_ref[...] = jnp.where(chosen > 0.0, 0.0, NEG_INF)

    def reset():
        m_ref[...] = jnp.full_like(m_ref, NEG_INF)
        l_ref[...] = jnp.zeros_like(l_ref)
        acc_ref[...] = jnp.zeros_like(acc_ref)

    ones_rows = jnp.ones((16, TQ), BF16)

    def online(scores, vt, masks=None, shift=None):
        m_old = m_ref[...]
        vt1 = jnp.concatenate([vt, ones_rows], axis=0)
        nblk = scores[0].shape[0] // SEL_BLOCK
        if masks is not None:
            caps = [jnp.where(mk < 0.5 * NEG_INF, 0.0, 3e38).astype(BF16) for mk in masks]
        m_new, pv = [], []
        for h, (hs, s) in enumerate(zip(heads, scores)):
            if masks is None:
                smax = jnp.max(s, axis=0, keepdims=True)
            else:
                smax = None
                for jj in range(nblk):
                    bm = jnp.max(s[SEL_BLOCK * jj:SEL_BLOCK * (jj + 1)], axis=0, keepdims=True) + masks[jj]
                    smax = bm if smax is None else jnp.maximum(smax, bm)
            if shift is not None:
                smax = smax + shift[h]
            mh = jnp.maximum(m_old[:, hs], smax)
            p = jnp.exp2(s - (mh if shift is None else mh - shift[h])).astype(BF16)
            if masks is not None:
                p = jnp.concatenate([jnp.minimum(p[SEL_BLOCK * jj:SEL_BLOCK * (jj + 1)], caps[jj])
                                     for jj in range(nblk)], axis=0)
            m_new.append(mh)
            pv.append(_dot(vt1, p))
        m_new = jnp.concatenate(m_new, axis=1)
        pv = jnp.concatenate(pv, axis=1)
        alpha = jnp.exp2(m_old - m_new)
        m_ref[...] = m_new
        l_ref[...] = alpha * l_ref[...] + pv[HEAD_DIM:HEAD_DIM + 1]
        acc_ref[...] = alpha * acc_ref[...] + pv[:HEAD_DIM]

    def sel_tile(kt, r0):
        k0 = pl.multiple_of(kt * TQ, TQ)
        keys = ksl_ref[0, pl.ds(k0, TQ), :]
        masks = [nm_ref[pl.ds(kt * bpt + jj, 1), :] for jj in range(bpt)]
        scores = [_dot(keys, qpad_ref[:, hs]) for hs in heads]
        shift = None
        if r0 is None:
            shift = [cvec_ref[0, :, hs] for hs in heads]
        else:
            scores = [s + bsel_ref[0, pl.ds(r0, TQ), hs] for s, hs in zip(scores, heads)]
        online(scores, vslt_ref[0, :, pl.ds(k0, TQ)], masks, shift)

    reset()
    n_far = jnp.maximum(nq - (N_SEL_BAND - 1), 0)

    def far_body(kt, c):
        sel_tile(kt, None)
        return c

    lax.fori_loop(0, n_far, far_body, 0)

    def near_body(kt, c):
        sel_tile(kt, pl.multiple_of(MAX_DIST - TQ * (nq - kt), TQ))
        return c

    lax.fori_loop(n_far, nq + 1, near_body, 0)
    os_ref[...] = acc_ref[...] / l_ref[...]

    reset()
    kws = (kw0_ref, kw1_ref, kw2_ref)
    vws = (vw0_ref, vw1_ref, vw2_ref)
    for d in range(N_WIN):
        @pl.when(nq >= d)
        def _(d=d):
            keys = kws[d][0]
            online([_dot(keys, qpad_ref[:, hs]) + bwin_ref[0, TQ * (N_WIN - 1 - d):TQ * (N_WIN - d), hs]
                    for hs in heads], vws[d][0])
    o_w = acc_ref[...] / l_ref[...]

    sg = _sigmoid(gt_ref[0].astype(F32))

    def gate(br):
        return jnp.concatenate([sg[HPG * br + h:HPG * br + h + 1] for h in range(HPG)], axis=1)

    ot = gate(0) * oc_ref[...] + gate(1) * os_ref[...] + gate(2) * o_w
    stacked = jnp.concatenate([ot[:, TQ * h:TQ * (h + 1)] for h in range(HPG)], axis=0)
    o_ref[0] = stacked.T.astype(o_ref.dtype)


def _nsa(pn, pt, kc, vct, bsel, bwin, bcmp, cvec, ovl):
    b, t, _ = pn.shape
    nq = t // TQ
    ncp = kc.shape[2]
    ns = t // SEL_BLOCK
    q_rows = N_HEADS * HEAD_DIM
    gate_blk0 = (q_rows + 2 * N_KV * HEAD_DIM) // 16
    vsl_blk0 = q_rows // HEAD_DIM
    vw_blk0 = vsl_blk0 + N_KV
    ksl_col, kw_col = 2, 3

    def kw_spec(d):
        return pl.BlockSpec((1, TQ, 256), lambda i, j, n: (i, jnp.maximum(n - d, 0), kw_col))

    def vw_spec(d):
        return pl.BlockSpec((1, HEAD_DIM, TQ), lambda i, j, n: (i, vw_blk0 + j, jnp.maximum(n - d, 0)))

    per_g = lambda a: pl.BlockSpec((1,) + a.shape[1:], lambda i, j, n: (j,) + (0,) * (a.ndim - 1))
    return pl.pallas_call(
        _nsa_kernel,
        out_shape=jax.ShapeDtypeStruct((b, t, q_rows), BF16),
        grid=(b, N_KV, nq),
        in_specs=[
            pl.BlockSpec((1, HPG * HEAD_DIM, TQ), lambda i, j, n: (i, j, n)),
            pl.BlockSpec((1, 16, TQ), lambda i, j, n: (i, gate_blk0 + j, n)),
            pl.BlockSpec((1, 1, ncp, HEAD_DIM), lambda i, j, n: (i, j, 0, 0)),
            pl.BlockSpec((1, 1, HEAD_DIM, ncp), lambda i, j, n: (i, j, 0, 0)),
            pl.BlockSpec((1, t, 256), lambda i, j, n: (i, 0, ksl_col)),
            pl.BlockSpec((1, HEAD_DIM, t), lambda i, j, n: (i, vsl_blk0 + j, 0)),
            kw_spec(0), kw_spec(1), kw_spec(2),
            vw_spec(0), vw_spec(1), vw_spec(2),
            per_g(bsel), per_g(bwin), per_g(bcmp), per_g(cvec),
            pl.BlockSpec(ovl.shape, lambda i, j, n: (0, 0)),
        ],
        out_specs=pl.BlockSpec((1, TQ, HPG * HEAD_DIM), lambda i, j, n: (i, n, j)),
        scratch_shapes=[
            pltpu.VMEM((N_KV * HEAD_DIM, LQ), BF16),
            pltpu.VMEM((ncp, LQ), F32),
            pltpu.VMEM((ns, TQ), F32),
            pltpu.VMEM((1, LQ), F32), pltpu.VMEM((1, LQ), F32), pltpu.VMEM((HEAD_DIM, LQ), F32),
            pltpu.VMEM((HEAD_DIM, LQ), F32), pltpu.VMEM((HEAD_DIM, LQ), F32),
        ],
        compiler_params=_cparams(("parallel", "parallel", "arbitrary")),
        name="nsa",
    )(pt, pt, kc, vct, pn, pt, pn, pn, pn, pt, pt, pt, bsel, bwin, bcmp, cvec, ovl)


def _merge_kernel(ys_ref, o_ref, u_ref, w1_ref, w2_ref, wo_ref, wga_ref, wgb_ref, z_ref):
    ys = ys_ref[...]
    ya = _dot(ys, w1_ref[...]) * _sigmoid(_dot(ys, w2_ref[...]))
    yb = _dot(o_ref[...], wo_ref[...])
    u = u_ref[...]
    z = _sigmoid(_dot(u, wga_ref[...])) * ya + _sigmoid(_dot(u, wgb_ref[...])) * yb
    z_ref[...] = z.astype(z_ref.dtype)


def _merge(ys, o, u, w1, w2, wo, wga, wgb):
    t = ys.shape[0]
    n = o.shape[0]
    d = w1.shape[1]
    tm, tn = min(MERGE_TM, t), MERGE_TN
    tpb = t // tm
    row = lambda a: pl.BlockSpec((tm, a.shape[1]), lambda i, j: (i, 0))
    col = lambda a: pl.BlockSpec((a.shape[0], tn), lambda i, j: (0, j))
    return pl.pallas_call(
        _merge_kernel,
        out_shape=jax.ShapeDtypeStruct((n, d), BF16),
        grid=(n // tm, d // tn),
        in_specs=[pl.BlockSpec((tm, w1.shape[0]), lambda i, j: (i % tpb, i // tpb)),
                  row(o), row(u), col(w1), col(w2), col(wo), col(wga), col(wgb)],
        out_specs=pl.BlockSpec((tm, tn), lambda i, j: (i, j)),
        compiler_params=_cparams(("parallel", "parallel")),
        name="merge",
    )(ys, o, u, w1, w2, wo, wga, wgb)


def _outproj_kernel(z_ref, h_ref, w_ref, g_ref, o_ref):
    mixed = _dot(z_ref[...], w_ref[...])
    o_ref[...] = h_ref[...] + _rms(mixed) * g_ref[...]


def _outproj(z, h, w, g):
    n, d = h.shape
    tm = min(OUT_TM, n)
    return pl.pallas_call(
        _outproj_kernel,
        out_shape=jax.ShapeDtypeStruct((n, d), F32),
        grid=(n // tm,),
        in_specs=[
            pl.BlockSpec((tm, d), lambda i: (i, 0)),
            pl.BlockSpec((tm, d), lambda i: (i, 0)),
            pl.BlockSpec(w.shape, lambda i: (0, 0)),
            pl.BlockSpec((1, d), lambda i: (0, 0)),
        ],
        out_specs=pl.BlockSpec((tm, d), lambda i: (i, 0)),
        compiler_params=_cparams(("parallel",)),
        name="outproj",
    )(z, h, w, g)


def _rel_bucket(dist):
    dist = jnp.maximum(dist, 0)
    max_exact = N_BUCKETS // 2
    d_f = jnp.maximum(dist, 1).astype(jnp.float32)
    large = max_exact + (jnp.log(d_f / max_exact) / math.log(MAX_DIST / max_exact)
                         * (N_BUCKETS - max_exact)).astype(jnp.int32)
    large = jnp.minimum(large, N_BUCKETS - 1)
    return jnp.where(dist < max_exact, dist, large)


def _toeplitz(yfun, offset, rows, cols):
    p = rows + cols
    slot = jnp.arange(p)
    v = yfun(jnp.where(slot < cols, slot, slot - p) + offset)
    m = jnp.tile(v, (1, rows))[:, :rows * (p - 1)].reshape(v.shape[0], rows, p - 1)
    return m[:, :, :cols]


def _bias_tables(rel_bias):
    n_dist = MAX_DIST + 2 * TQ
    by_dist = (LOG2E * rel_bias[_rel_bucket(jnp.arange(n_dist))]).T
    far = LOG2E * rel_bias[_rel_bucket(jnp.asarray(MAX_DIST))]

    def lookup(k, ok, shift=0.0):
        return jnp.where(ok[None, :], by_dist[:, jnp.clip(k, 0, n_dist - 1)] - shift, NEG_INF)

    def grouped(m):
        rows = m.shape[1]
        return m.reshape(N_KV, HPG, rows, TQ).transpose(0, 2, 1, 3).reshape(N_KV, rows, LQ)

    bsel = _toeplitz(lambda k: lookup(k, k >= 0), MAX_DIST, MAX_DIST + TQ, TQ)
    bwin = _toeplitz(lambda k: lookup(k, (k >= 0) & (k < WINDOW)), WINDOW, WINDOW + TQ, TQ)
    off = CMP_STRIDE * CMP_NEAR - (CMP_LEN - 1)
    per_c = [_toeplitz(lambda k, c=c: lookup(CMP_STRIDE * k + c + off, CMP_STRIDE * k + c + off >= 0,
                                             far[:, None]),
                       0, CMP_BAND, TQ // CMP_STRIDE) for c in range(CMP_STRIDE)]
    bcmp = jnp.stack(per_c, axis=-1).reshape(N_HEADS, CMP_BAND, TQ)
    cvec = jnp.broadcast_to(far.reshape(N_KV, 1, HPG, 1), (N_KV, 1, HPG, TQ)).reshape(N_KV, 1, LQ)
    return grouped(bsel), grouped(bwin), grouped(bcmp), cvec


def _overlap_t(t):
    nc = t // CMP_STRIDE
    cmp_start = np.arange(nc) * CMP_STRIDE
    sel_start = np.arange(t // SEL_BLOCK) * SEL_BLOCK
    ov = ((cmp_start[None, :] < sel_start[:, None] + SEL_BLOCK)
          & (cmp_start[None, :] + CMP_LEN > sel_start[:, None]))
    ov[:, nc - 1] = False
    out = np.zeros((t // SEL_BLOCK, CMP_FRONT + nc), np.float32)
    out[:, CMP_FRONT:] = ov
    return jnp.asarray(out, BF16)


def _s5_tables(a_re, a_im, log_dt, b_re, b_im, c_re, c_im, d_skip, batch):
    dt = jnp.exp(log_dt)[:, None]
    lam_re = jnp.minimum(a_re, -1e-4)
    lam_im = a_im
    mag = jnp.exp(lam_re * dt)
    ab_re = mag * jnp.cos(lam_im * dt)
    ab_im = mag * jnp.sin(lam_im * dt)
    den = lam_re * lam_re + lam_im * lam_im
    n_re = ab_re - 1.0
    n_im = ab_im
    co_re = (n_re * lam_re + n_im * lam_im) / den
    co_im = (n_im * lam_re - n_re * lam_im) / den
    bb_re = co_re[..., None] * b_re - co_im[..., None] * b_im
    bb_im = co_re[..., None] * b_im + co_im[..., None] * b_re
    sel = jnp.asarray(np.eye(4, dtype=np.float32)[np.arange(16) % 4])
    eye2 = jnp.eye(2, dtype=F32)
    bb = jnp.stack([bb_re, bb_im], 0).reshape(2, 2, 16, 2, SSM_STATE, SSM_GROUP)
    wbu = jnp.einsum('rhjgpc,jm,gk->jhmgcrkp', bb, sel, eye2).reshape(16, 256, 256)
    cc = jnp.stack([c_re, -c_im], 0).reshape(2, 2, 16, 2, SSM_GROUP, SSM_STATE)
    wc = jnp.einsum('rhjgcp,jm,gk->jrkphmgc', cc, sel, eye2).reshape(16, 256, 256)

    def lanes(a):
        a = a.reshape(2, 1, 16 * 2 * SSM_STATE)
        return jnp.broadcast_to(a.transpose(1, 0, 2), (batch, 2, 2048)).reshape(2 * batch, 2048)

    dmat = jnp.broadcast_to(d_skip.reshape(1, 2, 512), (batch, 2, 512)).reshape(2 * batch, 512)
    return wbu.astype(BF16), wc.astype(BF16), lanes(ab_re), lanes(ab_im), dmat


def kernel(x, ffn1_pre_g, ffn1_w_gate, ffn1_w_up, ffn1_w_down, ffn1_post_g, mix_pre_g, w_in, ssm_a_re, ssm_a_im, ssm_log_dt, ssm_b_re, ssm_b_im, ssm_c_re, ssm_c_im, ssm_d, ssm_glu_w1, ssm_glu_w2, cmp_pos, cmp_k_w1, cmp_k_w2, cmp_v_w1, cmp_v_w2, nsa_w_o, w_out, mix_post_g, ffn2_pre_g, ffn2_w_gate, ffn2_w_up, ffn2_w_down, ffn2_post_g, rel_bias):
    b, t, d = x.shape
    n = b * t
    assert b == 4 and t % TQ == 0 and t >= MAX_DIST + TQ and d == D_MODEL
    bf = lambda a: a.astype(BF16)
    h = x.reshape(n, d)
    for l in range(ffn1_pre_g.shape[0]):
        h, u = _ffn(h, ffn1_pre_g[l][None], bf(ffn1_w_gate[l]), bf(ffn1_w_up[l]), bf(ffn1_w_down[l]),
                    ffn1_post_g[l][None], mix_pre_g[l][None], True)

        w = w_in[l]
        o_q = SSM_WIDTH
        o_kv = o_q + N_HEADS * HEAD_DIM
        kvw = N_KV * HEAD_DIM
        o_gn = o_kv + 6 * kvw
        o_ga = o_gn + 3 * N_HEADS
        kv = lambda i: w[:, o_kv + i * kvw:o_kv + (i + 1) * kvw]
        wn = bf(jnp.concatenate([w[:, :o_q], kv(0), kv(1), kv(2), kv(4)], axis=1))
        gate_cols = np.full((N_KV, 4, HPG), -1)
        for g in range(N_KV):
            for br in range(3):
                for hh in range(HPG):
                    gate_cols[g, br, hh] = o_gn + 3 * HPG * g + 3 * hh + br
        gate_cols = gate_cols.reshape(-1)
        wg = jnp.where((gate_cols >= 0)[None, :], w[:, np.maximum(gate_cols, 0)], 0.0)
        w_q = w[:, o_q:o_kv] * (HEAD_DIM ** -0.5 * LOG2E)
        wt = bf(jnp.concatenate([w_q, kv(3), kv(5), wg], axis=1).T)
        us, pn, pt = _proj(u.reshape(b, t, d), wn, wt)

        wbu, wc, a_re, a_im, dmat = _s5_tables(ssm_a_re[l], ssm_a_im[l], ssm_log_dt[l], ssm_b_re[l],
                                               ssm_b_im[l], ssm_c_re[l], ssm_c_im[l], ssm_d[l], b)
        ys = _s5(us.reshape(t * 2 * b, 512), wbu, wc, a_re, a_im, dmat).reshape(t, b * SSM_WIDTH)

        def rows16(a):
            a = a.reshape(b, t // CMP_STRIDE, CMP_STRIDE, N_KV, HEAD_DIM)
            return a.transpose(0, 3, 1, 2, 4).reshape(b, N_KV, t // CMP_STRIDE, CMP_STRIDE * HEAD_DIM)

        half = CMP_STRIDE * HEAD_DIM
        w1cat = lambda w1: bf(jnp.concatenate([w1[:half], w1[half:]], axis=1))
        pos8 = jnp.zeros((8, half), F32).at[0].set(cmp_pos[l][:CMP_STRIDE].reshape(-1))
        pos8 = bf(pos8.at[1].set(cmp_pos[l][CMP_STRIDE:].reshape(-1)))
        kc, vct = _compress(rows16(pn[:, :, :kvw]), rows16(pn[:, :, kvw:2 * kvw]), pos8,
                            w1cat(cmp_k_w1[l]), bf(cmp_k_w2[l]), w1cat(cmp_v_w1[l]), bf(cmp_v_w2[l].T))
        bsel, bwin, bcmp, cvec = _bias_tables(rel_bias)
        o_nsa = _nsa(pn, pt, kc, vct, bsel, bwin, bcmp, cvec, _overlap_t(t))

        z = _merge(ys, o_nsa.reshape(n, N_HEADS * HEAD_DIM), u, bf(ssm_glu_w1[l]), bf(ssm_glu_w2[l]),
                   bf(nsa_w_o[l]), bf(w[:, o_ga:o_ga + d]), bf(w[:, o_ga + d:o_ga + 2 * d]))
        h = _outproj(z, h, bf(w_out[l]), mix_post_g[l][None])

        h, _ = _ffn(h, ffn2_pre_g[l][None], bf(ffn2_w_gate[l]), bf(ffn2_w_up[l]), bf(ffn2_w_down[l]),
                    ffn2_post_g[l][None], ffn2_post_g[l][None], False)
    return h.reshape(b, t, d)
```

```python
import functools
import math

import numpy as np
import jax
import jax.numpy as jnp
from jax import lax
from jax.experimental import pallas as pl
from jax.experimental.pallas import tpu as pltpu

F32 = jnp.float32
BF16 = jnp.bfloat16

D_MODEL = 2048
D_FF = 5632
EPS = 1e-6
SSM_WIDTH = 1024
SSM_GROUP = 16
SSM_GROUPS = 64
SSM_STATE = 64
N_HEADS = 16
N_KV = 4
HPG = 4
HEAD_DIM = 64
CMP_LEN = 32
CMP_STRIDE = 16
CMP_HID = 256
SEL_BLOCK = 64
N_SEL = 16
WINDOW = 512
SEL_FORCE = 1e4
NEG_INF = -1e30
LOG2E = 1.4426950408889634
N_BUCKETS = 32
MAX_DIST = 1024

VMEM_LIMIT = 56 * 1024 * 1024

FFN_TM = 512
FFN_TF = 512
PROJ_TM = 512
S5_LC = 64
S5_CW = 512
TQ = 256
LQ = HPG * TQ
CMP_FRONT = 128
CMP_NEAR = 72
CMP_BAND = CMP_NEAR + TQ // CMP_STRIDE
N_SEL_BAND = MAX_DIST // TQ + 1
N_WIN = WINDOW // TQ + 1
MERGE_TM = 1024
MERGE_TN = 512
OUT_TM = 512


def _cparams(sem):
    return pltpu.CompilerParams(dimension_semantics=sem, vmem_limit_bytes=VMEM_LIMIT)


def _rms(x):
    ms = jnp.mean(x * x, axis=-1, keepdims=True)
    return x * lax.rsqrt(ms + EPS)


def _sigmoid(x):
    return 1.0 / (1.0 + jnp.exp(-x))


def _gelu(x):
    return 0.5 * x * (1.0 + jnp.tanh(math.sqrt(2.0 / math.pi) * (x + 0.044715 * (x * x * x))))


def _dot(a, b):
    return jnp.dot(a, b, preferred_element_type=F32)


def _dot_nt(a, b):
    return lax.dot_general(a, b, (((1,), (1,)), ((), ())), preferred_element_type=F32)


def _ffn_kernel(x_ref, pre_ref, wg_ref, wu_ref, wd_ref, post_ref, nxt_ref, *rest, emit_next):
    if emit_next:
        o_ref, u_ref, xn_ref, acc_ref = rest
    else:
        o_ref, xn_ref, acc_ref = rest
    f = pl.program_id(1)

    @pl.when(f == 0)
    def _():
        xn_ref[...] = (_rms(x_ref[...]) * pre_ref[...]).astype(BF16)
        acc_ref[...] = jnp.zeros_like(acc_ref)

    xn = xn_ref[...]
    g = _dot(xn, wg_ref[...])
    u = _dot(xn, wu_ref[...])
    a = (g * _sigmoid(g) * u).astype(BF16)
    acc_ref[...] += _dot(a, wd_ref[...])

    @pl.when(f == pl.num_programs(1) - 1)
    def _():
        h = x_ref[...] + 0.5 * (_rms(acc_ref[...]) * post_ref[...])
        o_ref[...] = h
        if emit_next:
            u_ref[...] = (_rms(h) * nxt_ref[...]).astype(BF16)


def _ffn(x, pre_g, wg, wu, wd, post_g, nxt_g, emit_next):
    n, d = x.shape
    nf = wg.shape[1]
    tm, tf = min(FFN_TM, n), FFN_TF
    out_shape = [jax.ShapeDtypeStruct((n, d), F32)]
    out_specs = [pl.BlockSpec((tm, d), lambda i, f: (i, 0))]
    if emit_next:
        out_shape.append(jax.ShapeDtypeStruct((n, d), BF16))
        out_specs.append(pl.BlockSpec((tm, d), lambda i, f: (i, 0)))
    vec = pl.BlockSpec((1, d), lambda i, f: (0, 0))
    res = pl.pallas_call(
        functools.partial(_ffn_kernel, emit_next=emit_next),
        out_shape=out_shape,
        grid=(n // tm, nf // tf),
        in_specs=[
            pl.BlockSpec((tm, d), lambda i, f: (i, 0)),
            vec,
            pl.BlockSpec((d, tf), lambda i, f: (0, f)),
            pl.BlockSpec((d, tf), lambda i, f: (0, f)),
            pl.BlockSpec((tf, d), lambda i, f: (f, 0)),
            vec,
            vec,
        ],
        out_specs=out_specs,
        scratch_shapes=[pltpu.VMEM((tm, d), BF16), pltpu.VMEM((tm, d), F32)],
        compiler_params=_cparams(("parallel", "arbitrary")),
        name="ffn",
    )(x, pre_g, wg, wu, wd, post_g, nxt_g)
    return res if emit_next else (res[0], None)


def _proj_kernel(u_ref, wn_ref, wt_ref, us_ref, pn_ref, pt_ref):
    u = u_ref[0]
    nat = _dot(u, wn_ref[...])
    us_ref[...] = nat[:, :SSM_WIDTH].astype(us_ref.dtype)
    pn_ref[0] = nat[:, SSM_WIDTH:].astype(pn_ref.dtype)
    pt_ref[0] = _dot_nt(wt_ref[...], u).astype(pt_ref.dtype)


def _proj(u, wn, wt):
    b, t, d = u.shape
    tm = min(PROJ_TM, t)
    nk = wn.shape[1] - SSM_WIDTH
    return pl.pallas_call(
        _proj_kernel,
        out_shape=[jax.ShapeDtypeStruct((t, b * SSM_WIDTH), BF16),
                   jax.ShapeDtypeStruct((b, t, nk), BF16),
                   jax.ShapeDtypeStruct((b, wt.shape[0], t), BF16)],
        grid=(b, t // tm),
        in_specs=[
            pl.BlockSpec((1, tm, d), lambda i, j: (i, j, 0)),
            pl.BlockSpec(wn.shape, lambda i, j: (0, 0)),
            pl.BlockSpec(wt.shape, lambda i, j: (0, 0)),
        ],
        out_specs=[
            pl.BlockSpec((tm, SSM_WIDTH), lambda i, j: (j, i)),
            pl.BlockSpec((1, tm, nk), lambda i, j: (i, j, 0)),
            pl.BlockSpec((1, wt.shape[0], tm), lambda i, j: (i, 0, j)),
        ],
        compiler_params=_cparams(("parallel", "parallel")),
        name="proj",
    )(u, wn, wt)


def _s5_kernel(lhs_ref, wbu_ref, wc_ref, are_ref, aim_ref, dmat_ref, y_ref,
               xre_ref, xim_ref, bre_ref, bim_ref):
    lc = lhs_ref.shape[0] // 8

    @pl.when(pl.program_id(0) == 0)
    def _():
        xre_ref[...] = jnp.zeros_like(xre_ref)
        xim_ref[...] = jnp.zeros_like(xim_ref)

    row_odd = (lax.broadcasted_iota(jnp.int32, (lhs_ref.shape[0], 1), 0) % 2) == 1
    for j in range(16):
        jq = j // 4
        slab = lhs_ref[:, 128 * jq:128 * (jq + 1)]
        zero = jnp.zeros_like(slab)
        lj = jnp.concatenate([jnp.where(row_odd, zero, slab), jnp.where(row_odd, slab, zero)], axis=1)
        out = _dot(lj, wbu_ref[j])
        bre_ref[:, 128 * j:128 * (j + 1)] = out[:, :128]
        bim_ref[:, 128 * j:128 * (j + 1)] = out[:, 128:]

    for cc in range(bre_ref.shape[1] // S5_CW):
        cs = slice(cc * S5_CW, (cc + 1) * S5_CW)
        ar = are_ref[:, cs]
        ai = aim_ref[:, cs]

        def step(i, carry, cs=cs, ar=ar, ai=ai):
            xr, xi = carry
            rows = pl.ds(pl.multiple_of(i * 8, 8), 8)
            nr = ar * xr - ai * xi + bre_ref[rows, cs]
            ni = ar * xi + ai * xr + bim_ref[rows, cs]
            bre_ref[rows, cs] = nr
            bim_ref[rows, cs] = ni
            return nr, ni

        xr, xi = lax.fori_loop(0, lc, step, (xre_ref[:, cs], xim_ref[:, cs]))
        xre_ref[:, cs] = xr
        xim_ref[:, cs] = xi

    odd = (lax.broadcasted_iota(jnp.int32, (1, 8, 1), 1) % 2) == 1
    for jq in range(4):
        acc = None
        for jm in range(4):
            j = 4 * jq + jm
            xj = jnp.concatenate([bre_ref[:, 128 * j:128 * (j + 1)],
                                  bim_ref[:, 128 * j:128 * (j + 1)]], axis=1).astype(BF16)
            part = _dot(xj, wc_ref[j])
            acc = part if acc is None else acc + part
        acc3 = acc.reshape(lc, 8, 256)
        y = jnp.where(odd, acc3[:, :, 128:], acc3[:, :, :128])
        u_own = lhs_ref[:, 128 * jq:128 * (jq + 1)].astype(F32).reshape(lc, 8, 128)
        y = y + dmat_ref[:, 128 * jq:128 * (jq + 1)][None] * u_own
        y_ref[:, 128 * jq:128 * (jq + 1)] = _gelu(y).reshape(lc * 8, 128).astype(y_ref.dtype)


def _s5(lhs, wbu, wc, a_re, a_im, dmat):
    rows = lhs.shape[0]
    r = 8 * S5_LC
    nl = a_re.shape[1]
    return pl.pallas_call(
        _s5_kernel,
        out_shape=jax.ShapeDtypeStruct((rows, 512), BF16),
        grid=(rows // r,),
        in_specs=[
            pl.BlockSpec((r, 512), lambda c: (c, 0)),
            pl.BlockSpec(wbu.shape, lambda c: (0, 0, 0)),
            pl.BlockSpec(wc.shape, lambda c: (0, 0, 0)),
            pl.BlockSpec(a_re.shape, lambda c: (0, 0)),
            pl.BlockSpec(a_im.shape, lambda c: (0, 0)),
            pl.BlockSpec(dmat.shape, lambda c: (0, 0)),
        ],
        out_specs=pl.BlockSpec((r, 512), lambda c: (c, 0)),
        scratch_shapes=[pltpu.VMEM((8, nl), F32), pltpu.VMEM((8, nl), F32),
                        pltpu.VMEM((r, nl), F32), pltpu.VMEM((r, nl), F32)],
        compiler_params=_cparams(("arbitrary",)),
        name="s5",
    )(lhs, wbu, wc, a_re, a_im, dmat)


def _compress_kernel(rk_ref, rv_ref, pos_ref, w1k_ref, w2k_ref, w1v_ref, w2vt_ref, k_ref, vt_ref):
    m = rk_ref.shape[2]

    def hidden(r_ref, w1_ref):
        z = _dot(r_ref[0, 0], w1_ref[...])
        pz = _dot(pos_ref[...], w1_ref[...])
        posvec = pz[0:1, :CMP_HID] + pz[1:2, CMP_HID:]
        zhi = pltpu.roll(z[:, CMP_HID:], m - 1, 0)
        return _gelu(z[:, :CMP_HID] + zhi + posvec).astype(BF16)

    hk = hidden(rk_ref, w1k_ref)
    k_ref[0, 0, :CMP_FRONT, :] = jnp.zeros((CMP_FRONT, HEAD_DIM), k_ref.dtype)
    k_ref[0, 0, CMP_FRONT:, :] = _dot(hk, w2k_ref[...]).astype(k_ref.dtype)
    hv = hidden(rv_ref, w1v_ref)
    vt_ref[0, 0, :, :CMP_FRONT] = jnp.zeros((HEAD_DIM, CMP_FRONT), vt_ref.dtype)
    vt_ref[0, 0, :, CMP_FRONT:] = _dot_nt(w2vt_ref[...], hv).astype(vt_ref.dtype)


def _compress(rk, rv, pos8, w1k, w2k, w1v, w2vt):
    b, g, m, kd = rk.shape
    ncp = CMP_FRONT + m
    full = lambda a: pl.BlockSpec(a.shape, lambda i, j: (0,) * a.ndim)
    return pl.pallas_call(
        _compress_kernel,
        out_shape=[jax.ShapeDtypeStruct((b, g, ncp, HEAD_DIM), BF16),
                   jax.ShapeDtypeStruct((b, g, HEAD_DIM, ncp), BF16)],
        grid=(b, g),
        in_specs=[
            pl.BlockSpec((1, 1, m, kd), lambda i, j: (i, j, 0, 0)),
            pl.BlockSpec((1, 1, m, kd), lambda i, j: (i, j, 0, 0)),
            full(pos8), full(w1k), full(w2k), full(w1v), full(w2vt),
        ],
        out_specs=[
            pl.BlockSpec((1, 1, ncp, HEAD_DIM), lambda i, j: (i, j, 0, 0)),
            pl.BlockSpec((1, 1, HEAD_DIM, ncp), lambda i, j: (i, j, 0, 0)),
        ],
        compiler_params=_cparams(("parallel", "parallel")),
        name="compress",
    )(rk, rv, pos8, w1k, w2k, w1v, w2vt)


def _nsa_kernel(qt_ref, gt_ref, kc_ref, vct_ref, ksl_ref, vslt_ref,
                kw0_ref, kw1_ref, kw2_ref, vw0_ref, vw1_ref, vw2_ref,
                bsel_ref, bwin_ref, bcmp_ref, cvec_ref, ovl_ref,
                o_ref,
                qpad_ref, sc_ref, nm_ref, m_ref, l_ref, acc_ref, oc_ref, os_ref, sa_ref, sb_ref):
    g = pl.program_id(1)
    nq = pl.program_id(2)
    t0 = nq * TQ
    ncp = kc_ref.shape[2]
    ns = nm_ref.shape[0]
    cpq = TQ // CMP_STRIDE
    bpt = TQ // SEL_BLOCK

    q4 = qt_ref[0]
    qpad_ref[...] = jnp.zeros_like(qpad_ref)
    grow = pl.ds(pl.multiple_of(g * HEAD_DIM, HEAD_DIM), HEAD_DIM)
    for h in range(HPG):
        qpad_ref[grow, TQ * h:TQ * (h + 1)] = q4[HEAD_DIM * h:HEAD_DIM * (h + 1)]
    heads = [slice(TQ * h, TQ * (h + 1)) for h in range(HPG)]
    t_tok = t0 + lax.broadcasted_iota(jnp.int32, (1, TQ), 1)

    row = lax.broadcasted_iota(jnp.int32, (ncp, 1), 0)
    live = (row >= CMP_FRONT) & (row < cpq * nq + CMP_FRONT + cpq)
    for h, hs in enumerate(heads):
        s = _dot(kc_ref[0, 0], q4[HEAD_DIM * h:HEAD_DIM * (h + 1)])
        sc_ref[:, hs] = s + jnp.where(live, cvec_ref[0, :, hs], NEG_INF)
    band = pl.ds(pl.multiple_of(cpq * nq + CMP_FRONT - CMP_NEAR, 8), CMP_BAND)
    sc_ref[band, :] = sc_ref[band, :] + bcmp_ref[0]
    has_cmp = (t_tok >= CMP_LEN - 1).astype(F32)
    ps = None
    for h, hs in enumerate(heads):
        s = sc_ref[:, hs]
        e = jnp.exp2(s - jnp.max(s, axis=0, keepdims=True))
        p = e * (has_cmp / jnp.sum(e, axis=0, keepdims=True))
        oc_ref[:, hs] = _dot(vct_ref[0, 0], p.astype(BF16))
        ps = p if ps is None else ps + p

    ps_hi = ps.astype(BF16)
    ps_lo = (ps - ps_hi.astype(F32)).astype(BF16)
    imp = _dot(ovl_ref[...], ps_hi) + _dot(ovl_ref[...], ps_lo)
    blk = lax.broadcasted_iota(jnp.int32, (ns, TQ), 0)
    tt = t0 + lax.broadcasted_iota(jnp.int32, (ns, TQ), 1)
    valid_b = blk * SEL_BLOCK <= tt
    cur = tt // SEL_BLOCK
    forced = valid_b & ((blk == 0) | (blk == cur) | (blk == cur - 1))
    score = jnp.where(forced, SEL_FORCE, jnp.where(valid_b, imp, -SEL_FORCE))
    blkf = blk.astype(F32)

    def pick_one(_, carry):
        sc, chosen = carry
        mx = jnp.max(sc, axis=0, keepdims=True)
        first = jnp.min(jnp.where(sc == mx, blkf, float(ns)), axis=0, keepdims=True)
        hit = blkf == first
        return jnp.where(hit, -3e38, sc), jnp.where(hit, 1.0, chosen)

    _, chosen = lax.fori_loop(0, min(N_SEL, ns), pick_one, (score, jnp.zeros((ns, TQ), F32)))
    nm_ref[...] = jnp.where(chosen > 0.0, 0.0, NEG_INF)

    def reset():
        m_ref[...] = jnp.full_like(m_ref, NEG_INF)
        l_ref[...] = jnp.zeros_like(l_ref)
        acc_ref[...] = jnp.zeros_like(acc_ref)

    ones_rows = jnp.ones((16, TQ), BF16)

    def online(cur_ref, bias_fn, vt, masks=None):
        m_old = m_ref[...]
        vt1 = jnp.concatenate([vt, ones_rows], axis=0)
        blocks = [slice(SEL_BLOCK * jj, SEL_BLOCK * (jj + 1)) for jj in range(bpt)]
        if masks is not None:
            caps = [jnp.where(mk < 0.5 * NEG_INF, 0.0, 3e38).astype(BF16) for mk in masks]
        m_new = []
        for hs in heads:
            smax = None
            for jj, rows in enumerate(blocks):
                sb = cur_ref[rows, hs] + bias_fn(rows, hs)
                cur_ref[rows, hs] = sb
                bm = jnp.max(sb.reshape(SEL_BLOCK // 8, 8, TQ), axis=0)
                if masks is not None:
                    bm = bm + masks[jj]
                smax = bm if smax is None else jnp.maximum(smax, bm)
            m_new.append(jnp.maximum(m_old[:, hs], jnp.max(smax, axis=0, keepdims=True)))
        pv = []
        for mh, hs in zip(m_new, heads):
            ps = []
            for jj, rows in enumerate(blocks):
                p = jnp.exp2(cur_ref[rows, hs] - mh).astype(BF16)
                ps.append(p if masks is None else jnp.minimum(p, caps[jj]))
            pv.append(_dot(vt1, jnp.concatenate(ps, axis=0)))
        m_new = jnp.concatenate(m_new, axis=1)
        pv = jnp.concatenate(pv, axis=1)
        alpha = jnp.exp2(m_old - m_new)
        m_ref[...] = m_new
        l_ref[...] = alpha * l_ref[...] + pv[HEAD_DIM:HEAD_DIM + 1]
        acc_ref[...] = alpha * acc_ref[...] + pv[:HEAD_DIM]

    def raw_scores(dst_ref, keys):
        for hs in heads:
            dst_ref[:, hs] = _dot(keys, qpad_ref[:, hs])

    def sel_keys(kt):
        return ksl_ref[0, pl.ds(pl.multiple_of(kt * TQ, TQ), TQ), :]

    def sel_tile(cur_ref, nxt_ref, kt):
        raw_scores(nxt_ref, sel_keys(jnp.minimum(kt + 1, nq)))
        r0 = jnp.maximum(MAX_DIST + TQ - TQ * (nq - kt), 0)
        masks = [nm_ref[pl.ds(kt * bpt + jj, 1), :] for jj in range(bpt)]
        online(cur_ref,
               lambda rows, hs: bsel_ref[0, pl.ds(pl.multiple_of(r0 + rows.start, SEL_BLOCK), SEL_BLOCK), hs],
               vslt_ref[0, :, pl.ds(pl.multiple_of(kt * TQ, TQ), TQ)], masks)

    reset()
    raw_scores(sa_ref, sel_keys(0))

    def tile_pair(i, c):
        sel_tile(sa_ref, sb_ref, 2 * i)

        @pl.when(2 * i + 1 <= nq)
        def _():
            sel_tile(sb_ref, sa_ref, 2 * i + 1)
        return c

    lax.fori_loop(0, (nq + 2) // 2, tile_pair, 0)
    os_ref[...] = acc_ref[...] / l_ref[...]

    reset()
    kws = (kw0_ref, kw1_ref, kw2_ref)
    vws = (vw0_ref, vw1_ref, vw2_ref)
    for d in range(N_WIN):
        @pl.when(nq >= d)
        def _(d=d):
            raw_scores(sa_ref, kws[d][0])
            online(sa_ref,
                   lambda rows, hs: bwin_ref[0, TQ * (N_WIN - 1 - d) + rows.start:
                                             TQ * (N_WIN - 1 - d) + rows.stop, hs],
                   vws[d][0])
    o_w = acc_ref[...] / l_ref[...]

    sg = _sigmoid(gt_ref[0].astype(F32))

    def gate(br):
        return jnp.concatenate([sg[HPG * br + h:HPG * br + h + 1] for h in range(HPG)], axis=1)

    ot = gate(0) * oc_ref[...] + gate(1) * os_ref[...] + gate(2) * o_w
    stacked = jnp.concatenate([ot[:, TQ * h:TQ * (h + 1)] for h in range(HPG)], axis=0)
    o_ref[0] = stacked.T.astype(o_ref.dtype)


def _nsa(pn, pt, kc, vct, bsel, bwin, bcmp, cvec, ovl):
    b, t, _ = pn.shape
    nq = t // TQ
    ncp = kc.shape[2]
    ns = t // SEL_BLOCK
    q_rows = N_HEADS * HEAD_DIM
    gate_blk0 = (q_rows + 2 * N_KV * HEAD_DIM) // 16
    vsl_blk0 = q_rows // HEAD_DIM
    vw_blk0 = vsl_blk0 + N_KV
    ksl_col, kw_col = 2, 3

    def kw_spec(d):
        return pl.BlockSpec((1, TQ, 256), lambda i, j, n: (i, jnp.maximum(n - d, 0), kw_col))

    def vw_spec(d):
        return pl.BlockSpec((1, HEAD_DIM, TQ), lambda i, j, n: (i, vw_blk0 + j, jnp.maximum(n - d, 0)))

    per_g = lambda a: pl.BlockSpec((1,) + a.shape[1:], lambda i, j, n: (j,) + (0,) * (a.ndim - 1))
    return pl.pallas_call(
        _nsa_kernel,
        out_shape=jax.ShapeDtypeStruct((b, t, q_rows), BF16),
        grid=(b, N_KV, nq),
        in_specs=[
            pl.BlockSpec((1, HPG * HEAD_DIM, TQ), lambda i, j, n: (i, j, n)),
            pl.BlockSpec((1, 16, TQ), lambda i, j, n: (i, gate_blk0 + j, n)),
            pl.BlockSpec((1, 1, ncp, HEAD_DIM), lambda i, j, n: (i, j, 0, 0)),
            pl.BlockSpec((1, 1, HEAD_DIM, ncp), lambda i, j, n: (i, j, 0, 0)),
            pl.BlockSpec((1, t, 256), lambda i, j, n: (i, 0, ksl_col)),
            pl.BlockSpec((1, HEAD_DIM, t), lambda i, j, n: (i, vsl_blk0 + j, 0)),
            kw_spec(0), kw_spec(1), kw_spec(2),
            vw_spec(0), vw_spec(1), vw_spec(2),
            per_g(bsel), per_g(bwin), per_g(bcmp), per_g(cvec),
            pl.BlockSpec(ovl.shape, lambda i, j, n: (0, 0)),
        ],
        out_specs=pl.BlockSpec((1, TQ, HPG * HEAD_DIM), lambda i, j, n: (i, n, j)),
        scratch_shapes=[
            pltpu.VMEM((N_KV * HEAD_DIM, LQ), BF16),
            pltpu.VMEM((ncp, LQ), F32),
            pltpu.VMEM((ns, TQ), F32),
            pltpu.VMEM((1, LQ), F32), pltpu.VMEM((1, LQ), F32), pltpu.VMEM((HEAD_DIM, LQ), F32),
            pltpu.VMEM((HEAD_DIM, LQ), F32), pltpu.VMEM((HEAD_DIM, LQ), F32),
            pltpu.VMEM((TQ, LQ), F32), pltpu.VMEM((TQ, LQ), F32),
        ],
        compiler_params=_cparams(("parallel", "parallel", "arbitrary")),
        name="nsa",
    )(pt, pt, kc, vct, pn, pt, pn, pn, pn, pt, pt, pt, bsel, bwin, bcmp, cvec, ovl)


def _merge_kernel(ys_ref, o_ref, u_ref, w1_ref, w2_ref, wo_ref, wga_ref, wgb_ref, z_ref):
    ys = ys_ref[...]
    ya = _dot(ys, w1_ref[...]) * _sigmoid(_dot(ys, w2_ref[...]))
    yb = _dot(o_ref[...], wo_ref[...])
    u = u_ref[...]
    z = _sigmoid(_dot(u, wga_ref[...])) * ya + _sigmoid(_dot(u, wgb_ref[...])) * yb
    z_ref[...] = z.astype(z_ref.dtype)


def _merge(ys, o, u, w1, w2, wo, wga, wgb):
    t = ys.shape[0]
    n = o.shape[0]
    d = w1.shape[1]
    tm, tn = min(MERGE_TM, t), MERGE_TN
    tpb = t // tm
    row = lambda a: pl.BlockSpec((tm, a.shape[1]), lambda i, j: (i, 0))
    col = lambda a: pl.BlockSpec((a.shape[0], tn), lambda i, j: (0, j))
    return pl.pallas_call(
        _merge_kernel,
        out_shape=jax.ShapeDtypeStruct((n, d), BF16),
        grid=(n // tm, d // tn),
        in_specs=[pl.BlockSpec((tm, w1.shape[0]), lambda i, j: (i % tpb, i // tpb)),
                  row(o), row(u), col(w1), col(w2), col(wo), col(wga), col(wgb)],
        out_specs=pl.BlockSpec((tm, tn), lambda i, j: (i, j)),
        compiler_params=_cparams(("parallel", "parallel")),
        name="merge",
    )(ys, o, u, w1, w2, wo, wga, wgb)


def _outproj_kernel(z_ref, h_ref, w_ref, g_ref, o_ref):
    mixed = _dot(z_ref[...], w_ref[...])
    o_ref[...] = h_ref[...] + _rms(mixed) * g_ref[...]


def _outproj(z, h, w, g):
    n, d = h.shape
    tm = min(OUT_TM, n)
    return pl.pallas_call(
        _outproj_kernel,
        out_shape=jax.ShapeDtypeStruct((n, d), F32),
        grid=(n // tm,),
        in_specs=[
            pl.BlockSpec((tm, d), lambda i: (i, 0)),
            pl.BlockSpec((tm, d), lambda i: (i, 0)),
            pl.BlockSpec(w.shape, lambda i: (0, 0)),
            pl.BlockSpec((1, d), lambda i: (0, 0)),
        ],
        out_specs=pl.BlockSpec((tm, d), lambda i: (i, 0)),
        compiler_params=_cparams(("parallel",)),
        name="outproj",
    )(z, h, w, g)


def _rel_bucket(dist):
    dist = jnp.maximum(dist, 0)
    max_exact = N_BUCKETS // 2
    d_f = jnp.maximum(dist, 1).astype(jnp.float32)
    large = max_exact + (jnp.log(d_f / max_exact) / math.log(MAX_DIST / max_exact)
                         * (N_BUCKETS - max_exact)).astype(jnp.int32)
    large = jnp.minimum(large, N_BUCKETS - 1)
    return jnp.where(dist < max_exact, dist, large)


def _bias_tables(rel_bias):
    table = (LOG2E * rel_bias).reshape(N_BUCKETS, N_KV, HPG)
    far = LOG2E * rel_bias[_rel_bucket(jnp.asarray(MAX_DIST))]
    cvec = jnp.broadcast_to(far.reshape(N_KV, 1, HPG, 1), (N_KV, 1, HPG, TQ)).reshape(N_KV, 1, LQ)
    tok = jnp.arange(TQ)[None, :]
    n_sel, n_win = MAX_DIST + 2 * TQ, WINDOW + TQ
    d_sel = tok + MAX_DIST + TQ - jnp.arange(n_sel)[:, None]
    d_win = tok + WINDOW - jnp.arange(n_win)[:, None]
    d_cmp = tok - CMP_STRIDE * (jnp.arange(CMP_BAND)[:, None] - CMP_NEAR) - (CMP_LEN - 1)
    dist = jnp.concatenate([d_sel, d_win, d_cmp], axis=0)
    ok = jnp.concatenate([d_sel >= 0, (d_win >= 0) & (d_win < WINDOW), d_cmp >= 0], axis=0)
    onehot = jax.nn.one_hot(_rel_bucket(dist), N_BUCKETS, dtype=F32)
    v = jnp.einsum('rtb,bgh->grht', onehot, table, precision=lax.Precision.HIGHEST)
    v = jnp.where(ok[None, :, None, :], v, NEG_INF).reshape(N_KV, dist.shape[0], LQ)
    bsel, bwin, bcmp = v[:, :n_sel], v[:, n_sel:n_sel + n_win], v[:, n_sel + n_win:]
    bcmp = jnp.where(bcmp > 0.5 * NEG_INF, bcmp - cvec, NEG_INF)
    return bsel, bwin, bcmp, cvec


def _overlap_t(t):
    nc = t // CMP_STRIDE
    cmp_start = np.arange(nc) * CMP_STRIDE
    sel_start = np.arange(t // SEL_BLOCK) * SEL_BLOCK
    ov = ((cmp_start[None, :] < sel_start[:, None] + SEL_BLOCK)
          & (cmp_start[None, :] + CMP_LEN > sel_start[:, None]))
    ov[:, nc - 1] = False
    out = np.zeros((t // SEL_BLOCK, CMP_FRONT + nc), np.float32)
    out[:, CMP_FRONT:] = ov
    return jnp.asarray(out, BF16)


def _s5_tables(a_re, a_im, log_dt, b_re, b_im, c_re, c_im, d_skip, batch):
    dt = jnp.exp(log_dt)[:, None]
    lam_re = jnp.minimum(a_re, -1e-4)
    lam_im = a_im
    mag = jnp.exp(lam_re * dt)
    ab_re = mag * jnp.cos(lam_im * dt)
    ab_im = mag * jnp.sin(lam_im * dt)
    den = lam_re * lam_re + lam_im * lam_im
    n_re = ab_re - 1.0
    n_im = ab_im
    co_re = (n_re * lam_re + n_im * lam_im) / den
    co_im = (n_im * lam_re - n_re * lam_im) / den
    bb_re = co_re[..., None] * b_re - co_im[..., None] * b_im
    bb_im = co_re[..., None] * b_im + co_im[..., None] * b_re
    sel = jnp.asarray(np.eye(4, dtype=np.float32)[np.arange(16) % 4])
    eye2 = jnp.eye(2, dtype=F32)
    bb = jnp.stack([bb_re, bb_im], 0).reshape(2, 2, 16, 2, SSM_STATE, SSM_GROUP)
    wbu = jnp.einsum('rhjgpc,jm,gk->jhmgcrkp', bb, sel, eye2).reshape(16, 256, 256)
    cc = jnp.stack([c_re, -c_im], 0).reshape(2, 2, 16, 2, SSM_GROUP, SSM_STATE)
    wc = jnp.einsum('rhjgcp,jm,gk->jrkphmgc', cc, sel, eye2).reshape(16, 256, 256)

    def lanes(a):
        a = a.reshape(2, 1, 16 * 2 * SSM_STATE)
        return jnp.broadcast_to(a.transpose(1, 0, 2), (batch, 2, 2048)).reshape(2 * batch, 2048)

    dmat = jnp.broadcast_to(d_skip.reshape(1, 2, 512), (batch, 2, 512)).reshape(2 * batch, 512)
    return wbu.astype(BF16), wc.astype(BF16), lanes(ab_re), lanes(ab_im), dmat


def kernel(x, ffn1_pre_g, ffn1_w_gate, ffn1_w_up, ffn1_w_down, ffn1_post_g, mix_pre_g, w_in, ssm_a_re, ssm_a_im, ssm_log_dt, ssm_b_re, ssm_b_im, ssm_c_re, ssm_c_im, ssm_d, ssm_glu_w1, ssm_glu_w2, cmp_pos, cmp_k_w1, cmp_k_w2, cmp_v_w1, cmp_v_w2, nsa_w_o, w_out, mix_post_g, ffn2_pre_g, ffn2_w_gate, ffn2_w_up, ffn2_w_down, ffn2_post_g, rel_bias):
    b, t, d = x.shape
    n = b * t
    assert b == 4 and t % TQ == 0 and t >= MAX_DIST + TQ and d == D_MODEL
    bf = lambda a: a.astype(BF16)
    h = x.reshape(n, d)
    for l in range(ffn1_pre_g.shape[0]):
        h, u = _ffn(h, ffn1_pre_g[l][None], bf(ffn1_w_gate[l]), bf(ffn1_w_up[l]), bf(ffn1_w_down[l]),
                    ffn1_post_g[l][None], mix_pre_g[l][None], True)

        w = w_in[l]
        o_q = SSM_WIDTH
        o_kv = o_q + N_HEADS * HEAD_DIM
        kvw = N_KV * HEAD_DIM
        o_gn = o_kv + 6 * kvw
        o_ga = o_gn + 3 * N_HEADS
        kv = lambda i: w[:, o_kv + i * kvw:o_kv + (i + 1) * kvw]
        wn = bf(jnp.concatenate([w[:, :o_q], kv(0), kv(1), kv(2), kv(4)], axis=1))
        gate_cols = np.full((N_KV, 4, HPG), -1)
        for g in range(N_KV):
            for br in range(3):
                for hh in range(HPG):
                    gate_cols[g, br, hh] = o_gn + 3 * HPG * g + 3 * hh + br
        gate_cols = gate_cols.reshape(-1)
        wg = jnp.where((gate_cols >= 0)[None, :], w[:, np.maximum(gate_cols, 0)], 0.0)
        w_q = w[:, o_q:o_kv] * (HEAD_DIM ** -0.5 * LOG2E)
        wt = bf(jnp.concatenate([w_q, kv(3), kv(5), wg], axis=1).T)
        us, pn, pt = _proj(u.reshape(b, t, d), wn, wt)

        wbu, wc, a_re, a_im, dmat = _s5_tables(ssm_a_re[l], ssm_a_im[l], ssm_log_dt[l], ssm_b_re[l],
                                               ssm_b_im[l], ssm_c_re[l], ssm_c_im[l], ssm_d[l], b)
        ys = _s5(us.reshape(t * 2 * b, 512), wbu, wc, a_re, a_im, dmat).reshape(t, b * SSM_WIDTH)

        def rows16(a):
            a = a.reshape(b, t // CMP_STRIDE, CMP_STRIDE, N_KV, HEAD_DIM)
            return a.transpose(0, 3, 1, 2, 4).reshape(b, N_KV, t // CMP_STRIDE, CMP_STRIDE * HEAD_DIM)

        half = CMP_STRIDE * HEAD_DIM
        w1cat = lambda w1: bf(jnp.concatenate([w1[:half], w1[half:]], axis=1))
        pos8 = jnp.zeros((8, half), F32).at[0].set(cmp_pos[l][:CMP_STRIDE].reshape(-1))
        pos8 = bf(pos8.at[1].set(cmp_pos[l][CMP_STRIDE:].reshape(-1)))
        kc, vct = _compress(rows16(pn[:, :, :kvw]), rows16(pn[:, :, kvw:2 * kvw]), pos8,
                            w1cat(cmp_k_w1[l]), bf(cmp_k_w2[l]), w1cat(cmp_v_w1[l]), bf(cmp_v_w2[l].T))
        bsel, bwin, bcmp, cvec = _bias_tables(rel_bias)
        o_nsa = _nsa(pn, pt, kc, vct, bsel, bwin, bcmp, cvec, _overlap_t(t))

        z = _merge(ys, o_nsa.reshape(n, N_HEADS * HEAD_DIM), u, bf(ssm_glu_w1[l]), bf(ssm_glu_w2[l]),
                   bf(nsa_w_o[l]), bf(w[:, o_ga:o_ga + d]), bf(w[:, o_ga + d:o_ga + 2 * d]))
        h = _outproj(z, h, bf(w_out[l]), mix_post_g[l][None])

        h, _ = _ffn(h, ffn2_pre_g[l][None], bf(ffn2_w_gate[l]), bf(ffn2_w_up[l]), bf(ffn2_w_down[l]),
                    ffn2_post_g[l][None], ffn2_post_g[l][None], False)
    return h.reshape(b, t, d)
```

```python
import functools
import math

import numpy as np
import jax
import jax.numpy as jnp
from jax import lax
from jax.experimental import pallas as pl
from jax.experimental.pallas import tpu as pltpu

F32 = jnp.float32
BF16 = jnp.bfloat16

D_MODEL = 2048
D_FF = 5632
EPS = 1e-6
SSM_WIDTH = 1024
SSM_GROUP = 16
SSM_GROUPS = 64
SSM_STATE = 64
N_HEADS = 16
N_KV = 4
HPG = 4
HEAD_DIM = 64
CMP_LEN = 32
CMP_STRIDE = 16
CMP_HID = 256
SEL_BLOCK = 64
N_SEL = 16
WINDOW = 512
SEL_FORCE = 1e4
NEG_INF = -1e30
LOG2E = 1.4426950408889634
N_BUCKETS = 32
MAX_DIST = 1024

VMEM_LIMIT = 56 * 1024 * 1024

FFN_TM = 512
FFN_TF = 512
PROJ_TM = 512
S5_LC = 64
S5_CW = 512
TQ = 256
LQ = HPG * TQ
CMP_FRONT = 128
CMP_NEAR = 72
CMP_BAND = CMP_NEAR + TQ // CMP_STRIDE
N_SEL_BAND = MAX_DIST // TQ + 1
N_WIN = WINDOW // TQ + 1
MERGE_TM = 1024
MERGE_TN = 512
OUT_TM = 512


def _cparams(sem):
    return pltpu.CompilerParams(dimension_semantics=sem, vmem_limit_bytes=VMEM_LIMIT)


def _rms(x):
    ms = jnp.mean(x * x, axis=-1, keepdims=True)
    return x * lax.rsqrt(ms + EPS)


def _sigmoid(x):
    return 1.0 / (1.0 + jnp.exp(-x))


def _gelu(x):
    return 0.5 * x * (1.0 + jnp.tanh(math.sqrt(2.0 / math.pi) * (x + 0.044715 * (x * x * x))))


def _dot(a, b):
    return jnp.dot(a, b, preferred_element_type=F32)


def _dot_nt(a, b):
    return lax.dot_general(a, b, (((1,), (1,)), ((), ())), preferred_element_type=F32)


def _ffn_kernel(x_ref, pre_ref, wg_ref, wu_ref, wd_ref, post_ref, nxt_ref, *rest, emit_next):
    if emit_next:
        o_ref, u_ref, xn_ref, acc_ref = rest
    else:
        o_ref, xn_ref, acc_ref = rest
    f = pl.program_id(1)

    @pl.when(f == 0)
    def _():
        xn_ref[...] = (_rms(x_ref[...]) * pre_ref[...]).astype(BF16)
        acc_ref[...] = jnp.zeros_like(acc_ref)

    xn = xn_ref[...]
    g = _dot(xn, wg_ref[...])
    u = _dot(xn, wu_ref[...])
    a = (g * _sigmoid(g) * u).astype(BF16)
    acc_ref[...] += _dot(a, wd_ref[...])

    @pl.when(f == pl.num_programs(1) - 1)
    def _():
        h = x_ref[...] + 0.5 * (_rms(acc_ref[...]) * post_ref[...])
        o_ref[...] = h
        if emit_next:
            u_ref[...] = (_rms(h) * nxt_ref[...]).astype(BF16)


def _ffn(x, pre_g, wg, wu, wd, post_g, nxt_g, emit_next):
    n, d = x.shape
    nf = wg.shape[1]
    tm, tf = min(FFN_TM, n), FFN_TF
    out_shape = [jax.ShapeDtypeStruct((n, d), F32)]
    out_specs = [pl.BlockSpec((tm, d), lambda i, f: (i, 0))]
    if emit_next:
        out_shape.append(jax.ShapeDtypeStruct((n, d), BF16))
        out_specs.append(pl.BlockSpec((tm, d), lambda i, f: (i, 0)))
    vec = pl.BlockSpec((1, d), lambda i, f: (0, 0))
    res = pl.pallas_call(
        functools.partial(_ffn_kernel, emit_next=emit_next),
        out_shape=out_shape,
        grid=(n // tm, nf // tf),
        in_specs=[
            pl.BlockSpec((tm, d), lambda i, f: (i, 0)),
            vec,
            pl.BlockSpec((d, tf), lambda i, f: (0, f)),
            pl.BlockSpec((d, tf), lambda i, f: (0, f)),
            pl.BlockSpec((tf, d), lambda i, f: (f, 0)),
            vec,
            vec,
        ],
        out_specs=out_specs,
        scratch_shapes=[pltpu.VMEM((tm, d), BF16), pltpu.VMEM((tm, d), F32)],
        compiler_params=_cparams(("parallel", "arbitrary")),
        name="ffn",
    )(x, pre_g, wg, wu, wd, post_g, nxt_g)
    return res if emit_next else (res[0], None)


def _proj_kernel(u_ref, wn_ref, wt_ref, us_ref, pn_ref, pt_ref):
    u = u_ref[0]
    nat = _dot(u, wn_ref[...])
    us_ref[...] = nat[:, :SSM_WIDTH].astype(us_ref.dtype)
    pn_ref[0] = nat[:, SSM_WIDTH:].astype(pn_ref.dtype)
    pt_ref[0] = _dot_nt(wt_ref[...], u).astype(pt_ref.dtype)


def _proj(u, wn, wt):
    b, t, d = u.shape
    tm = min(PROJ_TM, t)
    nk = wn.shape[1] - SSM_WIDTH
    return pl.pallas_call(
        _proj_kernel,
        out_shape=[jax.ShapeDtypeStruct((t, b * SSM_WIDTH), BF16),
                   jax.ShapeDtypeStruct((b, t, nk), BF16),
                   jax.ShapeDtypeStruct((b, wt.shape[0], t), BF16)],
        grid=(b, t // tm),
        in_specs=[
            pl.BlockSpec((1, tm, d), lambda i, j: (i, j, 0)),
            pl.BlockSpec(wn.shape, lambda i, j: (0, 0)),
            pl.BlockSpec(wt.shape, lambda i, j: (0, 0)),
        ],
        out_specs=[
            pl.BlockSpec((tm, SSM_WIDTH), lambda i, j: (j, i)),
            pl.BlockSpec((1, tm, nk), lambda i, j: (i, j, 0)),
            pl.BlockSpec((1, wt.shape[0], tm), lambda i, j: (i, 0, j)),
        ],
        compiler_params=_cparams(("parallel", "parallel")),
        name="proj",
    )(u, wn, wt)


def _s5_kernel(lhs_ref, wbu_ref, wc_ref, are_ref, aim_ref, dmat_ref, y_ref,
               xre_ref, xim_ref, bre_ref, bim_ref):
    lc = lhs_ref.shape[0] // 8

    @pl.when(pl.program_id(0) == 0)
    def _():
        xre_ref[...] = jnp.zeros_like(xre_ref)
        xim_ref[...] = jnp.zeros_like(xim_ref)

    row_odd = (lax.broadcasted_iota(jnp.int32, (lhs_ref.shape[0], 1), 0) % 2) == 1
    for j in range(16):
        jq = j // 4
        slab = lhs_ref[:, 128 * jq:128 * (jq + 1)]
        zero = jnp.zeros_like(slab)
        lj = jnp.concatenate([jnp.where(row_odd, zero, slab), jnp.where(row_odd, slab, zero)], axis=1)
        out = _dot(lj, wbu_ref[j])
        bre_ref[:, 128 * j:128 * (j + 1)] = out[:, :128]
        bim_ref[:, 128 * j:128 * (j + 1)] = out[:, 128:]

    for cc in range(bre_ref.shape[1] // S5_CW):
        cs = slice(cc * S5_CW, (cc + 1) * S5_CW)
        ar = are_ref[:, cs]
        ai = aim_ref[:, cs]

        def step(i, carry, cs=cs, ar=ar, ai=ai):
            xr, xi = carry
            rows = pl.ds(pl.multiple_of(i * 8, 8), 8)
            nr = ar * xr - ai * xi + bre_ref[rows, cs]
            ni = ar * xi + ai * xr + bim_ref[rows, cs]
            bre_ref[rows, cs] = nr
            bim_ref[rows, cs] = ni
            return nr, ni

        xr, xi = lax.fori_loop(0, lc, step, (xre_ref[:, cs], xim_ref[:, cs]))
        xre_ref[:, cs] = xr
        xim_ref[:, cs] = xi

    odd = (lax.broadcasted_iota(jnp.int32, (1, 8, 1), 1) % 2) == 1
    for jq in range(4):
        acc = None
        for jm in range(4):
            j = 4 * jq + jm
            xj = jnp.concatenate([bre_ref[:, 128 * j:128 * (j + 1)],
                                  bim_ref[:, 128 * j:128 * (j + 1)]], axis=1).astype(BF16)
            part = _dot(xj, wc_ref[j])
            acc = part if acc is None else acc + part
        acc3 = acc.reshape(lc, 8, 256)
        y = jnp.where(odd, acc3[:, :, 128:], acc3[:, :, :128])
        u_own = lhs_ref[:, 128 * jq:128 * (jq + 1)].astype(F32).reshape(lc, 8, 128)
        y = y + dmat_ref[:, 128 * jq:128 * (jq + 1)][None] * u_own
        y_ref[:, 128 * jq:128 * (jq + 1)] = _gelu(y).reshape(lc * 8, 128).astype(y_ref.dtype)


def _s5(lhs, wbu, wc, a_re, a_im, dmat):
    rows = lhs.shape[0]
    r = 8 * S5_LC
    nl = a_re.shape[1]
    return pl.pallas_call(
        _s5_kernel,
        out_shape=jax.ShapeDtypeStruct((rows, 512), BF16),
        grid=(rows // r,),
        in_specs=[
            pl.BlockSpec((r, 512), lambda c: (c, 0)),
            pl.BlockSpec(wbu.shape, lambda c: (0, 0, 0)),
            pl.BlockSpec(wc.shape, lambda c: (0, 0, 0)),
            pl.BlockSpec(a_re.shape, lambda c: (0, 0)),
            pl.BlockSpec(a_im.shape, lambda c: (0, 0)),
            pl.BlockSpec(dmat.shape, lambda c: (0, 0)),
        ],
        out_specs=pl.BlockSpec((r, 512), lambda c: (c, 0)),
        scratch_shapes=[pltpu.VMEM((8, nl), F32), pltpu.VMEM((8, nl), F32),
                        pltpu.VMEM((r, nl), F32), pltpu.VMEM((r, nl), F32)],
        compiler_params=_cparams(("arbitrary",)),
        name="s5",
    )(lhs, wbu, wc, a_re, a_im, dmat)


def _compress_kernel(rk_ref, rv_ref, pos_ref, w1k_ref, w2k_ref, w1v_ref, w2vt_ref, k_ref, vt_ref):
    m = rk_ref.shape[2]

    def hidden(r_ref, w1_ref):
        z = _dot(r_ref[0, 0], w1_ref[...])
        pz = _dot(pos_ref[...], w1_ref[...])
        posvec = pz[0:1, :CMP_HID] + pz[1:2, CMP_HID:]
        zhi = pltpu.roll(z[:, CMP_HID:], m - 1, 0)
        return _gelu(z[:, :CMP_HID] + zhi + posvec).astype(BF16)

    hk = hidden(rk_ref, w1k_ref)
    k_ref[0, 0, :CMP_FRONT, :] = jnp.zeros((CMP_FRONT, HEAD_DIM), k_ref.dtype)
    k_ref[0, 0, CMP_FRONT:, :] = _dot(hk, w2k_ref[...]).astype(k_ref.dtype)
    hv = hidden(rv_ref, w1v_ref)
    vt_ref[0, 0, :, :CMP_FRONT] = jnp.zeros((HEAD_DIM, CMP_FRONT), vt_ref.dtype)
    vt_ref[0, 0, :, CMP_FRONT:] = _dot_nt(w2vt_ref[...], hv).astype(vt_ref.dtype)


def _compress(rk, rv, pos8, w1k, w2k, w1v, w2vt):
    b, g, m, kd = rk.shape
    ncp = CMP_FRONT + m
    full = lambda a: pl.BlockSpec(a.shape, lambda i, j: (0,) * a.ndim)
    return pl.pallas_call(
        _compress_kernel,
        out_shape=[jax.ShapeDtypeStruct((b, g, ncp, HEAD_DIM), BF16),
                   jax.ShapeDtypeStruct((b, g, HEAD_DIM, ncp), BF16)],
        grid=(b, g),
        in_specs=[
            pl.BlockSpec((1, 1, m, kd), lambda i, j: (i, j, 0, 0)),
            pl.BlockSpec((1, 1, m, kd), lambda i, j: (i, j, 0, 0)),
            full(pos8), full(w1k), full(w2k), full(w1v), full(w2vt),
        ],
        out_specs=[
            pl.BlockSpec((1, 1, ncp, HEAD_DIM), lambda i, j: (i, j, 0, 0)),
            pl.BlockSpec((1, 1, HEAD_DIM, ncp), lambda i, j: (i, j, 0, 0)),
        ],
        compiler_params=_cparams(("parallel", "parallel")),
        name="compress",
    )(rk, rv, pos8, w1k, w2k, w1v, w2vt)


def _nsa_kernel(qt_ref, gt_ref, kc_ref, vct_ref, ksl_ref, vslt_ref,
                kw0_ref, kw1_ref, kw2_ref, vw0_ref, vw1_ref, vw2_ref,
                bsel_ref, bwin_ref, bcmp_ref, cvec_ref, ovl_ref,
                o_ref,
                qpad_ref, sc_ref, nm_ref, m_ref, l_ref, acc_ref, oc_ref, os_ref,
                sa_ref, sb_ref, ma_ref, mb_ref, imp_ref):
    g = pl.program_id(1)
    nq = pl.program_id(2)
    t0 = nq * TQ
    ncp = kc_ref.shape[2]
    ns = nm_ref.shape[0]
    cpq = TQ // CMP_STRIDE
    bpt = TQ // SEL_BLOCK

    q4 = qt_ref[0]
    qpad_ref[...] = jnp.zeros_like(qpad_ref)
    grow = pl.ds(pl.multiple_of(g * HEAD_DIM, HEAD_DIM), HEAD_DIM)
    for h in range(HPG):
        qpad_ref[grow, TQ * h:TQ * (h + 1)] = q4[HEAD_DIM * h:HEAD_DIM * (h + 1)]
    heads = [slice(TQ * h, TQ * (h + 1)) for h in range(HPG)]
    t_tok = t0 + lax.broadcasted_iota(jnp.int32, (1, TQ), 1)

    ch = CMP_FRONT
    hi_row = cpq * nq + CMP_FRONT + cpq
    n_live = (hi_row + ch - 1) // ch

    def chunk(c):
        return pl.ds(pl.multiple_of(c * ch, ch), ch)

    def cmp_scores(c, carry):
        row = c * ch + lax.broadcasted_iota(jnp.int32, (ch, 1), 0)
        for h, hs in enumerate(heads):
            s = _dot(kc_ref[0, 0, chunk(c), :], q4[HEAD_DIM * h:HEAD_DIM * (h + 1)])
            sc_ref[chunk(c), hs] = s + jnp.where(row < hi_row, cvec_ref[0, :, hs], NEG_INF)
        return carry

    sc_ref[0:ch, :] = jnp.full((ch, LQ), NEG_INF, F32)
    lax.fori_loop(1, n_live, cmp_scores, 0)
    band = pl.ds(pl.multiple_of(cpq * nq + CMP_FRONT - CMP_NEAR, 8), CMP_BAND)
    sc_ref[band, :] = sc_ref[band, :] + bcmp_ref[0]

    def cmp_max(c, m8):
        return jnp.maximum(m8, jnp.max(sc_ref[chunk(c), :].reshape(ch // 8, 8, LQ), axis=0))

    m_cmp = jnp.max(lax.fori_loop(1, n_live, cmp_max, jnp.full((8, LQ), -3e38, F32)), axis=0, keepdims=True)

    def cmp_exp(c, l8):
        e = jnp.exp2(sc_ref[chunk(c), :] - m_cmp)
        sc_ref[chunk(c), :] = e
        return l8 + jnp.sum(e.reshape(ch // 8, 8, LQ), axis=0)

    l_cmp = jnp.sum(lax.fori_loop(1, n_live, cmp_exp, jnp.zeros((8, LQ), F32)), axis=0, keepdims=True)
    has_cmp = (t_tok >= CMP_LEN - 1).astype(F32)
    p_scale = jnp.concatenate([has_cmp] * HPG, axis=1) / l_cmp
    oc_ref[...] = jnp.zeros_like(oc_ref)
    imp_ref[...] = jnp.zeros_like(imp_ref)

    def cmp_out(c, carry):
        p = sc_ref[chunk(c), :] * p_scale
        oc_ref[...] += _dot(vct_ref[0, 0, :, chunk(c)], p.astype(BF16))
        ps = p[:, heads[0]] + p[:, heads[1]] + p[:, heads[2]] + p[:, heads[3]]
        ps_hi = ps.astype(BF16)
        ps_lo = (ps - ps_hi.astype(F32)).astype(BF16)
        ovl = ovl_ref[:, chunk(c)]
        imp_ref[...] += _dot(ovl, ps_hi) + _dot(ovl, ps_lo)
        return carry

    lax.fori_loop(1, n_live, cmp_out, 0)

    imp = imp_ref[...]
    blk = lax.broadcasted_iota(jnp.int32, (ns, TQ), 0)
    tt = t0 + lax.broadcasted_iota(jnp.int32, (ns, TQ), 1)
    valid_b = blk * SEL_BLOCK <= tt
    cur = tt // SEL_BLOCK
    forced = valid_b & ((blk == 0) | (blk == cur) | (blk == cur - 1))
    taken = -3e38
    score = jnp.where(forced, taken, jnp.where(valid_b, imp, -SEL_FORCE))
    blkf = blk.astype(F32)

    def pick_one(_, carry):
        sc, chosen = carry
        mx = jnp.max(sc, axis=0, keepdims=True)
        first = jnp.min(jnp.where(sc == mx, blkf, float(ns)), axis=0, keepdims=True)
        hit = blkf == first
        return jnp.where(hit, taken, sc), jnp.where(hit, 1.0, chosen)

    n_forced = 3
    _, chosen = lax.fori_loop(0, min(N_SEL, ns) - n_forced, pick_one, (score, forced.astype(F32)))
    nm_ref[...] = jnp.where(chosen > 0.0, 0.0, NEG_INF)

    def reset():
        m_ref[...] = jnp.full_like(m_ref, NEG_INF)
        l_ref[...] = jnp.zeros_like(l_ref)
        acc_ref[...] = jnp.zeros_like(acc_ref)

    ones_rows = jnp.ones((16, TQ), BF16)

    blocks = [slice(SEL_BLOCK * jj, SEL_BLOCK * (jj + 1)) for jj in range(bpt)]

    def scores(dst, keys, bias_fn):
        s_dst, bm_dst = dst
        for hs in heads:
            s = _dot(keys, qpad_ref[:, hs])
            for jj, rows in enumerate(blocks):
                sb = s[rows] + bias_fn(rows, hs)
                s_dst[rows, hs] = sb
                bm_dst[8 * jj:8 * (jj + 1), hs] = jnp.max(sb.reshape(SEL_BLOCK // 8, 8, TQ), axis=0)

    def online(src, vt, masks=None):
        s_src, bm_src = src
        m_old = m_ref[...]
        vt1 = jnp.concatenate([vt, ones_rows], axis=0)
        if masks is not None:
            caps = [jnp.where(mk < 0.5 * NEG_INF, 0.0, 3e38).astype(BF16) for mk in masks]
        m_new, pv = [], []
        for hs in heads:
            smax = None
            for jj in range(bpt):
                bm = bm_src[8 * jj:8 * (jj + 1), hs]
                if masks is not None:
                    bm = bm + masks[jj]
                smax = bm if smax is None else jnp.maximum(smax, bm)
            mh = jnp.maximum(m_old[:, hs], jnp.max(smax, axis=0, keepdims=True))
            ps = []
            for jj, rows in enumerate(blocks):
                p = jnp.exp2(s_src[rows, hs] - mh).astype(BF16)
                ps.append(p if masks is None else jnp.minimum(p, caps[jj]))
            m_new.append(mh)
            pv.append(_dot(vt1, jnp.concatenate(ps, axis=0)))
        m_new = jnp.concatenate(m_new, axis=1)
        pv = jnp.concatenate(pv, axis=1)
        alpha = jnp.exp2(m_old - m_new)
        m_ref[...] = m_new
        l_ref[...] = alpha * l_ref[...] + pv[HEAD_DIM:HEAD_DIM + 1]
        acc_ref[...] = alpha * acc_ref[...] + pv[:HEAD_DIM]

    def sel_scores(dst, kt):
        r0 = jnp.maximum(MAX_DIST + TQ - TQ * (nq - kt), 0)
        scores(dst, ksl_ref[0, pl.ds(pl.multiple_of(kt * TQ, TQ), TQ), :],
               lambda rows, hs: bsel_ref[0, pl.ds(pl.multiple_of(r0 + rows.start, SEL_BLOCK), SEL_BLOCK), hs])

    def sel_tile(cur, nxt, kt):
        sel_scores(nxt, jnp.minimum(kt + 1, nq))
        masks = [nm_ref[pl.ds(kt * bpt + jj, 1), :] for jj in range(bpt)]
        online(cur, vslt_ref[0, :, pl.ds(pl.multiple_of(kt * TQ, TQ), TQ)], masks)

    buf_a = (sa_ref, ma_ref)
    buf_b = (sb_ref, mb_ref)
    reset()
    sel_scores(buf_a, 0)

    def tile_pair(i, c):
        sel_tile(buf_a, buf_b, 2 * i)

        @pl.when(2 * i + 1 <= nq)
        def _():
            sel_tile(buf_b, buf_a, 2 * i + 1)
        return c

    lax.fori_loop(0, (nq + 2) // 2, tile_pair, 0)
    os_ref[...] = acc_ref[...] / l_ref[...]

    reset()
    kws = (kw0_ref, kw1_ref, kw2_ref)
    vws = (vw0_ref, vw1_ref, vw2_ref)
    def win_scores(dst, d):
        base = TQ * (N_WIN - 1 - d)
        scores(dst, kws[d][0], lambda rows, hs: bwin_ref[0, base + rows.start:base + rows.stop, hs])

    def win_online(src, d):
        gone = jnp.where(nq >= d, 0.0, NEG_INF) + jnp.zeros((1, TQ), F32)
        online(src, vws[d][0], [gone] * bpt)

    bufs = (buf_a, buf_b)
    win_scores(bufs[0], 0)
    for d in range(N_WIN):
        if d + 1 < N_WIN:
            win_scores(bufs[(d + 1) % 2], d + 1)
        win_online(bufs[d % 2], d)
    o_w = acc_ref[...] / l_ref[...]

    sg = _sigmoid(gt_ref[0].astype(F32))

    def gate(br):
        return jnp.concatenate([sg[HPG * br + h:HPG * br + h + 1] for h in range(HPG)], axis=1)

    ot = gate(0) * oc_ref[...] + gate(1) * os_ref[...] + gate(2) * o_w
    stacked = jnp.concatenate([ot[:, TQ * h:TQ * (h + 1)] for h in range(HPG)], axis=0)
    o_ref[0] = stacked.T.astype(o_ref.dtype)


def _nsa(pn, pt, kc, vct, bsel, bwin, bcmp, cvec, ovl):
    b, t, _ = pn.shape
    nq = t // TQ
    ncp = kc.shape[2]
    ns = t // SEL_BLOCK
    q_rows = N_HEADS * HEAD_DIM
    gate_blk0 = (q_rows + 2 * N_KV * HEAD_DIM) // 16
    vsl_blk0 = q_rows // HEAD_DIM
    vw_blk0 = vsl_blk0 + N_KV
    ksl_col, kw_col = 2, 3

    def kw_spec(d):
        return pl.BlockSpec((1, TQ, 256), lambda i, j, n: (i, jnp.maximum(n - d, 0), kw_col))

    def vw_spec(d):
        return pl.BlockSpec((1, HEAD_DIM, TQ), lambda i, j, n: (i, vw_blk0 + j, jnp.maximum(n - d, 0)))

    per_g = lambda a: pl.BlockSpec((1,) + a.shape[1:], lambda i, j, n: (j,) + (0,) * (a.ndim - 1))
    return pl.pallas_call(
        _nsa_kernel,
        out_shape=jax.ShapeDtypeStruct((b, t, q_rows), BF16),
        grid=(b, N_KV, nq),
        in_specs=[
            pl.BlockSpec((1, HPG * HEAD_DIM, TQ), lambda i, j, n: (i, j, n)),
            pl.BlockSpec((1, 16, TQ), lambda i, j, n: (i, gate_blk0 + j, n)),
            pl.BlockSpec((1, 1, ncp, HEAD_DIM), lambda i, j, n: (i, j, 0, 0)),
            pl.BlockSpec((1, 1, HEAD_DIM, ncp), lambda i, j, n: (i, j, 0, 0)),
            pl.BlockSpec((1, t, 256), lambda i, j, n: (i, 0, ksl_col)),
            pl.BlockSpec((1, HEAD_DIM, t), lambda i, j, n: (i, vsl_blk0 + j, 0)),
            kw_spec(0), kw_spec(1), kw_spec(2),
            vw_spec(0), vw_spec(1), vw_spec(2),
            per_g(bsel), per_g(bwin), per_g(bcmp), per_g(cvec),
            pl.BlockSpec(ovl.shape, lambda i, j, n: (0, 0)),
        ],
        out_specs=pl.BlockSpec((1, TQ, HPG * HEAD_DIM), lambda i, j, n: (i, n, j)),
        scratch_shapes=[
            pltpu.VMEM((N_KV * HEAD_DIM, LQ), BF16),
            pltpu.VMEM((ncp, LQ), F32),
            pltpu.VMEM((ns, TQ), F32),
            pltpu.VMEM((1, LQ), F32), pltpu.VMEM((1, LQ), F32), pltpu.VMEM((HEAD_DIM, LQ), F32),
            pltpu.VMEM((HEAD_DIM, LQ), F32), pltpu.VMEM((HEAD_DIM, LQ), F32),
            pltpu.VMEM((TQ, LQ), F32), pltpu.VMEM((TQ, LQ), F32),
            pltpu.VMEM((8 * TQ // SEL_BLOCK, LQ), F32), pltpu.VMEM((8 * TQ // SEL_BLOCK, LQ), F32),
            pltpu.VMEM((ns, TQ), F32),
        ],
        compiler_params=_cparams(("parallel", "parallel", "arbitrary")),
        name="nsa",
    )(pt, pt, kc, vct, pn, pt, pn, pn, pn, pt, pt, pt, bsel, bwin, bcmp, cvec, ovl)


def _merge_kernel(ys_ref, o_ref, u_ref, w1_ref, w2_ref, wo_ref, wga_ref, wgb_ref, z_ref):
    ys = ys_ref[...]
    ya = _dot(ys, w1_ref[...]) * _sigmoid(_dot(ys, w2_ref[...]))
    yb = _dot(o_ref[...], wo_ref[...])
    u = u_ref[...]
    z = _sigmoid(_dot(u, wga_ref[...])) * ya + _sigmoid(_dot(u, wgb_ref[...])) * yb
    z_ref[...] = z.astype(z_ref.dtype)


def _merge(ys, o, u, w1, w2, wo, wga, wgb):
    t = ys.shape[0]
    n = o.shape[0]
    d = w1.shape[1]
    tm, tn = min(MERGE_TM, t), MERGE_TN
    tpb = t // tm
    row = lambda a: pl.BlockSpec((tm, a.shape[1]), lambda i, j: (i, 0))
    col = lambda a: pl.BlockSpec((a.shape[0], tn), lambda i, j: (0, j))
    return pl.pallas_call(
        _merge_kernel,
        out_shape=jax.ShapeDtypeStruct((n, d), BF16),
        grid=(n // tm, d // tn),
        in_specs=[pl.BlockSpec((tm, w1.shape[0]), lambda i, j: (i % tpb, i // tpb)),
                  row(o), row(u), col(w1), col(w2), col(wo), col(wga), col(wgb)],
        out_specs=pl.BlockSpec((tm, tn), lambda i, j: (i, j)),
        compiler_params=_cparams(("parallel", "parallel")),
        name="merge",
    )(ys, o, u, w1, w2, wo, wga, wgb)


def _outproj_kernel(z_ref, h_ref, w_ref, g_ref, o_ref):
    mixed = _dot(z_ref[...], w_ref[...])
    o_ref[...] = h_ref[...] + _rms(mixed) * g_ref[...]


def _outproj(z, h, w, g):
    n, d = h.shape
    tm = min(OUT_TM, n)
    return pl.pallas_call(
        _outproj_kernel,
        out_shape=jax.ShapeDtypeStruct((n, d), F32),
        grid=(n // tm,),
        in_specs=[
            pl.BlockSpec((tm, d), lambda i: (i, 0)),
            pl.BlockSpec((tm, d), lambda i: (i, 0)),
            pl.BlockSpec(w.shape, lambda i: (0, 0)),
            pl.BlockSpec((1, d), lambda i: (0, 0)),
        ],
        out_specs=pl.BlockSpec((tm, d), lambda i: (i, 0)),
        compiler_params=_cparams(("parallel",)),
        name="outproj",
    )(z, h, w, g)


def _rel_bucket(dist):
    dist = jnp.maximum(dist, 0)
    max_exact = N_BUCKETS // 2
    d_f = jnp.maximum(dist, 1).astype(jnp.float32)
    large = max_exact + (jnp.log(d_f / max_exact) / math.log(MAX_DIST / max_exact)
                         * (N_BUCKETS - max_exact)).astype(jnp.int32)
    large = jnp.minimum(large, N_BUCKETS - 1)
    return jnp.where(dist < max_exact, dist, large)


def _bias_tables(rel_bias):
    table = (LOG2E * rel_bias).reshape(N_BUCKETS, N_KV, HPG)
    far = LOG2E * rel_bias[_rel_bucket(jnp.asarray(MAX_DIST))]
    cvec = jnp.broadcast_to(far.reshape(N_KV, 1, HPG, 1), (N_KV, 1, HPG, TQ)).reshape(N_KV, 1, LQ)
    tok = jnp.arange(TQ)[None, :]
    n_sel, n_win = MAX_DIST + 2 * TQ, WINDOW + TQ
    d_sel = tok + MAX_DIST + TQ - jnp.arange(n_sel)[:, None]
    d_win = tok + WINDOW - jnp.arange(n_win)[:, None]
    d_cmp = tok - CMP_STRIDE * (jnp.arange(CMP_BAND)[:, None] - CMP_NEAR) - (CMP_LEN - 1)
    dist = jnp.concatenate([d_sel, d_win, d_cmp], axis=0)
    ok = jnp.concatenate([d_sel >= 0, (d_win >= 0) & (d_win < WINDOW), d_cmp >= 0], axis=0)
    onehot = jax.nn.one_hot(_rel_bucket(dist), N_BUCKETS, dtype=F32)
    v = jnp.einsum('rtb,bgh->grht', onehot, table, precision=lax.Precision.HIGHEST)
    v = jnp.where(ok[None, :, None, :], v, NEG_INF).reshape(N_KV, dist.shape[0], LQ)
    bsel, bwin, bcmp = v[:, :n_sel], v[:, n_sel:n_sel + n_win], v[:, n_sel + n_win:]
    bcmp = jnp.where(bcmp > 0.5 * NEG_INF, bcmp - cvec, NEG_INF)
    return bsel, bwin, bcmp, cvec


def _overlap_t(t):
    nc = t // CMP_STRIDE
    cmp_start = np.arange(nc) * CMP_STRIDE
    sel_start = np.arange(t // SEL_BLOCK) * SEL_BLOCK
    ov = ((cmp_start[None, :] < sel_start[:, None] + SEL_BLOCK)
          & (cmp_start[None, :] + CMP_LEN > sel_start[:, None]))
    ov[:, nc - 1] = False
    out = np.zeros((t // SEL_BLOCK, CMP_FRONT + nc), np.float32)
    out[:, CMP_FRONT:] = ov
    return jnp.asarray(out, BF16)


def _s5_tables(a_re, a_im, log_dt, b_re, b_im, c_re, c_im, d_skip, batch):
    dt = jnp.exp(log_dt)[:, None]
    lam_re = jnp.minimum(a_re, -1e-4)
    lam_im = a_im
    mag = jnp.exp(lam_re * dt)
    ab_re = mag * jnp.cos(lam_im * dt)
    ab_im = mag * jnp.sin(lam_im * dt)
    den = lam_re * lam_re + lam_im * lam_im
    n_re = ab_re - 1.0
    n_im = ab_im
    co_re = (n_re * lam_re + n_im * lam_im) / den
    co_im = (n_im * lam_re - n_re * lam_im) / den
    bb_re = co_re[..., None] * b_re - co_im[..., None] * b_im
    bb_im = co_re[..., None] * b_im + co_im[..., None] * b_re
    sel = jnp.asarray(np.eye(4, dtype=np.float32)[np.arange(16) % 4])
    eye2 = jnp.eye(2, dtype=F32)
    bb = jnp.stack([bb_re, bb_im], 0).reshape(2, 2, 16, 2, SSM_STATE, SSM_GROUP)
    wbu = jnp.einsum('rhjgpc,jm,gk->jhmgcrkp', bb, sel, eye2).reshape(16, 256, 256)
    cc = jnp.stack([c_re, -c_im], 0).reshape(2, 2, 16, 2, SSM_GROUP, SSM_STATE)
    wc = jnp.einsum('rhjgcp,jm,gk->jrkphmgc', cc, sel, eye2).reshape(16, 256, 256)

    def lanes(a):
        a = a.reshape(2, 1, 16 * 2 * SSM_STATE)
        return jnp.broadcast_to(a.transpose(1, 0, 2), (batch, 2, 2048)).reshape(2 * batch, 2048)

    dmat = jnp.broadcast_to(d_skip.reshape(1, 2, 512), (batch, 2, 512)).reshape(2 * batch, 512)
    return wbu.astype(BF16), wc.astype(BF16), lanes(ab_re), lanes(ab_im), dmat


def kernel(x, ffn1_pre_g, ffn1_w_gate, ffn1_w_up, ffn1_w_down, ffn1_post_g, mix_pre_g, w_in, ssm_a_re, ssm_a_im, ssm_log_dt, ssm_b_re, ssm_b_im, ssm_c_re, ssm_c_im, ssm_d, ssm_glu_w1, ssm_glu_w2, cmp_pos, cmp_k_w1, cmp_k_w2, cmp_v_w1, cmp_v_w2, nsa_w_o, w_out, mix_post_g, ffn2_pre_g, ffn2_w_gate, ffn2_w_up, ffn2_w_down, ffn2_post_g, rel_bias):
    b, t, d = x.shape
    n = b * t
    assert b == 4 and t % TQ == 0 and t >= MAX_DIST + TQ and d == D_MODEL
    bf = lambda a: a.astype(BF16)
    h = x.reshape(n, d)
    for l in range(ffn1_pre_g.shape[0]):
        h, u = _ffn(h, ffn1_pre_g[l][None], bf(ffn1_w_gate[l]), bf(ffn1_w_up[l]), bf(ffn1_w_down[l]),
                    ffn1_post_g[l][None], mix_pre_g[l][None], True)

        w = w_in[l]
        o_q = SSM_WIDTH
        o_kv = o_q + N_HEADS * HEAD_DIM
        kvw = N_KV * HEAD_DIM
        o_gn = o_kv + 6 * kvw
        o_ga = o_gn + 3 * N_HEADS
        kv = lambda i: w[:, o_kv + i * kvw:o_kv + (i + 1) * kvw]
        wn = bf(jnp.concatenate([w[:, :o_q], kv(0), kv(1), kv(2), kv(4)], axis=1))
        gate_cols = np.full((N_KV, 4, HPG), -1)
        for g in range(N_KV):
            for br in range(3):
                for hh in range(HPG):
                    gate_cols[g, br, hh] = o_gn + 3 * HPG * g + 3 * hh + br
        gate_cols = gate_cols.reshape(-1)
        wg = jnp.where((gate_cols >= 0)[None, :], w[:, np.maximum(gate_cols, 0)], 0.0)
        w_q = w[:, o_q:o_kv] * (HEAD_DIM ** -0.5 * LOG2E)
        wt = bf(jnp.concatenate([w_q, kv(3), kv(5), wg], axis=1).T)
        us, pn, pt = _proj(u.reshape(b, t, d), wn, wt)

        wbu, wc, a_re, a_im, dmat = _s5_tables(ssm_a_re[l], ssm_a_im[l], ssm_log_dt[l], ssm_b_re[l],
                                               ssm_b_im[l], ssm_c_re[l], ssm_c_im[l], ssm_d[l], b)
        ys = _s5(us.reshape(t * 2 * b, 512), wbu, wc, a_re, a_im, dmat).reshape(t, b * SSM_WIDTH)

        def rows16(a):
            a = a.reshape(b, t // CMP_STRIDE, CMP_STRIDE, N_KV, HEAD_DIM)
            return a.transpose(0, 3, 1, 2, 4).reshape(b, N_KV, t // CMP_STRIDE, CMP_STRIDE * HEAD_DIM)

        half = CMP_STRIDE * HEAD_DIM
        w1cat = lambda w1: bf(jnp.concatenate([w1[:half], w1[half:]], axis=1))
        pos8 = jnp.zeros((8, half), F32).at[0].set(cmp_pos[l][:CMP_STRIDE].reshape(-1))
        pos8 = bf(pos8.at[1].set(cmp_pos[l][CMP_STRIDE:].reshape(-1)))
        kc, vct = _compress(rows16(pn[:, :, :kvw]), rows16(pn[:, :, kvw:2 * kvw]), pos8,
                            w1cat(cmp_k_w1[l]), bf(cmp_k_w2[l]), w1cat(cmp_v_w1[l]), bf(cmp_v_w2[l].T))
        bsel, bwin, bcmp, cvec = _bias_tables(rel_bias)
        o_nsa = _nsa(pn, pt, kc, vct, bsel, bwin, bcmp, cvec, _overlap_t(t))

        z = _merge(ys, o_nsa.reshape(n, N_HEADS * HEAD_DIM), u, bf(ssm_glu_w1[l]), bf(ssm_glu_w2[l]),
                   bf(nsa_w_o[l]), bf(w[:, o_ga:o_ga + d]), bf(w[:, o_ga + d:o_ga + 2 * d]))
        h = _outproj(z, h, bf(w_out[l]), mix_post_g[l][None])

        h, _ = _ffn(h, ffn2_pre_g[l][None], bf(ffn2_w_gate[l]), bf(ffn2_w_up[l]), bf(ffn2_w_down[l]),
                    ffn2_post_g[l][None], ffn2_post_g[l][None], False)
    return h.reshape(b, t, d)
```

```python
import functools
import math

import numpy as np
import jax
import jax.numpy as jnp
from jax import lax
from jax.experimental import pallas as pl
from jax.experimental.pallas import tpu as pltpu

F32 = jnp.float32
BF16 = jnp.bfloat16

D_MODEL = 2048
D_FF = 5632
EPS = 1e-6
SSM_WIDTH = 1024
SSM_GROUP = 16
SSM_GROUPS = 64
SSM_STATE = 64
N_HEADS = 16
N_KV = 4
HPG = 4
HEAD_DIM = 64
CMP_LEN = 32
CMP_STRIDE = 16
CMP_HID = 256
SEL_BLOCK = 64
N_SEL = 16
WINDOW = 512
SEL_FORCE = 1e4
NEG_INF = -1e30
LOG2E = 1.4426950408889634
N_BUCKETS = 32
MAX_DIST = 1024

VMEM_LIMIT = 56 * 1024 * 1024

FFN_TM = 512
FFN_TF = 512
PROJ_TM = 512
S5_LC = 64
S5_CW = 512
TQ = 256
TK = 512
LQ = HPG * TQ
CMP_FRONT = 128
CMP_NEAR = 72
CMP_BAND = CMP_NEAR + TQ // CMP_STRIDE
N_WIN = WINDOW // TQ + 1
MERGE_TM = 1024
MERGE_TN = 512
OUT_TM = 512


def _cparams(sem):
    return pltpu.CompilerParams(dimension_semantics=sem, vmem_limit_bytes=VMEM_LIMIT)


def _rms(x):
    ms = jnp.mean(x * x, axis=-1, keepdims=True)
    return x * lax.rsqrt(ms + EPS)


def _sigmoid(x):
    return 1.0 / (1.0 + jnp.exp(-x))


def _gelu(x):
    return 0.5 * x * (1.0 + jnp.tanh(math.sqrt(2.0 / math.pi) * (x + 0.044715 * (x * x * x))))


def _dot(a, b):
    return jnp.dot(a, b, preferred_element_type=F32)


def _dot_nt(a, b):
    return lax.dot_general(a, b, (((1,), (1,)), ((), ())), preferred_element_type=F32)


def _ffn_kernel(x_ref, pre_ref, wg_ref, wu_ref, wd_ref, post_ref, nxt_ref, *rest, emit_next):
    if emit_next:
        o_ref, u_ref, xn_ref, acc_ref = rest
    else:
        o_ref, xn_ref, acc_ref = rest
    f = pl.program_id(1)

    @pl.when(f == 0)
    def _():
        xn_ref[...] = (_rms(x_ref[...]) * pre_ref[...]).astype(BF16)
        acc_ref[...] = jnp.zeros_like(acc_ref)

    xn = xn_ref[...]
    g = _dot(xn, wg_ref[...])
    u = _dot(xn, wu_ref[...])
    a = (g * _sigmoid(g) * u).astype(BF16)
    acc_ref[...] += _dot(a, wd_ref[...])

    @pl.when(f == pl.num_programs(1) - 1)
    def _():
        h = x_ref[...] + 0.5 * (_rms(acc_ref[...]) * post_ref[...])
        o_ref[...] = h
        if emit_next:
            u_ref[...] = (_rms(h) * nxt_ref[...]).astype(BF16)


def _ffn(x, pre_g, wg, wu, wd, post_g, nxt_g, emit_next):
    n, d = x.shape
    nf = wg.shape[1]
    tm, tf = min(FFN_TM, n), FFN_TF
    out_shape = [jax.ShapeDtypeStruct((n, d), F32)]
    out_specs = [pl.BlockSpec((tm, d), lambda i, f: (i, 0))]
    if emit_next:
        out_shape.append(jax.ShapeDtypeStruct((n, d), BF16))
        out_specs.append(pl.BlockSpec((tm, d), lambda i, f: (i, 0)))
    vec = pl.BlockSpec((1, d), lambda i, f: (0, 0))
    res = pl.pallas_call(
        functools.partial(_ffn_kernel, emit_next=emit_next),
        out_shape=out_shape,
        grid=(n // tm, nf // tf),
        in_specs=[
            pl.BlockSpec((tm, d), lambda i, f: (i, 0)),
            vec,
            pl.BlockSpec((d, tf), lambda i, f: (0, f)),
            pl.BlockSpec((d, tf), lambda i, f: (0, f)),
            pl.BlockSpec((tf, d), lambda i, f: (f, 0)),
            vec,
            vec,
        ],
        out_specs=out_specs,
        scratch_shapes=[pltpu.VMEM((tm, d), BF16), pltpu.VMEM((tm, d), F32)],
        compiler_params=_cparams(("parallel", "arbitrary")),
        name="ffn",
    )(x, pre_g, wg, wu, wd, post_g, nxt_g)
    return res if emit_next else (res[0], None)


def _proj_kernel(u_ref, wn_ref, wt_ref, us_ref, pn_ref, pt_ref):
    u = u_ref[0]
    nat = _dot(u, wn_ref[...])
    us_ref[...] = nat[:, :SSM_WIDTH].astype(us_ref.dtype)
    pn_ref[0] = nat[:, SSM_WIDTH:].astype(pn_ref.dtype)
    pt_ref[0] = _dot_nt(wt_ref[...], u).astype(pt_ref.dtype)


def _proj(u, wn, wt):
    b, t, d = u.shape
    tm = min(PROJ_TM, t)
    nk = wn.shape[1] - SSM_WIDTH
    return pl.pallas_call(
        _proj_kernel,
        out_shape=[jax.ShapeDtypeStruct((t, b * SSM_WIDTH), BF16),
                   jax.ShapeDtypeStruct((b, t, nk), BF16),
                   jax.ShapeDtypeStruct((b, wt.shape[0], t), BF16)],
        grid=(b, t // tm),
        in_specs=[
            pl.BlockSpec((1, tm, d), lambda i, j: (i, j, 0)),
            pl.BlockSpec(wn.shape, lambda i, j: (0, 0)),
            pl.BlockSpec(wt.shape, lambda i, j: (0, 0)),
        ],
        out_specs=[
            pl.BlockSpec((tm, SSM_WIDTH), lambda i, j: (j, i)),
            pl.BlockSpec((1, tm, nk), lambda i, j: (i, j, 0)),
            pl.BlockSpec((1, wt.shape[0], tm), lambda i, j: (i, 0, j)),
        ],
        compiler_params=_cparams(("parallel", "parallel")),
        name="proj",
    )(u, wn, wt)


def _s5_kernel(lhs_ref, wbu_ref, wc_ref, are_ref, aim_ref, dmat_ref, y_ref,
               xre_ref, xim_ref, bre_ref, bim_ref):
    lc = lhs_ref.shape[0] // 8

    @pl.when(pl.program_id(0) == 0)
    def _():
        xre_ref[...] = jnp.zeros_like(xre_ref)
        xim_ref[...] = jnp.zeros_like(xim_ref)

    row_odd = (lax.broadcasted_iota(jnp.int32, (lhs_ref.shape[0], 1), 0) % 2) == 1
    for j in range(16):
        jq = j // 4
        slab = lhs_ref[:, 128 * jq:128 * (jq + 1)]
        zero = jnp.zeros_like(slab)
        lj = jnp.concatenate([jnp.where(row_odd, zero, slab), jnp.where(row_odd, slab, zero)], axis=1)
        out = _dot(lj, wbu_ref[j])
        bre_ref[:, 128 * j:128 * (j + 1)] = out[:, :128]
        bim_ref[:, 128 * j:128 * (j + 1)] = out[:, 128:]

    for cc in range(bre_ref.shape[1] // S5_CW):
        cs = slice(cc * S5_CW, (cc + 1) * S5_CW)
        ar = are_ref[:, cs]
        ai = aim_ref[:, cs]

        def step(i, carry, cs=cs, ar=ar, ai=ai):
            xr, xi = carry
            rows = pl.ds(pl.multiple_of(i * 8, 8), 8)
            nr = ar * xr - ai * xi + bre_ref[rows, cs]
            ni = ar * xi + ai * xr + bim_ref[rows, cs]
            bre_ref[rows, cs] = nr
            bim_ref[rows, cs] = ni
            return nr, ni

        xr, xi = lax.fori_loop(0, lc, step, (xre_ref[:, cs], xim_ref[:, cs]))
        xre_ref[:, cs] = xr
        xim_ref[:, cs] = xi

    odd = (lax.broadcasted_iota(jnp.int32, (1, 8, 1), 1) % 2) == 1
    for jq in range(4):
        acc = None
        for jm in range(4):
            j = 4 * jq + jm
            xj = jnp.concatenate([bre_ref[:, 128 * j:128 * (j + 1)],
                                  bim_ref[:, 128 * j:128 * (j + 1)]], axis=1).astype(BF16)
            part = _dot(xj, wc_ref[j])
            acc = part if acc is None else acc + part
        acc3 = acc.reshape(lc, 8, 256)
        y = jnp.where(odd, acc3[:, :, 128:], acc3[:, :, :128])
        u_own = lhs_ref[:, 128 * jq:128 * (jq + 1)].astype(F32).reshape(lc, 8, 128)
        y = y + dmat_ref[:, 128 * jq:128 * (jq + 1)][None] * u_own
        y_ref[:, 128 * jq:128 * (jq + 1)] = _gelu(y).reshape(lc * 8, 128).astype(y_ref.dtype)


def _s5(lhs, wbu, wc, a_re, a_im, dmat):
    rows = lhs.shape[0]
    r = 8 * S5_LC
    nl = a_re.shape[1]
    return pl.pallas_call(
        _s5_kernel,
        out_shape=jax.ShapeDtypeStruct((rows, 512), BF16),
        grid=(rows // r,),
        in_specs=[
            pl.BlockSpec((r, 512), lambda c: (c, 0)),
            pl.BlockSpec(wbu.shape, lambda c: (0, 0, 0)),
            pl.BlockSpec(wc.shape, lambda c: (0, 0, 0)),
            pl.BlockSpec(a_re.shape, lambda c: (0, 0)),
            pl.BlockSpec(a_im.shape, lambda c: (0, 0)),
            pl.BlockSpec(dmat.shape, lambda c: (0, 0)),
        ],
        out_specs=pl.BlockSpec((r, 512), lambda c: (c, 0)),
        scratch_shapes=[pltpu.VMEM((8, nl), F32), pltpu.VMEM((8, nl), F32),
                        pltpu.VMEM((r, nl), F32), pltpu.VMEM((r, nl), F32)],
        compiler_params=_cparams(("arbitrary",)),
        name="s5",
    )(lhs, wbu, wc, a_re, a_im, dmat)


def _compress_kernel(rk_ref, rv_ref, pos_ref, w1k_ref, w2k_ref, w1v_ref, w2vt_ref, k_ref, vt_ref):
    m = rk_ref.shape[2]

    def hidden(r_ref, w1_ref):
        z = _dot(r_ref[0, 0], w1_ref[...])
        pz = _dot(pos_ref[...], w1_ref[...])
        posvec = pz[0:1, :CMP_HID] + pz[1:2, CMP_HID:]
        zhi = pltpu.roll(z[:, CMP_HID:], m - 1, 0)
        return _gelu(z[:, :CMP_HID] + zhi + posvec).astype(BF16)

    hk = hidden(rk_ref, w1k_ref)
    k_ref[0, 0, :CMP_FRONT, :] = jnp.zeros((CMP_FRONT, HEAD_DIM), k_ref.dtype)
    k_ref[0, 0, CMP_FRONT:, :] = _dot(hk, w2k_ref[...]).astype(k_ref.dtype)
    hv = hidden(rv_ref, w1v_ref)
    vt_ref[0, 0, :, :CMP_FRONT] = jnp.zeros((HEAD_DIM, CMP_FRONT), vt_ref.dtype)
    vt_ref[0, 0, :, CMP_FRONT:] = _dot_nt(w2vt_ref[...], hv).astype(vt_ref.dtype)


def _compress(rk, rv, pos8, w1k, w2k, w1v, w2vt):
    b, g, m, kd = rk.shape
    ncp = CMP_FRONT + m
    full = lambda a: pl.BlockSpec(a.shape, lambda i, j: (0,) * a.ndim)
    return pl.pallas_call(
        _compress_kernel,
        out_shape=[jax.ShapeDtypeStruct((b, g, ncp, HEAD_DIM), BF16),
                   jax.ShapeDtypeStruct((b, g, HEAD_DIM, ncp), BF16)],
        grid=(b, g),
        in_specs=[
            pl.BlockSpec((1, 1, m, kd), lambda i, j: (i, j, 0, 0)),
            pl.BlockSpec((1, 1, m, kd), lambda i, j: (i, j, 0, 0)),
            full(pos8), full(w1k), full(w2k), full(w1v), full(w2vt),
        ],
        out_specs=[
            pl.BlockSpec((1, 1, ncp, HEAD_DIM), lambda i, j: (i, j, 0, 0)),
            pl.BlockSpec((1, 1, HEAD_DIM, ncp), lambda i, j: (i, j, 0, 0)),
        ],
        compiler_params=_cparams(("parallel", "parallel")),
        name="compress",
    )(rk, rv, pos8, w1k, w2k, w1v, w2vt)


def _nsa_kernel(qt_ref, gt_ref, kc_ref, vct_ref, ksl_ref, vslt_ref,
                kw0_ref, kw1_ref, kw2_ref, vw0_ref, vw1_ref, vw2_ref,
                bsel_ref, bwin_ref, bcmp_ref, cvec_ref, ovl_ref,
                o_ref,
                qpad_ref, sc_ref, nm_ref, m_ref, l_ref, acc_ref, oc_ref, ow_ref,
                sa_ref, sb_ref, ma_ref, mb_ref, imp_ref):
    g = pl.program_id(1)
    nq = pl.program_id(2)
    t0 = nq * TQ
    ncp = kc_ref.shape[2]
    ns = nm_ref.shape[0]
    cpq = TQ // CMP_STRIDE
    bpt = TQ // SEL_BLOCK

    q4 = qt_ref[0]
    qpad_ref[...] = jnp.zeros_like(qpad_ref)
    grow = pl.ds(pl.multiple_of(g * HEAD_DIM, HEAD_DIM), HEAD_DIM)
    for h in range(HPG):
        qpad_ref[grow, TQ * h:TQ * (h + 1)] = q4[HEAD_DIM * h:HEAD_DIM * (h + 1)]
    heads = [slice(TQ * h, TQ * (h + 1)) for h in range(HPG)]
    t_tok = t0 + lax.broadcasted_iota(jnp.int32, (1, TQ), 1)

    ch = CMP_FRONT
    hi_row = cpq * nq + CMP_FRONT + cpq
    n_live = (hi_row + ch - 1) // ch

    def chunk(c):
        return pl.ds(pl.multiple_of(c * ch, ch), ch)

    def cmp_scores(c, carry):
        row = c * ch + lax.broadcasted_iota(jnp.int32, (ch, 1), 0)
        for h, hs in enumerate(heads):
            s = _dot(kc_ref[0, 0, chunk(c), :], q4[HEAD_DIM * h:HEAD_DIM * (h + 1)])
            sc_ref[chunk(c), hs] = s + jnp.where(row < hi_row, cvec_ref[0, :, hs], NEG_INF)
        return carry

    sc_ref[0:ch, :] = jnp.full((ch, LQ), NEG_INF, F32)
    lax.fori_loop(1, n_live, cmp_scores, 0)
    band = pl.ds(pl.multiple_of(cpq * nq + CMP_FRONT - CMP_NEAR, 8), CMP_BAND)
    sc_ref[band, :] = sc_ref[band, :] + bcmp_ref[0]

    def cmp_max(c, m8):
        return jnp.maximum(m8, jnp.max(sc_ref[chunk(c), :].reshape(ch // 8, 8, LQ), axis=0))

    m_cmp = jnp.max(lax.fori_loop(1, n_live, cmp_max, jnp.full((8, LQ), -3e38, F32)), axis=0, keepdims=True)

    def cmp_exp(c, l8):
        e = jnp.exp2(sc_ref[chunk(c), :] - m_cmp)
        sc_ref[chunk(c), :] = e
        return l8 + jnp.sum(e.reshape(ch // 8, 8, LQ), axis=0)

    l_cmp = jnp.sum(lax.fori_loop(1, n_live, cmp_exp, jnp.zeros((8, LQ), F32)), axis=0, keepdims=True)
    has_cmp = (t_tok >= CMP_LEN - 1).astype(F32)
    p_scale = jnp.concatenate([has_cmp] * HPG, axis=1) / l_cmp
    oc_ref[...] = jnp.zeros_like(oc_ref)
    imp_ref[...] = jnp.zeros_like(imp_ref)

    def cmp_out(c, carry):
        p = sc_ref[chunk(c), :] * p_scale
        oc_ref[...] += _dot(vct_ref[0, 0, :, chunk(c)], p.astype(BF16))
        ps = p[:, heads[0]] + p[:, heads[1]] + p[:, heads[2]] + p[:, heads[3]]
        ps_hi = ps.astype(BF16)
        ps_lo = (ps - ps_hi.astype(F32)).astype(BF16)
        ovl = ovl_ref[:, chunk(c)]
        imp_ref[...] += _dot(ovl, ps_hi) + _dot(ovl, ps_lo)
        return carry

    lax.fori_loop(1, n_live, cmp_out, 0)

    def select_blocks():
        imp = imp_ref[...]
        blk = lax.broadcasted_iota(jnp.int32, (ns, TQ), 0)
        tt = t0 + lax.broadcasted_iota(jnp.int32, (ns, TQ), 1)
        valid_b = blk * SEL_BLOCK <= tt
        cur = tt // SEL_BLOCK
        forced = valid_b & ((blk == 0) | (blk == cur) | (blk == cur - 1))
        taken = -3e38
        sc = jnp.where(forced, taken, jnp.where(valid_b, imp, -SEL_FORCE))
        chosen = forced.astype(F32)
        blkf = blk.astype(F32)
        n_forced = 3
        for _ in range(min(N_SEL, ns) - n_forced):
            mx = jnp.max(sc, axis=0, keepdims=True)
            first = jnp.min(jnp.where(sc == mx, blkf, float(ns)), axis=0, keepdims=True)
            hit = blkf == first
            sc = jnp.where(hit, taken, sc)
            chosen = jnp.where(hit, 1.0, chosen)
        nm_ref[...] = jnp.where(chosen > 0.0, 0.0, NEG_INF)

    def reset():
        m_ref[...] = jnp.full_like(m_ref, NEG_INF)
        l_ref[...] = jnp.zeros_like(l_ref)
        acc_ref[...] = jnp.zeros_like(acc_ref)

    def scores(dst, keys, bias_fn):
        s_dst, bm_dst = dst
        for hs in heads:
            s = _dot(keys, qpad_ref[:, hs])
            for jj in range(keys.shape[0] // SEL_BLOCK):
                rows = slice(SEL_BLOCK * jj, SEL_BLOCK * (jj + 1))
                sb = s[rows] + bias_fn(rows, hs)
                s_dst[rows, hs] = sb
                bm_dst[8 * jj:8 * (jj + 1), hs] = jnp.max(sb.reshape(SEL_BLOCK // 8, 8, TQ), axis=0)

    def online(src, vt, masks):
        s_src, bm_src = src
        m_old = m_ref[...]
        vt1 = jnp.concatenate([vt, jnp.ones((16, vt.shape[1]), BF16)], axis=0)
        caps = [jnp.where(mk < 0.5 * NEG_INF, 0.0, 3e38).astype(BF16) for mk in masks]
        m_new, pv = [], []
        for hs in heads:
            smax = None
            for jj, mk in enumerate(masks):
                bm = bm_src[8 * jj:8 * (jj + 1), hs] + mk
                smax = bm if smax is None else jnp.maximum(smax, bm)
            mh = jnp.maximum(m_old[:, hs], jnp.max(smax, axis=0, keepdims=True))
            ps = [jnp.minimum(jnp.exp2(s_src[SEL_BLOCK * jj:SEL_BLOCK * (jj + 1), hs] - mh).astype(BF16), cap)
                  for jj, cap in enumerate(caps)]
            m_new.append(mh)
            pv.append(_dot(vt1, jnp.concatenate(ps, axis=0)))
        m_new = jnp.concatenate(m_new, axis=1)
        pv = jnp.concatenate(pv, axis=1)
        alpha = jnp.exp2(m_old - m_new)
        m_ref[...] = m_new
        l_ref[...] = alpha * l_ref[...] + pv[HEAD_DIM:HEAD_DIM + 1]
        acc_ref[...] = alpha * acc_ref[...] + pv[:HEAD_DIM]

    last = (nq * TQ) // TK

    def sel_scores(dst, kt):
        r0 = jnp.maximum(MAX_DIST + TK + TK * kt - t0, 0)
        scores(dst, ksl_ref[0, pl.ds(pl.multiple_of(kt * TK, TK), TK), :],
               lambda rows, hs: bsel_ref[0, pl.ds(pl.multiple_of(r0 + rows.start, SEL_BLOCK), SEL_BLOCK), hs])

    def sel_tile(cur, nxt, kt):
        sel_scores(nxt, jnp.minimum(kt + 1, last))
        masks = [nm_ref[pl.ds(kt * (TK // SEL_BLOCK) + jj, 1), :] for jj in range(TK // SEL_BLOCK)]
        online(cur, vslt_ref[0, :, pl.ds(pl.multiple_of(kt * TK, TK), TK)], masks)

    buf_a = (sa_ref, ma_ref)
    buf_b = (sb_ref, mb_ref)

    reset()
    kws = (kw0_ref, kw1_ref, kw2_ref)
    vws = (vw0_ref, vw1_ref, vw2_ref)

    def win_scores(dst, d):
        base = TQ * (N_WIN - 1 - d)
        scores(dst, kws[d][0], lambda rows, hs: bwin_ref[0, base + rows.start:base + rows.stop, hs])

    def win_online(src, d):
        gone = jnp.where(nq >= d, 0.0, NEG_INF) + jnp.zeros((1, TQ), F32)
        online(src, vws[d][0], [gone] * bpt)

    bufs = (buf_a, buf_b)
    win_scores(bufs[0], 0)
    for d in range(N_WIN):
        if d + 1 < N_WIN:
            win_scores(bufs[(d + 1) % 2], d + 1)
        win_online(bufs[d % 2], d)
    ow_ref[...] = acc_ref[...] / l_ref[...]
    reset()
    sel_scores(buf_a, 0)
    select_blocks()


    def tile_pair(i, c):
        sel_tile(buf_a, buf_b, 2 * i)

        @pl.when(2 * i + 1 <= last)
        def _():
            sel_tile(buf_b, buf_a, 2 * i + 1)
        return c

    lax.fori_loop(0, (last + 2) // 2, tile_pair, 0)
    o_s = acc_ref[...] / l_ref[...]

    sg = _sigmoid(gt_ref[0].astype(F32))

    def gate(br):
        return jnp.concatenate([sg[HPG * br + h:HPG * br + h + 1] for h in range(HPG)], axis=1)

    ot = gate(0) * oc_ref[...] + gate(1) * o_s + gate(2) * ow_ref[...]
    stacked = jnp.concatenate([ot[:, TQ * h:TQ * (h + 1)] for h in range(HPG)], axis=0)
    o_ref[0] = stacked.T.astype(o_ref.dtype)


def _nsa(pn, pt, kc, vct, bsel, bwin, bcmp, cvec, ovl):
    b, t, _ = pn.shape
    nq = t // TQ
    ncp = kc.shape[2]
    ns = t // SEL_BLOCK
    q_rows = N_HEADS * HEAD_DIM
    gate_blk0 = (q_rows + 2 * N_KV * HEAD_DIM) // 16
    vsl_blk0 = q_rows // HEAD_DIM
    vw_blk0 = vsl_blk0 + N_KV
    ksl_col, kw_col = 2, 3

    def kw_spec(d):
        return pl.BlockSpec((1, TQ, 256), lambda i, j, n: (i, jnp.maximum(n - d, 0), kw_col))

    def vw_spec(d):
        return pl.BlockSpec((1, HEAD_DIM, TQ), lambda i, j, n: (i, vw_blk0 + j, jnp.maximum(n - d, 0)))

    per_g = lambda a: pl.BlockSpec((1,) + a.shape[1:], lambda i, j, n: (j,) + (0,) * (a.ndim - 1))
    return pl.pallas_call(
        _nsa_kernel,
        out_shape=jax.ShapeDtypeStruct((b, t, q_rows), BF16),
        grid=(b, N_KV, nq),
        in_specs=[
            pl.BlockSpec((1, HPG * HEAD_DIM, TQ), lambda i, j, n: (i, j, n)),
            pl.BlockSpec((1, 16, TQ), lambda i, j, n: (i, gate_blk0 + j, n)),
            pl.BlockSpec((1, 1, ncp, HEAD_DIM), lambda i, j, n: (i, j, 0, 0)),
            pl.BlockSpec((1, 1, HEAD_DIM, ncp), lambda i, j, n: (i, j, 0, 0)),
            pl.BlockSpec((1, t, 256), lambda i, j, n: (i, 0, ksl_col)),
            pl.BlockSpec((1, HEAD_DIM, t), lambda i, j, n: (i, vsl_blk0 + j, 0)),
            kw_spec(0), kw_spec(1), kw_spec(2),
            vw_spec(0), vw_spec(1), vw_spec(2),
            per_g(bsel), per_g(bwin), per_g(bcmp), per_g(cvec),
            pl.BlockSpec(ovl.shape, lambda i, j, n: (0, 0)),
        ],
        out_specs=pl.BlockSpec((1, TQ, HPG * HEAD_DIM), lambda i, j, n: (i, n, j)),
        scratch_shapes=[
            pltpu.VMEM((N_KV * HEAD_DIM, LQ), BF16),
            pltpu.VMEM((ncp, LQ), F32),
            pltpu.VMEM((ns, TQ), F32),
            pltpu.VMEM((1, LQ), F32), pltpu.VMEM((1, LQ), F32), pltpu.VMEM((HEAD_DIM, LQ), F32),
            pltpu.VMEM((HEAD_DIM, LQ), F32), pltpu.VMEM((HEAD_DIM, LQ), F32),
            pltpu.VMEM((TK, LQ), F32), pltpu.VMEM((TK, LQ), F32),
            pltpu.VMEM((8 * TK // SEL_BLOCK, LQ), F32), pltpu.VMEM((8 * TK // SEL_BLOCK, LQ), F32),
            pltpu.VMEM((ns, TQ), F32),
        ],
        compiler_params=_cparams(("parallel", "parallel", "arbitrary")),
        name="nsa",
    )(pt, pt, kc, vct, pn, pt, pn, pn, pn, pt, pt, pt, bsel, bwin, bcmp, cvec, ovl)


def _merge_kernel(ys_ref, o_ref, u_ref, w1_ref, w2_ref, wo_ref, wga_ref, wgb_ref, z_ref):
    ys = ys_ref[...]
    ya = _dot(ys, w1_ref[...]) * _sigmoid(_dot(ys, w2_ref[...]))
    yb = _dot(o_ref[...], wo_ref[...])
    u = u_ref[...]
    z = _sigmoid(_dot(u, wga_ref[...])) * ya + _sigmoid(_dot(u, wgb_ref[...])) * yb
    z_ref[...] = z.astype(z_ref.dtype)


def _merge(ys, o, u, w1, w2, wo, wga, wgb):
    t = ys.shape[0]
    n = o.shape[0]
    d = w1.shape[1]
    tm, tn = min(MERGE_TM, t), MERGE_TN
    tpb = t // tm
    row = lambda a: pl.BlockSpec((tm, a.shape[1]), lambda i, j: (i, 0))
    col = lambda a: pl.BlockSpec((a.shape[0], tn), lambda i, j: (0, j))
    return pl.pallas_call(
        _merge_kernel,
        out_shape=jax.ShapeDtypeStruct((n, d), BF16),
        grid=(n // tm, d // tn),
        in_specs=[pl.BlockSpec((tm, w1.shape[0]), lambda i, j: (i % tpb, i // tpb)),
                  row(o), row(u), col(w1), col(w2), col(wo), col(wga), col(wgb)],
        out_specs=pl.BlockSpec((tm, tn), lambda i, j: (i, j)),
        compiler_params=_cparams(("parallel", "parallel")),
        name="merge",
    )(ys, o, u, w1, w2, wo, wga, wgb)


def _outproj_kernel(z_ref, h_ref, w_ref, g_ref, o_ref):
    mixed = _dot(z_ref[...], w_ref[...])
    o_ref[...] = h_ref[...] + _rms(mixed) * g_ref[...]


def _outproj(z, h, w, g):
    n, d = h.shape
    tm = min(OUT_TM, n)
    return pl.pallas_call(
        _outproj_kernel,
        out_shape=jax.ShapeDtypeStruct((n, d), F32),
        grid=(n // tm,),
        in_specs=[
            pl.BlockSpec((tm, d), lambda i: (i, 0)),
            pl.BlockSpec((tm, d), lambda i: (i, 0)),
            pl.BlockSpec(w.shape, lambda i: (0, 0)),
            pl.BlockSpec((1, d), lambda i: (0, 0)),
        ],
        out_specs=pl.BlockSpec((tm, d), lambda i: (i, 0)),
        compiler_params=_cparams(("parallel",)),
        name="outproj",
    )(z, h, w, g)


def _rel_bucket(dist):
    dist = jnp.maximum(dist, 0)
    max_exact = N_BUCKETS // 2
    d_f = jnp.maximum(dist, 1).astype(jnp.float32)
    large = max_exact + (jnp.log(d_f / max_exact) / math.log(MAX_DIST / max_exact)
                         * (N_BUCKETS - max_exact)).astype(jnp.int32)
    large = jnp.minimum(large, N_BUCKETS - 1)
    return jnp.where(dist < max_exact, dist, large)


def _bias_tables(rel_bias):
    table = (LOG2E * rel_bias).reshape(N_BUCKETS, N_KV, HPG)
    far = LOG2E * rel_bias[_rel_bucket(jnp.asarray(MAX_DIST))]
    cvec = jnp.broadcast_to(far.reshape(N_KV, 1, HPG, 1), (N_KV, 1, HPG, TQ)).reshape(N_KV, 1, LQ)
    tok = jnp.arange(TQ)[None, :]
    n_sel, n_win = MAX_DIST + 2 * TK, WINDOW + TQ
    d_sel = tok + MAX_DIST + TK - jnp.arange(n_sel)[:, None]
    d_win = tok + WINDOW - jnp.arange(n_win)[:, None]
    d_cmp = tok - CMP_STRIDE * (jnp.arange(CMP_BAND)[:, None] - CMP_NEAR) - (CMP_LEN - 1)
    dist = jnp.concatenate([d_sel, d_win, d_cmp], axis=0)
    ok = jnp.concatenate([d_sel >= 0, (d_win >= 0) & (d_win < WINDOW), d_cmp >= 0], axis=0)
    onehot = jax.nn.one_hot(_rel_bucket(dist), N_BUCKETS, dtype=F32)
    v = jnp.einsum('rtb,bgh->grht', onehot, table, precision=lax.Precision.HIGHEST)
    v = jnp.where(ok[None, :, None, :], v, NEG_INF).reshape(N_KV, dist.shape[0], LQ)
    bsel, bwin, bcmp = v[:, :n_sel], v[:, n_sel:n_sel + n_win], v[:, n_sel + n_win:]
    bcmp = jnp.where(bcmp > 0.5 * NEG_INF, bcmp - cvec, NEG_INF)
    return bsel, bwin, bcmp, cvec


def _overlap_t(t):
    nc = t // CMP_STRIDE
    cmp_start = np.arange(nc) * CMP_STRIDE
    sel_start = np.arange(t // SEL_BLOCK) * SEL_BLOCK
    ov = ((cmp_start[None, :] < sel_start[:, None] + SEL_BLOCK)
          & (cmp_start[None, :] + CMP_LEN > sel_start[:, None]))
    ov[:, nc - 1] = False
    out = np.zeros((t // SEL_BLOCK, CMP_FRONT + nc), np.float32)
    out[:, CMP_FRONT:] = ov
    return jnp.asarray(out, BF16)


def _s5_tables(a_re, a_im, log_dt, b_re, b_im, c_re, c_im, d_skip, batch):
    dt = jnp.exp(log_dt)[:, None]
    lam_re = jnp.minimum(a_re, -1e-4)
    lam_im = a_im
    mag = jnp.exp(lam_re * dt)
    ab_re = mag * jnp.cos(lam_im * dt)
    ab_im = mag * jnp.sin(lam_im * dt)
    den = lam_re * lam_re + lam_im * lam_im
    n_re = ab_re - 1.0
    n_im = ab_im
    co_re = (n_re * lam_re + n_im * lam_im) / den
    co_im = (n_im * lam_re - n_re * lam_im) / den
    bb_re = co_re[..., None] * b_re - co_im[..., None] * b_im
    bb_im = co_re[..., None] * b_im + co_im[..., None] * b_re
    sel = jnp.asarray(np.eye(4, dtype=np.float32)[np.arange(16) % 4])
    eye2 = jnp.eye(2, dtype=F32)
    bb = jnp.stack([bb_re, bb_im], 0).reshape(2, 2, 16, 2, SSM_STATE, SSM_GROUP)
    wbu = jnp.einsum('rhjgpc,jm,gk->jhmgcrkp', bb, sel, eye2).reshape(16, 256, 256)
    cc = jnp.stack([c_re, -c_im], 0).reshape(2, 2, 16, 2, SSM_GROUP, SSM_STATE)
    wc = jnp.einsum('rhjgcp,jm,gk->jrkphmgc', cc, sel, eye2).reshape(16, 256, 256)

    def lanes(a):
        a = a.reshape(2, 1, 16 * 2 * SSM_STATE)
        return jnp.broadcast_to(a.transpose(1, 0, 2), (batch, 2, 2048)).reshape(2 * batch, 2048)

    dmat = jnp.broadcast_to(d_skip.reshape(1, 2, 512), (batch, 2, 512)).reshape(2 * batch, 512)
    return wbu.astype(BF16), wc.astype(BF16), lanes(ab_re), lanes(ab_im), dmat


def kernel(x, ffn1_pre_g, ffn1_w_gate, ffn1_w_up, ffn1_w_down, ffn1_post_g, mix_pre_g, w_in, ssm_a_re, ssm_a_im, ssm_log_dt, ssm_b_re, ssm_b_im, ssm_c_re, ssm_c_im, ssm_d, ssm_glu_w1, ssm_glu_w2, cmp_pos, cmp_k_w1, cmp_k_w2, cmp_v_w1, cmp_v_w2, nsa_w_o, w_out, mix_post_g, ffn2_pre_g, ffn2_w_gate, ffn2_w_up, ffn2_w_down, ffn2_post_g, rel_bias):
    b, t, d = x.shape
    n = b * t
    assert b == 4 and t % TK == 0 and TK % TQ == 0 and t >= MAX_DIST + TK and d == D_MODEL
    bf = lambda a: a.astype(BF16)
    h = x.reshape(n, d)
    for l in range(ffn1_pre_g.shape[0]):
        h, u = _ffn(h, ffn1_pre_g[l][None], bf(ffn1_w_gate[l]), bf(ffn1_w_up[l]), bf(ffn1_w_down[l]),
                    ffn1_post_g[l][None], mix_pre_g[l][None], True)

        w = w_in[l]
        o_q = SSM_WIDTH
        o_kv = o_q + N_HEADS * HEAD_DIM
        kvw = N_KV * HEAD_DIM
        o_gn = o_kv + 6 * kvw
        o_ga = o_gn + 3 * N_HEADS
        kv = lambda i: w[:, o_kv + i * kvw:o_kv + (i + 1) * kvw]
        wn = bf(jnp.concatenate([w[:, :o_q], kv(0), kv(1), kv(2), kv(4)], axis=1))
        gate_cols = np.full((N_KV, 4, HPG), -1)
        for g in range(N_KV):
            for br in range(3):
                for hh in range(HPG):
                    gate_cols[g, br, hh] = o_gn + 3 * HPG * g + 3 * hh + br
        gate_cols = gate_cols.reshape(-1)
        wg = jnp.where((gate_cols >= 0)[None, :], w[:, np.maximum(gate_cols, 0)], 0.0)
        w_q = w[:, o_q:o_kv] * (HEAD_DIM ** -0.5 * LOG2E)
        wt = bf(jnp.concatenate([w_q, kv(3), kv(5), wg], axis=1).T)
        us, pn, pt = _proj(u.reshape(b, t, d), wn, wt)

        wbu, wc, a_re, a_im, dmat = _s5_tables(ssm_a_re[l], ssm_a_im[l], ssm_log_dt[l], ssm_b_re[l],
                                               ssm_b_im[l], ssm_c_re[l], ssm_c_im[l], ssm_d[l], b)
        ys = _s5(us.reshape(t * 2 * b, 512), wbu, wc, a_re, a_im, dmat).reshape(t, b * SSM_WIDTH)

        def rows16(a):
            a = a.reshape(b, t // CMP_STRIDE, CMP_STRIDE, N_KV, HEAD_DIM)
            return a.transpose(0, 3, 1, 2, 4).reshape(b, N_KV, t // CMP_STRIDE, CMP_STRIDE * HEAD_DIM)

        half = CMP_STRIDE * HEAD_DIM
        w1cat = lambda w1: bf(jnp.concatenate([w1[:half], w1[half:]], axis=1))
        pos8 = jnp.zeros((8, half), F32).at[0].set(cmp_pos[l][:CMP_STRIDE].reshape(-1))
        pos8 = bf(pos8.at[1].set(cmp_pos[l][CMP_STRIDE:].reshape(-1)))
        kc, vct = _compress(rows16(pn[:, :, :kvw]), rows16(pn[:, :, kvw:2 * kvw]), pos8,
                            w1cat(cmp_k_w1[l]), bf(cmp_k_w2[l]), w1cat(cmp_v_w1[l]), bf(cmp_v_w2[l].T))
        bsel, bwin, bcmp, cvec = _bias_tables(rel_bias)
        o_nsa = _nsa(pn, pt, kc, vct, bsel, bwin, bcmp, cvec, _overlap_t(t))

        z = _merge(ys, o_nsa.reshape(n, N_HEADS * HEAD_DIM), u, bf(ssm_glu_w1[l]), bf(ssm_glu_w2[l]),
                   bf(nsa_w_o[l]), bf(w[:, o_ga:o_ga + d]), bf(w[:, o_ga + d:o_ga + 2 * d]))
        h = _outproj(z, h, bf(w_out[l]), mix_post_g[l][None])

        h, _ = _ffn(h, ffn2_pre_g[l][None], bf(ffn2_w_gate[l]), bf(ffn2_w_up[l]), bf(ffn2_w_down[l]),
                    ffn2_post_g[l][None], ffn2_post_g[l][None], False)
    return h.reshape(b, t, d)
```

```python
import functools
import math

import numpy as np
import jax
import jax.numpy as jnp
from jax import lax
from jax.experimental import pallas as pl
from jax.experimental.pallas import tpu as pltpu

F32 = jnp.float32
BF16 = jnp.bfloat16

D_MODEL = 2048
D_FF = 5632
EPS = 1e-6
SSM_WIDTH = 1024
SSM_GROUP = 16
SSM_GROUPS = 64
SSM_STATE = 64
N_HEADS = 16
N_KV = 4
HPG = 4
HEAD_DIM = 64
CMP_LEN = 32
CMP_STRIDE = 16
CMP_HID = 256
SEL_BLOCK = 64
N_SEL = 16
WINDOW = 512
SEL_FORCE = 1e4
NEG_INF = -1e30
LOG2E = 1.4426950408889634
N_BUCKETS = 32
MAX_DIST = 1024

VMEM_LIMIT = 56 * 1024 * 1024

FFN_TM = 512
FFN_TF = 512
PROJ_TM = 512
S5_LC = 64
S5_CW = 512
TQ = 256
TK = 512
LQ = HPG * TQ
CMP_FRONT = 128
CMP_NEAR = 72
CMP_BAND = CMP_NEAR + TQ // CMP_STRIDE
N_WIN = WINDOW // TQ + 1
MERGE_TM = 1024
MERGE_TN = 512
OUT_TM = 512


def _cparams(sem):
    return pltpu.CompilerParams(dimension_semantics=sem, vmem_limit_bytes=VMEM_LIMIT)


def _rms(x):
    ms = jnp.mean(x * x, axis=-1, keepdims=True)
    return x * lax.rsqrt(ms + EPS)


def _sigmoid(x):
    return 1.0 / (1.0 + jnp.exp(-x))


def _gelu(x):
    return 0.5 * x * (1.0 + jnp.tanh(math.sqrt(2.0 / math.pi) * (x + 0.044715 * (x * x * x))))


def _dot(a, b):
    return jnp.dot(a, b, preferred_element_type=F32)


def _dot_nt(a, b):
    return lax.dot_general(a, b, (((1,), (1,)), ((), ())), preferred_element_type=F32)


def _ffn_kernel(x_ref, pre_ref, wg_ref, wu_ref, wd_ref, post_ref, nxt_ref, *rest, emit_next):
    if emit_next:
        o_ref, u_ref, xn_ref, acc_ref = rest
    else:
        o_ref, xn_ref, acc_ref = rest
    f = pl.program_id(1)

    @pl.when(f == 0)
    def _():
        xn_ref[...] = (_rms(x_ref[...]) * pre_ref[...]).astype(BF16)
        acc_ref[...] = jnp.zeros_like(acc_ref)

    xn = xn_ref[...]
    g = _dot(xn, wg_ref[...])
    u = _dot(xn, wu_ref[...])
    a = (g * _sigmoid(g) * u).astype(BF16)
    acc_ref[...] += _dot(a, wd_ref[...])

    @pl.when(f == pl.num_programs(1) - 1)
    def _():
        h = x_ref[...] + 0.5 * (_rms(acc_ref[...]) * post_ref[...])
        o_ref[...] = h
        if emit_next:
            u_ref[...] = (_rms(h) * nxt_ref[...]).astype(BF16)


def _ffn(x, pre_g, wg, wu, wd, post_g, nxt_g, emit_next):
    n, d = x.shape
    nf = wg.shape[1]
    tm, tf = min(FFN_TM, n), FFN_TF
    out_shape = [jax.ShapeDtypeStruct((n, d), F32)]
    out_specs = [pl.BlockSpec((tm, d), lambda i, f: (i, 0))]
    if emit_next:
        out_shape.append(jax.ShapeDtypeStruct((n, d), BF16))
        out_specs.append(pl.BlockSpec((tm, d), lambda i, f: (i, 0)))
    vec = pl.BlockSpec((1, d), lambda i, f: (0, 0))
    res = pl.pallas_call(
        functools.partial(_ffn_kernel, emit_next=emit_next),
        out_shape=out_shape,
        grid=(n // tm, nf // tf),
        in_specs=[
            pl.BlockSpec((tm, d), lambda i, f: (i, 0)),
            vec,
            pl.BlockSpec((d, tf), lambda i, f: (0, f)),
            pl.BlockSpec((d, tf), lambda i, f: (0, f)),
            pl.BlockSpec((tf, d), lambda i, f: (f, 0)),
            vec,
            vec,
        ],
        out_specs=out_specs,
        scratch_shapes=[pltpu.VMEM((tm, d), BF16), pltpu.VMEM((tm, d), F32)],
        compiler_params=_cparams(("parallel", "arbitrary")),
        name="ffn",
    )(x, pre_g, wg, wu, wd, post_g, nxt_g)
    return res if emit_next else (res[0], None)


def _proj_kernel(u_ref, wn_ref, wt_ref, us_ref, pn_ref, pt_ref):
    u = u_ref[0]
    nat = _dot(u, wn_ref[...])
    us_ref[...] = nat[:, :SSM_WIDTH].astype(us_ref.dtype)
    pn_ref[0] = nat[:, SSM_WIDTH:].astype(pn_ref.dtype)
    pt_ref[0] = _dot_nt(wt_ref[...], u).astype(pt_ref.dtype)


def _proj(u, wn, wt):
    b, t, d = u.shape
    tm = min(PROJ_TM, t)
    nk = wn.shape[1] - SSM_WIDTH
    return pl.pallas_call(
        _proj_kernel,
        out_shape=[jax.ShapeDtypeStruct((t, b * SSM_WIDTH), BF16),
                   jax.ShapeDtypeStruct((b, t, nk), BF16),
                   jax.ShapeDtypeStruct((b, wt.shape[0], t), BF16)],
        grid=(b, t // tm),
        in_specs=[
            pl.BlockSpec((1, tm, d), lambda i, j: (i, j, 0)),
            pl.BlockSpec(wn.shape, lambda i, j: (0, 0)),
            pl.BlockSpec(wt.shape, lambda i, j: (0, 0)),
        ],
        out_specs=[
            pl.BlockSpec((tm, SSM_WIDTH), lambda i, j: (j, i)),
            pl.BlockSpec((1, tm, nk), lambda i, j: (i, j, 0)),
            pl.BlockSpec((1, wt.shape[0], tm), lambda i, j: (i, 0, j)),
        ],
        compiler_params=_cparams(("parallel", "parallel")),
        name="proj",
    )(u, wn, wt)


def _s5_kernel(lhs_ref, wbu_ref, wc_ref, are_ref, aim_ref, dmat_ref, y_ref,
               xre_ref, xim_ref, bre_ref, bim_ref):
    lc = lhs_ref.shape[0] // 8

    @pl.when(pl.program_id(0) == 0)
    def _():
        xre_ref[...] = jnp.zeros_like(xre_ref)
        xim_ref[...] = jnp.zeros_like(xim_ref)

    row_odd = (lax.broadcasted_iota(jnp.int32, (lhs_ref.shape[0], 1), 0) % 2) == 1
    for j in range(16):
        jq = j // 4
        slab = lhs_ref[:, 128 * jq:128 * (jq + 1)]
        zero = jnp.zeros_like(slab)
        lj = jnp.concatenate([jnp.where(row_odd, zero, slab), jnp.where(row_odd, slab, zero)], axis=1)
        out = _dot(lj, wbu_ref[j])
        bre_ref[:, 128 * j:128 * (j + 1)] = out[:, :128]
        bim_ref[:, 128 * j:128 * (j + 1)] = out[:, 128:]

    for cc in range(bre_ref.shape[1] // S5_CW):
        cs = slice(cc * S5_CW, (cc + 1) * S5_CW)
        ar = are_ref[:, cs]
        ai = aim_ref[:, cs]

        def step(i, carry, cs=cs, ar=ar, ai=ai):
            xr, xi = carry
            rows = pl.ds(pl.multiple_of(i * 8, 8), 8)
            nr = ar * xr - ai * xi + bre_ref[rows, cs]
            ni = ar * xi + ai * xr + bim_ref[rows, cs]
            bre_ref[rows, cs] = nr
            bim_ref[rows, cs] = ni
            return nr, ni

        xr, xi = lax.fori_loop(0, lc, step, (xre_ref[:, cs], xim_ref[:, cs]))
        xre_ref[:, cs] = xr
        xim_ref[:, cs] = xi

    odd = (lax.broadcasted_iota(jnp.int32, (1, 8, 1), 1) % 2) == 1
    for jq in range(4):
        acc = None
        for jm in range(4):
            j = 4 * jq + jm
            xj = jnp.concatenate([bre_ref[:, 128 * j:128 * (j + 1)],
                                  bim_ref[:, 128 * j:128 * (j + 1)]], axis=1).astype(BF16)
            part = _dot(xj, wc_ref[j])
            acc = part if acc is None else acc + part
        acc3 = acc.reshape(lc, 8, 256)
        y = jnp.where(odd, acc3[:, :, 128:], acc3[:, :, :128])
        u_own = lhs_ref[:, 128 * jq:128 * (jq + 1)].astype(F32).reshape(lc, 8, 128)
        y = y + dmat_ref[:, 128 * jq:128 * (jq + 1)][None] * u_own
        y_ref[:, 128 * jq:128 * (jq + 1)] = _gelu(y).reshape(lc * 8, 128).astype(y_ref.dtype)


def _s5(lhs, wbu, wc, a_re, a_im, dmat):
    rows = lhs.shape[0]
    r = 8 * S5_LC
    nl = a_re.shape[1]
    return pl.pallas_call(
        _s5_kernel,
        out_shape=jax.ShapeDtypeStruct((rows, 512), BF16),
        grid=(rows // r,),
        in_specs=[
            pl.BlockSpec((r, 512), lambda c: (c, 0)),
            pl.BlockSpec(wbu.shape, lambda c: (0, 0, 0)),
            pl.BlockSpec(wc.shape, lambda c: (0, 0, 0)),
            pl.BlockSpec(a_re.shape, lambda c: (0, 0)),
            pl.BlockSpec(a_im.shape, lambda c: (0, 0)),
            pl.BlockSpec(dmat.shape, lambda c: (0, 0)),
        ],
        out_specs=pl.BlockSpec((r, 512), lambda c: (c, 0)),
        scratch_shapes=[pltpu.VMEM((8, nl), F32), pltpu.VMEM((8, nl), F32),
                        pltpu.VMEM((r, nl), F32), pltpu.VMEM((r, nl), F32)],
        compiler_params=_cparams(("arbitrary",)),
        name="s5",
    )(lhs, wbu, wc, a_re, a_im, dmat)


def _compress_kernel(r_ref, pos_ref, w1k_ref, w2k_ref, w1v_ref, w2vt_ref, k_ref, vt_ref,
                     zk_ref, zv_ref, pk_ref, pv_ref):
    s = pl.program_id(1)
    m = r_ref.shape[1]
    kvw = N_KV * HEAD_DIM

    @pl.when(s == 0)
    def _():
        for ref in (zk_ref, zv_ref, pk_ref, pv_ref):
            ref[...] = jnp.zeros_like(ref)

    r = r_ref[0]
    zk_ref[...] += _dot(r[:, :kvw], w1k_ref[0])
    zv_ref[...] += _dot(r[:, kvw:], w1v_ref[0])
    pk_ref[...] += _dot(pos_ref[0], w1k_ref[0])
    pv_ref[...] += _dot(pos_ref[0], w1v_ref[0])

    @pl.when(s == pl.num_programs(1) - 1)
    def _():
        k_ref[0, :, :CMP_FRONT, :] = jnp.zeros((N_KV, CMP_FRONT, HEAD_DIM), k_ref.dtype)
        vt_ref[0, :, :, :CMP_FRONT] = jnp.zeros((N_KV, HEAD_DIM, CMP_FRONT), vt_ref.dtype)

        def hidden(z_ref, p_ref, g):
            lo = slice(2 * CMP_HID * g, 2 * CMP_HID * g + CMP_HID)
            hi = slice(2 * CMP_HID * g + CMP_HID, 2 * CMP_HID * (g + 1))
            posvec = p_ref[0:1, lo] + p_ref[1:2, hi]
            return _gelu(z_ref[:, lo] + pltpu.roll(z_ref[:, hi], m - 1, 0) + posvec).astype(BF16)

        for g in range(N_KV):
            k_ref[0, g, CMP_FRONT:, :] = _dot(hidden(zk_ref, pk_ref, g), w2k_ref[...]).astype(k_ref.dtype)
            vt_ref[0, g, :, CMP_FRONT:] = _dot_nt(w2vt_ref[...], hidden(zv_ref, pv_ref, g)).astype(vt_ref.dtype)


def _compress(r, pos, w1k, w2k, w1v, w2vt):
    b, m, width = r.shape
    cols = width // CMP_STRIDE
    kvw = N_KV * HEAD_DIM
    ncp = CMP_FRONT + m
    nl = w1k.shape[2]
    full = lambda a: pl.BlockSpec(a.shape, lambda i, s: (0,) * a.ndim)
    slot = lambda a: pl.BlockSpec((1,) + a.shape[1:], lambda i, s: (s,) + (0,) * (a.ndim - 1))
    return pl.pallas_call(
        _compress_kernel,
        out_shape=[jax.ShapeDtypeStruct((b, N_KV, ncp, HEAD_DIM), BF16),
                   jax.ShapeDtypeStruct((b, N_KV, HEAD_DIM, ncp), BF16)],
        grid=(b, CMP_STRIDE),
        in_specs=[
            pl.BlockSpec((1, m, 2 * kvw), lambda i, s: (i, 0, s * (cols // (2 * kvw)))),
            slot(pos), slot(w1k), full(w2k), slot(w1v), full(w2vt),
        ],
        out_specs=[
            pl.BlockSpec((1, N_KV, ncp, HEAD_DIM), lambda i, s: (i, 0, 0, 0)),
            pl.BlockSpec((1, N_KV, HEAD_DIM, ncp), lambda i, s: (i, 0, 0, 0)),
        ],
        scratch_shapes=[pltpu.VMEM((m, nl), F32), pltpu.VMEM((m, nl), F32),
                        pltpu.VMEM((8, nl), F32), pltpu.VMEM((8, nl), F32)],
        compiler_params=_cparams(("parallel", "arbitrary")),
        name="compress",
    )(r, pos, w1k, w2k, w1v, w2vt)


def _nsa_kernel(qt_ref, gt_ref, kc_ref, vct_ref, ksl_ref, vslt_ref,
                kw0_ref, kw1_ref, kw2_ref, vw0_ref, vw1_ref, vw2_ref,
                bsel_ref, bwin_ref, bcmp_ref, cvec_ref, ovl_ref,
                o_ref,
                qpad_ref, sc_ref, nm_ref, m_ref, l_ref, acc_ref, oc_ref, ow_ref,
                sa_ref, sb_ref, ma_ref, mb_ref, imp_ref):
    g = pl.program_id(1)
    nq = pl.program_id(2)
    t0 = nq * TQ
    ncp = kc_ref.shape[2]
    ns = nm_ref.shape[0]
    cpq = TQ // CMP_STRIDE
    bpt = TQ // SEL_BLOCK

    q4 = qt_ref[0]
    qpad_ref[...] = jnp.zeros_like(qpad_ref)
    grow = pl.ds(pl.multiple_of(g * HEAD_DIM, HEAD_DIM), HEAD_DIM)
    for h in range(HPG):
        qpad_ref[grow, TQ * h:TQ * (h + 1)] = q4[HEAD_DIM * h:HEAD_DIM * (h + 1)]
    heads = [slice(TQ * h, TQ * (h + 1)) for h in range(HPG)]
    t_tok = t0 + lax.broadcasted_iota(jnp.int32, (1, TQ), 1)

    ch = CMP_FRONT
    hi_row = cpq * nq + CMP_FRONT + cpq
    n_live = (hi_row + ch - 1) // ch

    def chunk(c):
        return pl.ds(pl.multiple_of(c * ch, ch), ch)

    def cmp_scores(c, carry):
        row = c * ch + lax.broadcasted_iota(jnp.int32, (ch, 1), 0)
        for h, hs in enumerate(heads):
            s = _dot(kc_ref[0, 0, chunk(c), :], q4[HEAD_DIM * h:HEAD_DIM * (h + 1)])
            sc_ref[chunk(c), hs] = s + jnp.where(row < hi_row, cvec_ref[0, :, hs], NEG_INF)
        return carry

    sc_ref[0:ch, :] = jnp.full((ch, LQ), NEG_INF, F32)
    lax.fori_loop(1, n_live, cmp_scores, 0)
    band = pl.ds(pl.multiple_of(cpq * nq + CMP_FRONT - CMP_NEAR, 8), CMP_BAND)
    sc_ref[band, :] = sc_ref[band, :] + bcmp_ref[0]

    def cmp_max(c, m8):
        return jnp.maximum(m8, jnp.max(sc_ref[chunk(c), :].reshape(ch // 8, 8, LQ), axis=0))

    m_cmp = jnp.max(lax.fori_loop(1, n_live, cmp_max, jnp.full((8, LQ), -3e38, F32)), axis=0, keepdims=True)

    def cmp_exp(c, l8):
        e = jnp.exp2(sc_ref[chunk(c), :] - m_cmp)
        sc_ref[chunk(c), :] = e
        return l8 + jnp.sum(e.reshape(ch // 8, 8, LQ), axis=0)

    l_cmp = jnp.sum(lax.fori_loop(1, n_live, cmp_exp, jnp.zeros((8, LQ), F32)), axis=0, keepdims=True)
    has_cmp = (t_tok >= CMP_LEN - 1).astype(F32)
    p_scale = jnp.concatenate([has_cmp] * HPG, axis=1) / l_cmp
    oc_ref[...] = jnp.zeros_like(oc_ref)
    imp_ref[...] = jnp.zeros_like(imp_ref)

    def cmp_out(c, carry):
        p = sc_ref[chunk(c), :] * p_scale
        oc_ref[...] += _dot(vct_ref[0, 0, :, chunk(c)], p.astype(BF16))
        ps = p[:, heads[0]] + p[:, heads[1]] + p[:, heads[2]] + p[:, heads[3]]
        ps_hi = ps.astype(BF16)
        ps_lo = (ps - ps_hi.astype(F32)).astype(BF16)
        ovl = ovl_ref[:, chunk(c)]
        imp_ref[...] += _dot(ovl, ps_hi) + _dot(ovl, ps_lo)
        return carry

    lax.fori_loop(1, n_live, cmp_out, 0)

    def select_blocks():
        imp = imp_ref[...]
        blk = lax.broadcasted_iota(jnp.int32, (ns, TQ), 0)
        tt = t0 + lax.broadcasted_iota(jnp.int32, (ns, TQ), 1)
        valid_b = blk * SEL_BLOCK <= tt
        cur = tt // SEL_BLOCK
        forced = valid_b & ((blk == 0) | (blk == cur) | (blk == cur - 1))
        taken = -3e38
        sc = jnp.where(forced, taken, jnp.where(valid_b, imp, -SEL_FORCE))
        chosen = forced.astype(F32)
        blkf = blk.astype(F32)
        n_forced = 3
        for _ in range(min(N_SEL, ns) - n_forced):
            mx = jnp.max(sc, axis=0, keepdims=True)
            first = jnp.min(jnp.where(sc == mx, blkf, float(ns)), axis=0, keepdims=True)
            hit = blkf == first
            sc = jnp.where(hit, taken, sc)
            chosen = jnp.where(hit, 1.0, chosen)
        nm_ref[...] = jnp.where(chosen > 0.0, 0.0, NEG_INF)

    def reset():
        m_ref[...] = jnp.full_like(m_ref, NEG_INF)
        l_ref[...] = jnp.zeros_like(l_ref)
        acc_ref[...] = jnp.zeros_like(acc_ref)

    def scores(dst, keys, bias_fn):
        s_dst, bm_dst = dst
        for hs in heads:
            s = _dot(keys, qpad_ref[:, hs])
            for jj in range(keys.shape[0] // SEL_BLOCK):
                rows = slice(SEL_BLOCK * jj, SEL_BLOCK * (jj + 1))
                sb = s[rows] + bias_fn(rows, hs)
                s_dst[rows, hs] = sb
                bm_dst[8 * jj:8 * (jj + 1), hs] = jnp.max(sb.reshape(SEL_BLOCK // 8, 8, TQ), axis=0)

    def online(src, vt, masks):
        s_src, bm_src = src
        m_old = m_ref[...]
        vt1 = jnp.concatenate([vt, jnp.ones((16, vt.shape[1]), BF16)], axis=0)
        caps = [jnp.where(mk < 0.5 * NEG_INF, 0.0, 3e38).astype(BF16) for mk in masks]
        m_new, pv = [], []
        for hs in heads:
            smax = None
            for jj, mk in enumerate(masks):
                bm = bm_src[8 * jj:8 * (jj + 1), hs] + mk
                smax = bm if smax is None else jnp.maximum(smax, bm)
            mh = jnp.maximum(m_old[:, hs], jnp.max(smax, axis=0, keepdims=True))
            ps = [jnp.minimum(jnp.exp2(s_src[SEL_BLOCK * jj:SEL_BLOCK * (jj + 1), hs] - mh).astype(BF16), cap)
                  for jj, cap in enumerate(caps)]
            m_new.append(mh)
            pv.append(_dot(vt1, jnp.concatenate(ps, axis=0)))
        m_new = jnp.concatenate(m_new, axis=1)
        pv = jnp.concatenate(pv, axis=1)
        alpha = jnp.exp2(m_old - m_new)
        m_ref[...] = m_new
        l_ref[...] = alpha * l_ref[...] + pv[HEAD_DIM:HEAD_DIM + 1]
        acc_ref[...] = alpha * acc_ref[...] + pv[:HEAD_DIM]

    last = (nq * TQ) // TK

    def sel_scores(dst, kt):
        r0 = jnp.maximum(MAX_DIST + TK + TK * kt - t0, 0)
        scores(dst, ksl_ref[0, pl.ds(pl.multiple_of(kt * TK, TK), TK), :],
               lambda rows, hs: bsel_ref[0, pl.ds(pl.multiple_of(r0 + rows.start, SEL_BLOCK), SEL_BLOCK), hs])

    def sel_tile(cur, nxt, kt):
        sel_scores(nxt, jnp.minimum(kt + 1, last))
        gone = jnp.where(kt <= last, 0.0, NEG_INF)
        kt = jnp.minimum(kt, last)
        masks = [nm_ref[pl.ds(kt * (TK // SEL_BLOCK) + jj, 1), :] + gone for jj in range(TK // SEL_BLOCK)]
        online(cur, vslt_ref[0, :, pl.ds(pl.multiple_of(kt * TK, TK), TK)], masks)

    buf_a = (sa_ref, ma_ref)
    buf_b = (sb_ref, mb_ref)

    reset()
    kws = (kw0_ref, kw1_ref, kw2_ref)
    vws = (vw0_ref, vw1_ref, vw2_ref)

    def win_scores(dst, d):
        base = TQ * (N_WIN - 1 - d)
        scores(dst, kws[d][0], lambda rows, hs: bwin_ref[0, base + rows.start:base + rows.stop, hs])

    def win_online(src, d):
        gone = jnp.where(nq >= d, 0.0, NEG_INF) + jnp.zeros((1, TQ), F32)
        online(src, vws[d][0], [gone] * bpt)

    bufs = (buf_a, buf_b)
    win_scores(bufs[0], 0)
    for d in range(N_WIN):
        if d + 1 < N_WIN:
            win_scores(bufs[(d + 1) % 2], d + 1)
        win_online(bufs[d % 2], d)
    ow_ref[...] = acc_ref[...] / l_ref[...]
    reset()
    sel_scores(buf_a, 0)
    select_blocks()


    def tile_pair(i, c):
        sel_tile(buf_a, buf_b, 2 * i)
        sel_tile(buf_b, buf_a, 2 * i + 1)
        return c

    lax.fori_loop(0, (last + 2) // 2, tile_pair, 0)
    o_s = acc_ref[...] / l_ref[...]

    sg = _sigmoid(gt_ref[0].astype(F32))

    def gate(br):
        return jnp.concatenate([sg[HPG * br + h:HPG * br + h + 1] for h in range(HPG)], axis=1)

    ot = gate(0) * oc_ref[...] + gate(1) * o_s + gate(2) * ow_ref[...]
    stacked = jnp.concatenate([ot[:, TQ * h:TQ * (h + 1)] for h in range(HPG)], axis=0)
    o_ref[0] = stacked.T.astype(o_ref.dtype)


def _nsa(pn, pt, kc, vct, bsel, bwin, bcmp, cvec, ovl):
    b, t, _ = pn.shape
    nq = t // TQ
    ncp = kc.shape[2]
    ns = t // SEL_BLOCK
    q_rows = N_HEADS * HEAD_DIM
    gate_blk0 = (q_rows + 2 * N_KV * HEAD_DIM) // 16
    vsl_blk0 = q_rows // HEAD_DIM
    vw_blk0 = vsl_blk0 + N_KV
    ksl_col, kw_col = 2, 3

    def kw_spec(d):
        return pl.BlockSpec((1, TQ, 256), lambda i, j, n: (i, jnp.maximum(n - d, 0), kw_col))

    def vw_spec(d):
        return pl.BlockSpec((1, HEAD_DIM, TQ), lambda i, j, n: (i, vw_blk0 + j, jnp.maximum(n - d, 0)))

    per_g = lambda a: pl.BlockSpec((1,) + a.shape[1:], lambda i, j, n: (j,) + (0,) * (a.ndim - 1))
    return pl.pallas_call(
        _nsa_kernel,
        out_shape=jax.ShapeDtypeStruct((b, t, q_rows), BF16),
        grid=(b, N_KV, nq),
        in_specs=[
            pl.BlockSpec((1, HPG * HEAD_DIM, TQ), lambda i, j, n: (i, j, n)),
            pl.BlockSpec((1, 16, TQ), lambda i, j, n: (i, gate_blk0 + j, n)),
            pl.BlockSpec((1, 1, ncp, HEAD_DIM), lambda i, j, n: (i, j, 0, 0)),
            pl.BlockSpec((1, 1, HEAD_DIM, ncp), lambda i, j, n: (i, j, 0, 0)),
            pl.BlockSpec((1, t, 256), lambda i, j, n: (i, 0, ksl_col)),
            pl.BlockSpec((1, HEAD_DIM, t), lambda i, j, n: (i, vsl_blk0 + j, 0)),
            kw_spec(0), kw_spec(1), kw_spec(2),
            vw_spec(0), vw_spec(1), vw_spec(2),
            per_g(bsel), per_g(bwin), per_g(bcmp), per_g(cvec),
            pl.BlockSpec(ovl.shape, lambda i, j, n: (0, 0)),
        ],
        out_specs=pl.BlockSpec((1, TQ, HPG * HEAD_DIM), lambda i, j, n: (i, n, j)),
        scratch_shapes=[
            pltpu.VMEM((N_KV * HEAD_DIM, LQ), BF16),
            pltpu.VMEM((ncp, LQ), F32),
            pltpu.VMEM((ns, TQ), F32),
            pltpu.VMEM((1, LQ), F32), pltpu.VMEM((1, LQ), F32), pltpu.VMEM((HEAD_DIM, LQ), F32),
            pltpu.VMEM((HEAD_DIM, LQ), F32), pltpu.VMEM((HEAD_DIM, LQ), F32),
            pltpu.VMEM((TK, LQ), F32), pltpu.VMEM((TK, LQ), F32),
            pltpu.VMEM((8 * TK // SEL_BLOCK, LQ), F32), pltpu.VMEM((8 * TK // SEL_BLOCK, LQ), F32),
            pltpu.VMEM((ns, TQ), F32),
        ],
        compiler_params=_cparams(("parallel", "parallel", "arbitrary")),
        name="nsa",
    )(pt, pt, kc, vct, pn, pt, pn, pn, pn, pt, pt, pt, bsel, bwin, bcmp, cvec, ovl)


def _merge_kernel(ys_ref, o_ref, u_ref, w1_ref, w2_ref, wo_ref, wga_ref, wgb_ref, z_ref):
    ys = ys_ref[...]
    ya = _dot(ys, w1_ref[...]) * _sigmoid(_dot(ys, w2_ref[...]))
    yb = _dot(o_ref[...], wo_ref[...])
    u = u_ref[...]
    z = _sigmoid(_dot(u, wga_ref[...])) * ya + _sigmoid(_dot(u, wgb_ref[...])) * yb
    z_ref[...] = z.astype(z_ref.dtype)


def _merge(ys, o, u, w1, w2, wo, wga, wgb):
    t = ys.shape[0]
    n = o.shape[0]
    d = w1.shape[1]
    tm, tn = min(MERGE_TM, t), MERGE_TN
    tpb = t // tm
    row = lambda a: pl.BlockSpec((tm, a.shape[1]), lambda i, j: (i, 0))
    col = lambda a: pl.BlockSpec((a.shape[0], tn), lambda i, j: (0, j))
    return pl.pallas_call(
        _merge_kernel,
        out_shape=jax.ShapeDtypeStruct((n, d), BF16),
        grid=(n // tm, d // tn),
        in_specs=[pl.BlockSpec((tm, w1.shape[0]), lambda i, j: (i % tpb, i // tpb)),
                  row(o), row(u), col(w1), col(w2), col(wo), col(wga), col(wgb)],
        out_specs=pl.BlockSpec((tm, tn), lambda i, j: (i, j)),
        compiler_params=_cparams(("parallel", "parallel")),
        name="merge",
    )(ys, o, u, w1, w2, wo, wga, wgb)


def _outproj_kernel(z_ref, h_ref, w_ref, g_ref, o_ref):
    mixed = _dot(z_ref[...], w_ref[...])
    o_ref[...] = h_ref[...] + _rms(mixed) * g_ref[...]


def _outproj(z, h, w, g):
    n, d = h.shape
    tm = min(OUT_TM, n)
    return pl.pallas_call(
        _outproj_kernel,
        out_shape=jax.ShapeDtypeStruct((n, d), F32),
        grid=(n // tm,),
        in_specs=[
            pl.BlockSpec((tm, d), lambda i: (i, 0)),
            pl.BlockSpec((tm, d), lambda i: (i, 0)),
            pl.BlockSpec(w.shape, lambda i: (0, 0)),
            pl.BlockSpec((1, d), lambda i: (0, 0)),
        ],
        out_specs=pl.BlockSpec((tm, d), lambda i: (i, 0)),
        compiler_params=_cparams(("parallel",)),
        name="outproj",
    )(z, h, w, g)


def _rel_bucket(dist):
    dist = jnp.maximum(dist, 0)
    max_exact = N_BUCKETS // 2
    d_f = jnp.maximum(dist, 1).astype(jnp.float32)
    large = max_exact + (jnp.log(d_f / max_exact) / math.log(MAX_DIST / max_exact)
                         * (N_BUCKETS - max_exact)).astype(jnp.int32)
    large = jnp.minimum(large, N_BUCKETS - 1)
    return jnp.where(dist < max_exact, dist, large)


def _bias_tables(rel_bias):
    table = (LOG2E * rel_bias).reshape(N_BUCKETS, N_KV, HPG)
    far = LOG2E * rel_bias[_rel_bucket(jnp.asarray(MAX_DIST))]
    cvec = jnp.broadcast_to(far.reshape(N_KV, 1, HPG, 1), (N_KV, 1, HPG, TQ)).reshape(N_KV, 1, LQ)
    tok = jnp.arange(TQ)[None, :]
    n_sel, n_win = MAX_DIST + 2 * TK, WINDOW + TQ
    d_sel = tok + MAX_DIST + TK - jnp.arange(n_sel)[:, None]
    d_win = tok + WINDOW - jnp.arange(n_win)[:, None]
    d_cmp = tok - CMP_STRIDE * (jnp.arange(CMP_BAND)[:, None] - CMP_NEAR) - (CMP_LEN - 1)
    dist = jnp.concatenate([d_sel, d_win, d_cmp], axis=0)
    ok = jnp.concatenate([d_sel >= 0, (d_win >= 0) & (d_win < WINDOW), d_cmp >= 0], axis=0)
    onehot = jax.nn.one_hot(_rel_bucket(dist), N_BUCKETS, dtype=F32)
    v = jnp.einsum('rtb,bgh->grht', onehot, table, precision=lax.Precision.HIGHEST)
    v = jnp.where(ok[None, :, None, :], v, NEG_INF).reshape(N_KV, dist.shape[0], LQ)
    bsel, bwin, bcmp = v[:, :n_sel], v[:, n_sel:n_sel + n_win], v[:, n_sel + n_win:]
    bcmp = jnp.where(bcmp > 0.5 * NEG_INF, bcmp - cvec, NEG_INF)
    return bsel, bwin, bcmp, cvec


def _overlap_t(t):
    nc = t // CMP_STRIDE
    cmp_start = np.arange(nc) * CMP_STRIDE
    sel_start = np.arange(t // SEL_BLOCK) * SEL_BLOCK
    ov = ((cmp_start[None, :] < sel_start[:, None] + SEL_BLOCK)
          & (cmp_start[None, :] + CMP_LEN > sel_start[:, None]))
    ov[:, nc - 1] = False
    out = np.zeros((t // SEL_BLOCK, CMP_FRONT + nc), np.float32)
    out[:, CMP_FRONT:] = ov
    return jnp.asarray(out, BF16)


def _s5_tables(a_re, a_im, log_dt, b_re, b_im, c_re, c_im, d_skip, batch):
    dt = jnp.exp(log_dt)[:, None]
    lam_re = jnp.minimum(a_re, -1e-4)
    lam_im = a_im
    mag = jnp.exp(lam_re * dt)
    ab_re = mag * jnp.cos(lam_im * dt)
    ab_im = mag * jnp.sin(lam_im * dt)
    den = lam_re * lam_re + lam_im * lam_im
    n_re = ab_re - 1.0
    n_im = ab_im
    co_re = (n_re * lam_re + n_im * lam_im) / den
    co_im = (n_im * lam_re - n_re * lam_im) / den
    bb_re = co_re[..., None] * b_re - co_im[..., None] * b_im
    bb_im = co_re[..., None] * b_im + co_im[..., None] * b_re
    sel = jnp.asarray(np.eye(4, dtype=np.float32)[np.arange(16) % 4])
    eye2 = jnp.eye(2, dtype=F32)
    bb = jnp.stack([bb_re, bb_im], 0).reshape(2, 2, 16, 2, SSM_STATE, SSM_GROUP)
    wbu = jnp.einsum('rhjgpc,jm,gk->jhmgcrkp', bb, sel, eye2).reshape(16, 256, 256)
    cc = jnp.stack([c_re, -c_im], 0).reshape(2, 2, 16, 2, SSM_GROUP, SSM_STATE)
    wc = jnp.einsum('rhjgcp,jm,gk->jrkphmgc', cc, sel, eye2).reshape(16, 256, 256)

    def lanes(a):
        a = a.reshape(2, 1, 16 * 2 * SSM_STATE)
        return jnp.broadcast_to(a.transpose(1, 0, 2), (batch, 2, 2048)).reshape(2 * batch, 2048)

    dmat = jnp.broadcast_to(d_skip.reshape(1, 2, 512), (batch, 2, 512)).reshape(2 * batch, 512)
    return wbu.astype(BF16), wc.astype(BF16), lanes(ab_re), lanes(ab_im), dmat


def kernel(x, ffn1_pre_g, ffn1_w_gate, ffn1_w_up, ffn1_w_down, ffn1_post_g, mix_pre_g, w_in, ssm_a_re, ssm_a_im, ssm_log_dt, ssm_b_re, ssm_b_im, ssm_c_re, ssm_c_im, ssm_d, ssm_glu_w1, ssm_glu_w2, cmp_pos, cmp_k_w1, cmp_k_w2, cmp_v_w1, cmp_v_w2, nsa_w_o, w_out, mix_post_g, ffn2_pre_g, ffn2_w_gate, ffn2_w_up, ffn2_w_down, ffn2_post_g, rel_bias):
    b, t, d = x.shape
    n = b * t
    assert b == 4 and t % TK == 0 and TK % TQ == 0 and t >= MAX_DIST + TK and d == D_MODEL
    bf = lambda a: a.astype(BF16)
    h = x.reshape(n, d)
    for l in range(ffn1_pre_g.shape[0]):
        h, u = _ffn(h, ffn1_pre_g[l][None], bf(ffn1_w_gate[l]), bf(ffn1_w_up[l]), bf(ffn1_w_down[l]),
                    ffn1_post_g[l][None], mix_pre_g[l][None], True)

        w = w_in[l]
        o_q = SSM_WIDTH
        o_kv = o_q + N_HEADS * HEAD_DIM
        kvw = N_KV * HEAD_DIM
        o_gn = o_kv + 6 * kvw
        o_ga = o_gn + 3 * N_HEADS
        kv = lambda i: w[:, o_kv + i * kvw:o_kv + (i + 1) * kvw]
        wn = bf(jnp.concatenate([w[:, :o_q], kv(0), kv(1), kv(2), kv(4)], axis=1))
        gate_cols = np.full((N_KV, 4, HPG), -1)
        for g in range(N_KV):
            for br in range(3):
                for hh in range(HPG):
                    gate_cols[g, br, hh] = o_gn + 3 * HPG * g + 3 * hh + br
        gate_cols = gate_cols.reshape(-1)
        wg = jnp.where((gate_cols >= 0)[None, :], w[:, np.maximum(gate_cols, 0)], 0.0)
        w_q = w[:, o_q:o_kv] * (HEAD_DIM ** -0.5 * LOG2E)
        wt = bf(jnp.concatenate([w_q, kv(3), kv(5), wg], axis=1).T)
        us, pn, pt = _proj(u.reshape(b, t, d), wn, wt)

        wbu, wc, a_re, a_im, dmat = _s5_tables(ssm_a_re[l], ssm_a_im[l], ssm_log_dt[l], ssm_b_re[l],
                                               ssm_b_im[l], ssm_c_re[l], ssm_c_im[l], ssm_d[l], b)
        ys = _s5(us.reshape(t * 2 * b, 512), wbu, wc, a_re, a_im, dmat).reshape(t, b * SSM_WIDTH)

        half = CMP_STRIDE * HEAD_DIM
        eye_g = jnp.eye(N_KV, dtype=F32)

        def w1_slots(w1):
            cat = jnp.concatenate([w1[:half], w1[half:]], axis=1).reshape(CMP_STRIDE, HEAD_DIM, 2 * CMP_HID)
            return bf(jnp.einsum('sdc,gh->sgdhc', cat, eye_g).reshape(CMP_STRIDE, kvw, N_KV * 2 * CMP_HID))

        pos = jnp.stack([cmp_pos[l][:CMP_STRIDE], cmp_pos[l][CMP_STRIDE:]], axis=1)
        pos = bf(jnp.pad(jnp.tile(pos, (1, 1, N_KV)), ((0, 0), (0, 6), (0, 0))))
        kc, vct = _compress(pn.reshape(b, t // CMP_STRIDE, CMP_STRIDE * pn.shape[2]), pos,
                            w1_slots(cmp_k_w1[l]), bf(cmp_k_w2[l]), w1_slots(cmp_v_w1[l]), bf(cmp_v_w2[l].T))
        bsel, bwin, bcmp, cvec = _bias_tables(rel_bias)
        o_nsa = _nsa(pn, pt, kc, vct, bsel, bwin, bcmp, cvec, _overlap_t(t))

        z = _merge(ys, o_nsa.reshape(n, N_HEADS * HEAD_DIM), u, bf(ssm_glu_w1[l]), bf(ssm_glu_w2[l]),
                   bf(nsa_w_o[l]), bf(w[:, o_ga:o_ga + d]), bf(w[:, o_ga + d:o_ga + 2 * d]))
        h = _outproj(z, h, bf(w_out[l]), mix_post_g[l][None])

        h, _ = _ffn(h, ffn2_pre_g[l][None], bf(ffn2_w_gate[l]), bf(ffn2_w_up[l]), bf(ffn2_w_down[l]),
                    ffn2_post_g[l][None], ffn2_post_g[l][None], False)
    return h.reshape(b, t, d)
```

```python
import functools
import math

import numpy as np
import jax
import jax.numpy as jnp
from jax import lax
from jax.experimental import pallas as pl
from jax.experimental.pallas import tpu as pltpu

F32 = jnp.float32
BF16 = jnp.bfloat16

D_MODEL = 2048
D_FF = 5632
EPS = 1e-6
SSM_WIDTH = 1024
SSM_GROUP = 16
SSM_GROUPS = 64
SSM_STATE = 64
N_HEADS = 16
N_KV = 4
HPG = 4
HEAD_DIM = 64
CMP_LEN = 32
CMP_STRIDE = 16
CMP_HID = 256
SEL_BLOCK = 64
N_SEL = 16
WINDOW = 512
SEL_FORCE = 1e4
NEG_INF = -1e30
LOG2E = 1.4426950408889634
N_BUCKETS = 32
MAX_DIST = 1024

VMEM_LIMIT = 56 * 1024 * 1024

FFN_TM = 512
FFN_TF = 512
PROJ_TM = 512
S5_LC = 64
S5_CW = 512
TQ = 256
TK = 512
LQ = HPG * TQ
CMP_FRONT = 128
CMP_NEAR = 72
CMP_BAND = CMP_NEAR + TQ // CMP_STRIDE
N_WIN = WINDOW // TQ + 1
MERGE_TM = 1024
MERGE_TN = 512
OUT_TM = 512


def _cparams(sem):
    return pltpu.CompilerParams(dimension_semantics=sem, vmem_limit_bytes=VMEM_LIMIT)


def _rms(x):
    ms = jnp.mean(x * x, axis=-1, keepdims=True)
    return x * lax.rsqrt(ms + EPS)


def _sigmoid(x):
    return 1.0 / (1.0 + jnp.exp(-x))


def _gelu(x):
    return 0.5 * x * (1.0 + jnp.tanh(math.sqrt(2.0 / math.pi) * (x + 0.044715 * (x * x * x))))


def _dot(a, b):
    return jnp.dot(a, b, preferred_element_type=F32)


def _dot_nt(a, b):
    return lax.dot_general(a, b, (((1,), (1,)), ((), ())), preferred_element_type=F32)


def _ffn_kernel(x_ref, pre_ref, wg_ref, wu_ref, wd_ref, post_ref, nxt_ref, *rest, emit_next):
    if emit_next:
        o_ref, u_ref, xn_ref, acc_ref = rest
    else:
        o_ref, xn_ref, acc_ref = rest
    f = pl.program_id(1)

    @pl.when(f == 0)
    def _():
        xn_ref[...] = (_rms(x_ref[...]) * pre_ref[...]).astype(BF16)
        acc_ref[...] = jnp.zeros_like(acc_ref)

    xn = xn_ref[...]
    g = _dot(xn, wg_ref[...])
    u = _dot(xn, wu_ref[...])
    a = (g * _sigmoid(g) * u).astype(BF16)
    acc_ref[...] += _dot(a, wd_ref[...])

    @pl.when(f == pl.num_programs(1) - 1)
    def _():
        h = x_ref[...] + 0.5 * (_rms(acc_ref[...]) * post_ref[...])
        o_ref[...] = h
        if emit_next:
            u_ref[...] = (_rms(h) * nxt_ref[...]).astype(BF16)


def _ffn(x, pre_g, wg, wu, wd, post_g, nxt_g, emit_next):
    n, d = x.shape
    nf = wg.shape[1]
    tm, tf = min(FFN_TM, n), FFN_TF
    out_shape = [jax.ShapeDtypeStruct((n, d), F32)]
    out_specs = [pl.BlockSpec((tm, d), lambda i, f: (i, 0))]
    if emit_next:
        out_shape.append(jax.ShapeDtypeStruct((n, d), BF16))
        out_specs.append(pl.BlockSpec((tm, d), lambda i, f: (i, 0)))
    vec = pl.BlockSpec((1, d), lambda i, f: (0, 0))
    res = pl.pallas_call(
        functools.partial(_ffn_kernel, emit_next=emit_next),
        out_shape=out_shape,
        grid=(n // tm, nf // tf),
        in_specs=[
            pl.BlockSpec((tm, d), lambda i, f: (i, 0)),
            vec,
            pl.BlockSpec((d, tf), lambda i, f: (0, f)),
            pl.BlockSpec((d, tf), lambda i, f: (0, f)),
            pl.BlockSpec((tf, d), lambda i, f: (f, 0)),
            vec,
            vec,
        ],
        out_specs=out_specs,
        scratch_shapes=[pltpu.VMEM((tm, d), BF16), pltpu.VMEM((tm, d), F32)],
        compiler_params=_cparams(("parallel", "arbitrary")),
        name="ffn",
    )(x, pre_g, wg, wu, wd, post_g, nxt_g)
    return res if emit_next else (res[0], None)


def _proj_kernel(u_ref, wn_ref, wt_ref, us_ref, pn_ref, pt_ref):
    u = u_ref[0]
    nat = _dot(u, wn_ref[...])
    us_ref[...] = nat[:, :SSM_WIDTH].astype(us_ref.dtype)
    pn_ref[0] = nat[:, SSM_WIDTH:].astype(pn_ref.dtype)
    pt_ref[0] = _dot_nt(wt_ref[...], u).astype(pt_ref.dtype)


def _proj(u, wn, wt):
    b, t, d = u.shape
    tm = min(PROJ_TM, t)
    nk = wn.shape[1] - SSM_WIDTH
    return pl.pallas_call(
        _proj_kernel,
        out_shape=[jax.ShapeDtypeStruct((t, b * SSM_WIDTH), BF16),
                   jax.ShapeDtypeStruct((b, t, nk), BF16),
                   jax.ShapeDtypeStruct((b, wt.shape[0], t), BF16)],
        grid=(b, t // tm),
        in_specs=[
            pl.BlockSpec((1, tm, d), lambda i, j: (i, j, 0)),
            pl.BlockSpec(wn.shape, lambda i, j: (0, 0)),
            pl.BlockSpec(wt.shape, lambda i, j: (0, 0)),
        ],
        out_specs=[
            pl.BlockSpec((tm, SSM_WIDTH), lambda i, j: (j, i)),
            pl.BlockSpec((1, tm, nk), lambda i, j: (i, j, 0)),
            pl.BlockSpec((1, wt.shape[0], tm), lambda i, j: (i, 0, j)),
        ],
        compiler_params=_cparams(("parallel", "parallel")),
        name="proj",
    )(u, wn, wt)


def _s5_kernel(lhs_ref, wbu_ref, wc_ref, are_ref, aim_ref, dmat_ref, y_ref,
               xre_ref, xim_ref, bre_ref, bim_ref):
    lc = lhs_ref.shape[0] // 8

    @pl.when(pl.program_id(0) == 0)
    def _():
        xre_ref[...] = jnp.zeros_like(xre_ref)
        xim_ref[...] = jnp.zeros_like(xim_ref)

    row_odd = (lax.broadcasted_iota(jnp.int32, (lhs_ref.shape[0], 1), 0) % 2) == 1
    for j in range(16):
        jq = j // 4
        slab = lhs_ref[:, 128 * jq:128 * (jq + 1)]
        zero = jnp.zeros_like(slab)
        lj = jnp.concatenate([jnp.where(row_odd, zero, slab), jnp.where(row_odd, slab, zero)], axis=1)
        out = _dot(lj, wbu_ref[j])
        bre_ref[:, 128 * j:128 * (j + 1)] = out[:, :128]
        bim_ref[:, 128 * j:128 * (j + 1)] = out[:, 128:]

    for cc in range(bre_ref.shape[1] // S5_CW):
        cs = slice(cc * S5_CW, (cc + 1) * S5_CW)
        ar = are_ref[:, cs]
        ai = aim_ref[:, cs]

        def step(i, carry, cs=cs, ar=ar, ai=ai):
            xr, xi = carry
            rows = pl.ds(pl.multiple_of(i * 8, 8), 8)
            nr = ar * xr - ai * xi + bre_ref[rows, cs]
            ni = ar * xi + ai * xr + bim_ref[rows, cs]
            bre_ref[rows, cs] = nr
            bim_ref[rows, cs] = ni
            return nr, ni

        xr, xi = lax.fori_loop(0, lc, step, (xre_ref[:, cs], xim_ref[:, cs]))
        xre_ref[:, cs] = xr
        xim_ref[:, cs] = xi

    odd = (lax.broadcasted_iota(jnp.int32, (1, 8, 1), 1) % 2) == 1
    for jq in range(4):
        acc = None
        for jm in range(4):
            j = 4 * jq + jm
            xj = jnp.concatenate([bre_ref[:, 128 * j:128 * (j + 1)],
                                  bim_ref[:, 128 * j:128 * (j + 1)]], axis=1).astype(BF16)
            part = _dot(xj, wc_ref[j])
            acc = part if acc is None else acc + part
        acc3 = acc.reshape(lc, 8, 256)
        y = jnp.where(odd, acc3[:, :, 128:], acc3[:, :, :128])
        u_own = lhs_ref[:, 128 * jq:128 * (jq + 1)].astype(F32).reshape(lc, 8, 128)
        y = y + dmat_ref[:, 128 * jq:128 * (jq + 1)][None] * u_own
        y_ref[:, 128 * jq:128 * (jq + 1)] = _gelu(y).reshape(lc * 8, 128).astype(y_ref.dtype)


def _s5(lhs, wbu, wc, a_re, a_im, dmat):
    rows = lhs.shape[0]
    r = 8 * S5_LC
    nl = a_re.shape[1]
    return pl.pallas_call(
        _s5_kernel,
        out_shape=jax.ShapeDtypeStruct((rows, 512), BF16),
        grid=(rows // r,),
        in_specs=[
            pl.BlockSpec((r, 512), lambda c: (c, 0)),
            pl.BlockSpec(wbu.shape, lambda c: (0, 0, 0)),
            pl.BlockSpec(wc.shape, lambda c: (0, 0, 0)),
            pl.BlockSpec(a_re.shape, lambda c: (0, 0)),
            pl.BlockSpec(a_im.shape, lambda c: (0, 0)),
            pl.BlockSpec(dmat.shape, lambda c: (0, 0)),
        ],
        out_specs=pl.BlockSpec((r, 512), lambda c: (c, 0)),
        scratch_shapes=[pltpu.VMEM((8, nl), F32), pltpu.VMEM((8, nl), F32),
                        pltpu.VMEM((r, nl), F32), pltpu.VMEM((r, nl), F32)],
        compiler_params=_cparams(("arbitrary",)),
        name="s5",
    )(lhs, wbu, wc, a_re, a_im, dmat)


def _compress_kernel(rk_ref, rv_ref, pos_ref, w1k_ref, w2k_ref, w1v_ref, w2vt_ref, k_ref, vt_ref):
    m = rk_ref.shape[2]

    def hidden(r_ref, w1_ref):
        z = _dot(r_ref[0, 0], w1_ref[...])
        pz = _dot(pos_ref[...], w1_ref[...])
        posvec = pz[0:1, :CMP_HID] + pz[1:2, CMP_HID:]
        zhi = pltpu.roll(z[:, CMP_HID:], m - 1, 0)
        return _gelu(z[:, :CMP_HID] + zhi + posvec).astype(BF16)

    hk = hidden(rk_ref, w1k_ref)
    k_ref[0, 0, :CMP_FRONT, :] = jnp.zeros((CMP_FRONT, HEAD_DIM), k_ref.dtype)
    k_ref[0, 0, CMP_FRONT:, :] = _dot(hk, w2k_ref[...]).astype(k_ref.dtype)
    hv = hidden(rv_ref, w1v_ref)
    vt_ref[0, 0, :, :CMP_FRONT] = jnp.zeros((HEAD_DIM, CMP_FRONT), vt_ref.dtype)
    vt_ref[0, 0, :, CMP_FRONT:] = _dot_nt(w2vt_ref[...], hv).astype(vt_ref.dtype)


def _compress(rk, rv, pos8, w1k, w2k, w1v, w2vt):
    b, g, m, kd = rk.shape
    ncp = CMP_FRONT + m
    full = lambda a: pl.BlockSpec(a.shape, lambda i, j: (0,) * a.ndim)
    return pl.pallas_call(
        _compress_kernel,
        out_shape=[jax.ShapeDtypeStruct((b, g, ncp, HEAD_DIM), BF16),
                   jax.ShapeDtypeStruct((b, g, HEAD_DIM, ncp), BF16)],
        grid=(b, g),
        in_specs=[
            pl.BlockSpec((1, 1, m, kd), lambda i, j: (i, j, 0, 0)),
            pl.BlockSpec((1, 1, m, kd), lambda i, j: (i, j, 0, 0)),
            full(pos8), full(w1k), full(w2k), full(w1v), full(w2vt),
        ],
        out_specs=[
            pl.BlockSpec((1, 1, ncp, HEAD_DIM), lambda i, j: (i, j, 0, 0)),
            pl.BlockSpec((1, 1, HEAD_DIM, ncp), lambda i, j: (i, j, 0, 0)),
        ],
        compiler_params=_cparams(("parallel", "parallel")),
        name="compress",
    )(rk, rv, pos8, w1k, w2k, w1v, w2vt)


def _nsa_kernel(qt_ref, gt_ref, kc_ref, vct_ref, ksl_ref, vslt_ref,
                kw0_ref, kw1_ref, kw2_ref, vw0_ref, vw1_ref, vw2_ref,
                bsel_ref, bwin_ref, bcmp_ref, cvec_ref, ovl_ref,
                o_ref,
                qpad_ref, sc_ref, nm_ref, m_ref, l_ref, acc_ref, oc_ref, ow_ref,
                sa_ref, sb_ref, ma_ref, mb_ref, imp_ref):
    g = pl.program_id(1)
    nq = pl.program_id(2)
    t0 = nq * TQ
    ncp = kc_ref.shape[2]
    ns = nm_ref.shape[0]
    cpq = TQ // CMP_STRIDE
    bpt = TQ // SEL_BLOCK

    q4 = qt_ref[0]

    @pl.when(nq == 0)
    def _():
        qpad_ref[...] = jnp.zeros_like(qpad_ref)

    grow = pl.ds(pl.multiple_of(g * HEAD_DIM, HEAD_DIM), HEAD_DIM)
    for h in range(HPG):
        qpad_ref[grow, TQ * h:TQ * (h + 1)] = q4[HEAD_DIM * h:HEAD_DIM * (h + 1)]
    heads = [slice(TQ * h, TQ * (h + 1)) for h in range(HPG)]
    t_tok = t0 + lax.broadcasted_iota(jnp.int32, (1, TQ), 1)

    ch = CMP_FRONT
    hi_row = cpq * nq + CMP_FRONT + cpq
    n_live = (hi_row + ch - 1) // ch

    def chunk(c):
        return pl.ds(pl.multiple_of(c * ch, ch), ch)

    def cmp_scores(c, carry):
        row = c * ch + lax.broadcasted_iota(jnp.int32, (ch, 1), 0)
        for h, hs in enumerate(heads):
            s = _dot(kc_ref[0, 0, chunk(c), :], q4[HEAD_DIM * h:HEAD_DIM * (h + 1)])
            sc_ref[chunk(c), hs] = s + jnp.where(row < hi_row, cvec_ref[0, :, hs], NEG_INF)
        return carry

    band_row = cpq * nq + CMP_FRONT - CMP_NEAR

    @pl.when(band_row < ch)
    def _():
        sc_ref[0:ch, :] = jnp.full((ch, LQ), NEG_INF, F32)

    lax.fori_loop(1, n_live, cmp_scores, 0)
    band = pl.ds(pl.multiple_of(band_row, 8), CMP_BAND)
    sc_ref[band, :] = sc_ref[band, :] + bcmp_ref[0]

    def cmp_max(c, m8):
        return jnp.maximum(m8, jnp.max(sc_ref[chunk(c), :].reshape(ch // 8, 8, LQ), axis=0))

    m_cmp = jnp.max(lax.fori_loop(1, n_live, cmp_max, jnp.full((8, LQ), -3e38, F32)), axis=0, keepdims=True)

    def cmp_exp(c, l8):
        e = jnp.exp2(sc_ref[chunk(c), :] - m_cmp)
        sc_ref[chunk(c), :] = e
        return l8 + jnp.sum(e.reshape(ch // 8, 8, LQ), axis=0)

    l_cmp = jnp.sum(lax.fori_loop(1, n_live, cmp_exp, jnp.zeros((8, LQ), F32)), axis=0, keepdims=True)
    has_cmp = (t_tok >= CMP_LEN - 1).astype(F32)
    p_scale = jnp.concatenate([has_cmp] * HPG, axis=1) / l_cmp
    oc_ref[...] = jnp.zeros_like(oc_ref)
    imp_ref[...] = jnp.zeros_like(imp_ref)

    def cmp_out(c, carry):
        p = sc_ref[chunk(c), :] * p_scale
        oc_ref[...] += _dot(vct_ref[0, 0, :, chunk(c)], p.astype(BF16))
        ps = p[:, heads[0]] + p[:, heads[1]] + p[:, heads[2]] + p[:, heads[3]]
        ps_hi = ps.astype(BF16)
        ps_lo = (ps - ps_hi.astype(F32)).astype(BF16)
        ovl = ovl_ref[:, chunk(c)]
        imp_ref[...] += _dot(ovl, ps_hi) + _dot(ovl, ps_lo)
        return carry

    lax.fori_loop(1, n_live, cmp_out, 0)

    def select_blocks():
        imp = imp_ref[...]
        blk = lax.broadcasted_iota(jnp.int32, (ns, TQ), 0)
        tt = t0 + lax.broadcasted_iota(jnp.int32, (ns, TQ), 1)
        valid_b = blk * SEL_BLOCK <= tt
        cur = tt // SEL_BLOCK
        forced = valid_b & ((blk == 0) | (blk == cur) | (blk == cur - 1))
        taken = -3e38
        sc = jnp.where(forced, taken, jnp.where(valid_b, imp, -SEL_FORCE))
        blkf = blk.astype(F32)
        n_forced = 3
        for _ in range(min(N_SEL, ns) - n_forced):
            mx = jnp.max(sc, axis=0, keepdims=True)
            first = jnp.min(jnp.where(sc == mx, blkf, float(ns)), axis=0, keepdims=True)
            sc = jnp.where(blkf == first, taken, sc)
        nm_ref[...] = jnp.where(sc == taken, 0.0, NEG_INF)

    def reset():
        m_ref[...] = jnp.full_like(m_ref, NEG_INF)
        l_ref[...] = jnp.zeros_like(l_ref)
        acc_ref[...] = jnp.zeros_like(acc_ref)

    def scores(dst, keys, bias_fn):
        s_dst, bm_dst = dst
        for hs in heads:
            s = _dot(keys, qpad_ref[:, hs])
            for jj in range(keys.shape[0] // SEL_BLOCK):
                rows = slice(SEL_BLOCK * jj, SEL_BLOCK * (jj + 1))
                sb = s[rows] + bias_fn(rows, hs)
                s_dst[rows, hs] = sb
                bm_dst[8 * jj:8 * (jj + 1), hs] = jnp.max(sb.reshape(SEL_BLOCK // 8, 8, TQ), axis=0)

    def online(src, vt, masks):
        s_src, bm_src = src
        m_old = m_ref[...]
        vt1 = jnp.concatenate([vt, jnp.ones((16, vt.shape[1]), BF16)], axis=0)
        caps = [jnp.where(mk < 0.5 * NEG_INF, 0.0, 3e38).astype(BF16) for mk in masks]
        m_new, pv = [], []
        for hs in heads:
            smax = None
            for jj, mk in enumerate(masks):
                bm = bm_src[8 * jj:8 * (jj + 1), hs] + mk
                smax = bm if smax is None else jnp.maximum(smax, bm)
            mh = jnp.maximum(m_old[:, hs], jnp.max(smax, axis=0, keepdims=True))
            ps = [jnp.minimum(jnp.exp2(s_src[SEL_BLOCK * jj:SEL_BLOCK * (jj + 1), hs] - mh).astype(BF16), cap)
                  for jj, cap in enumerate(caps)]
            m_new.append(mh)
            pv.append(_dot(vt1, jnp.concatenate(ps, axis=0)))
        m_new = jnp.concatenate(m_new, axis=1)
        pv = jnp.concatenate(pv, axis=1)
        alpha = jnp.exp2(m_old - m_new)
        m_ref[...] = m_new
        l_ref[...] = alpha * l_ref[...] + pv[HEAD_DIM:HEAD_DIM + 1]
        acc_ref[...] = alpha * acc_ref[...] + pv[:HEAD_DIM]

    last = (nq * TQ) // TK

    def sel_scores(dst, kt):
        r0 = jnp.maximum(MAX_DIST + TK + TK * kt - t0, 0)
        scores(dst, ksl_ref[0, pl.ds(pl.multiple_of(kt * TK, TK), TK), :],
               lambda rows, hs: bsel_ref[0, pl.ds(pl.multiple_of(r0 + rows.start, SEL_BLOCK), SEL_BLOCK), hs])

    def sel_tile(cur, nxt, kt):
        sel_scores(nxt, jnp.minimum(kt + 1, last))
        gone = jnp.where(kt <= last, 0.0, NEG_INF)
        kt = jnp.minimum(kt, last)
        masks = [nm_ref[pl.ds(kt * (TK // SEL_BLOCK) + jj, 1), :] + gone for jj in range(TK // SEL_BLOCK)]
        online(cur, vslt_ref[0, :, pl.ds(pl.multiple_of(kt * TK, TK), TK)], masks)

    buf_a = (sa_ref, ma_ref)
    buf_b = (sb_ref, mb_ref)

    reset()
    kws = (kw0_ref, kw1_ref, kw2_ref)
    vws = (vw0_ref, vw1_ref, vw2_ref)

    def win_scores(dst, d):
        base = TQ * (N_WIN - 1 - d)
        scores(dst, kws[d][0], lambda rows, hs: bwin_ref[0, base + rows.start:base + rows.stop, hs])

    def win_online(src, d):
        gone = jnp.where(nq >= d, 0.0, NEG_INF) + jnp.zeros((1, TQ), F32)
        online(src, vws[d][0], [gone] * bpt)

    bufs = (buf_a, buf_b)
    win_scores(bufs[0], 0)
    for d in range(N_WIN):
        if d + 1 < N_WIN:
            win_scores(bufs[(d + 1) % 2], d + 1)
        win_online(bufs[d % 2], d)
    ow_ref[...] = acc_ref[...] / l_ref[...]
    reset()
    sel_scores(buf_a, 0)
    select_blocks()


    def tile_pair(i, c):
        sel_tile(buf_a, buf_b, 2 * i)
        sel_tile(buf_b, buf_a, 2 * i + 1)
        return c

    lax.fori_loop(0, (last + 2) // 2, tile_pair, 0)
    o_s = acc_ref[...] / l_ref[...]

    sg = _sigmoid(gt_ref[0].astype(F32))

    def gate(br):
        return jnp.concatenate([sg[HPG * br + h:HPG * br + h + 1] for h in range(HPG)], axis=1)

    ot = gate(0) * oc_ref[...] + gate(1) * o_s + gate(2) * ow_ref[...]
    stacked = jnp.concatenate([ot[:, TQ * h:TQ * (h + 1)] for h in range(HPG)], axis=0)
    o_ref[0] = stacked.T.astype(o_ref.dtype)


def _nsa(pn, pt, kc, vct, bsel, bwin, bcmp, cvec, ovl):
    b, t, _ = pn.shape
    nq = t // TQ
    ncp = kc.shape[2]
    ns = t // SEL_BLOCK
    q_rows = N_HEADS * HEAD_DIM
    gate_blk0 = (q_rows + 2 * N_KV * HEAD_DIM) // 16
    vsl_blk0 = q_rows // HEAD_DIM
    vw_blk0 = vsl_blk0 + N_KV
    ksl_col, kw_col = 2, 3

    def kw_spec(d):
        return pl.BlockSpec((1, TQ, 256), lambda i, j, n: (i, jnp.maximum(n - d, 0), kw_col))

    def vw_spec(d):
        return pl.BlockSpec((1, HEAD_DIM, TQ), lambda i, j, n: (i, vw_blk0 + j, jnp.maximum(n - d, 0)))

    per_g = lambda a: pl.BlockSpec((1,) + a.shape[1:], lambda i, j, n: (j,) + (0,) * (a.ndim - 1))
    return pl.pallas_call(
        _nsa_kernel,
        out_shape=jax.ShapeDtypeStruct((b, t, q_rows), BF16),
        grid=(b, N_KV, nq),
        in_specs=[
            pl.BlockSpec((1, HPG * HEAD_DIM, TQ), lambda i, j, n: (i, j, n)),
            pl.BlockSpec((1, 16, TQ), lambda i, j, n: (i, gate_blk0 + j, n)),
            pl.BlockSpec((1, 1, ncp, HEAD_DIM), lambda i, j, n: (i, j, 0, 0)),
            pl.BlockSpec((1, 1, HEAD_DIM, ncp), lambda i, j, n: (i, j, 0, 0)),
            pl.BlockSpec((1, t, 256), lambda i, j, n: (i, 0, ksl_col)),
            pl.BlockSpec((1, HEAD_DIM, t), lambda i, j, n: (i, vsl_blk0 + j, 0)),
            kw_spec(0), kw_spec(1), kw_spec(2),
            vw_spec(0), vw_spec(1), vw_spec(2),
            per_g(bsel), per_g(bwin), per_g(bcmp), per_g(cvec),
            pl.BlockSpec(ovl.shape, lambda i, j, n: (0, 0)),
        ],
        out_specs=pl.BlockSpec((1, TQ, HPG * HEAD_DIM), lambda i, j, n: (i, n, j)),
        scratch_shapes=[
            pltpu.VMEM((N_KV * HEAD_DIM, LQ), BF16),
            pltpu.VMEM((ncp, LQ), F32),
            pltpu.VMEM((ns, TQ), F32),
            pltpu.VMEM((1, LQ), F32), pltpu.VMEM((1, LQ), F32), pltpu.VMEM((HEAD_DIM, LQ), F32),
            pltpu.VMEM((HEAD_DIM, LQ), F32), pltpu.VMEM((HEAD_DIM, LQ), F32),
            pltpu.VMEM((TK, LQ), F32), pltpu.VMEM((TK, LQ), F32),
            pltpu.VMEM((8 * TK // SEL_BLOCK, LQ), F32), pltpu.VMEM((8 * TK // SEL_BLOCK, LQ), F32),
            pltpu.VMEM((ns, TQ), F32),
        ],
        compiler_params=_cparams(("parallel", "parallel", "arbitrary")),
        name="nsa",
    )(pt, pt, kc, vct, pn, pt, pn, pn, pn, pt, pt, pt, bsel, bwin, bcmp, cvec, ovl)


def _merge_kernel(ys_ref, o_ref, u_ref, w1_ref, w2_ref, wo_ref, wga_ref, wgb_ref, z_ref):
    ys = ys_ref[...]
    ya = _dot(ys, w1_ref[...]) * _sigmoid(_dot(ys, w2_ref[...]))
    yb = _dot(o_ref[...], wo_ref[...])
    u = u_ref[...]
    z = _sigmoid(_dot(u, wga_ref[...])) * ya + _sigmoid(_dot(u, wgb_ref[...])) * yb
    z_ref[...] = z.astype(z_ref.dtype)


def _merge(ys, o, u, w1, w2, wo, wga, wgb):
    t = ys.shape[0]
    n = o.shape[0]
    d = w1.shape[1]
    tm, tn = min(MERGE_TM, t), MERGE_TN
    tpb = t // tm
    row = lambda a: pl.BlockSpec((tm, a.shape[1]), lambda i, j: (i, 0))
    col = lambda a: pl.BlockSpec((a.shape[0], tn), lambda i, j: (0, j))
    return pl.pallas_call(
        _merge_kernel,
        out_shape=jax.ShapeDtypeStruct((n, d), BF16),
        grid=(n // tm, d // tn),
        in_specs=[pl.BlockSpec((tm, w1.shape[0]), lambda i, j: (i % tpb, i // tpb)),
                  row(o), row(u), col(w1), col(w2), col(wo), col(wga), col(wgb)],
        out_specs=pl.BlockSpec((tm, tn), lambda i, j: (i, j)),
        compiler_params=_cparams(("parallel", "parallel")),
        name="merge",
    )(ys, o, u, w1, w2, wo, wga, wgb)


def _outproj_kernel(z_ref, h_ref, w_ref, g_ref, o_ref):
    mixed = _dot(z_ref[...], w_ref[...])
    o_ref[...] = h_ref[...] + _rms(mixed) * g_ref[...]


def _outproj(z, h, w, g):
    n, d = h.shape
    tm = min(OUT_TM, n)
    return pl.pallas_call(
        _outproj_kernel,
        out_shape=jax.ShapeDtypeStruct((n, d), F32),
        grid=(n // tm,),
        in_specs=[
            pl.BlockSpec((tm, d), lambda i: (i, 0)),
            pl.BlockSpec((tm, d), lambda i: (i, 0)),
            pl.BlockSpec(w.shape, lambda i: (0, 0)),
            pl.BlockSpec((1, d), lambda i: (0, 0)),
        ],
        out_specs=pl.BlockSpec((tm, d), lambda i: (i, 0)),
        compiler_params=_cparams(("parallel",)),
        name="outproj",
    )(z, h, w, g)


def _rel_bucket(dist):
    dist = jnp.maximum(dist, 0)
    max_exact = N_BUCKETS // 2
    d_f = jnp.maximum(dist, 1).astype(jnp.float32)
    large = max_exact + (jnp.log(d_f / max_exact) / math.log(MAX_DIST / max_exact)
                         * (N_BUCKETS - max_exact)).astype(jnp.int32)
    large = jnp.minimum(large, N_BUCKETS - 1)
    return jnp.where(dist < max_exact, dist, large)


def _bias_tables(rel_bias):
    table = (LOG2E * rel_bias).reshape(N_BUCKETS, N_KV, HPG)
    far = LOG2E * rel_bias[_rel_bucket(jnp.asarray(MAX_DIST))]
    cvec = jnp.broadcast_to(far.reshape(N_KV, 1, HPG, 1), (N_KV, 1, HPG, TQ)).reshape(N_KV, 1, LQ)
    tok = jnp.arange(TQ)[None, :]
    n_sel, n_win = MAX_DIST + 2 * TK, WINDOW + TQ
    d_sel = tok + MAX_DIST + TK - jnp.arange(n_sel)[:, None]
    d_win = tok + WINDOW - jnp.arange(n_win)[:, None]
    d_cmp = tok - CMP_STRIDE * (jnp.arange(CMP_BAND)[:, None] - CMP_NEAR) - (CMP_LEN - 1)
    dist = jnp.concatenate([d_sel, d_win, d_cmp], axis=0)
    ok = jnp.concatenate([d_sel >= 0, (d_win >= 0) & (d_win < WINDOW), d_cmp >= 0], axis=0)
    onehot = jax.nn.one_hot(_rel_bucket(dist), N_BUCKETS, dtype=F32)
    v = jnp.einsum('rtb,bgh->grht', onehot, table, precision=lax.Precision.HIGHEST)
    v = jnp.where(ok[None, :, None, :], v, NEG_INF).reshape(N_KV, dist.shape[0], LQ)
    bsel, bwin, bcmp = v[:, :n_sel], v[:, n_sel:n_sel + n_win], v[:, n_sel + n_win:]
    bcmp = jnp.where(bcmp > 0.5 * NEG_INF, bcmp - cvec, NEG_INF)
    return bsel, bwin, bcmp, cvec


def _overlap_t(t):
    nc = t // CMP_STRIDE
    cmp_start = np.arange(nc) * CMP_STRIDE
    sel_start = np.arange(t // SEL_BLOCK) * SEL_BLOCK
    ov = ((cmp_start[None, :] < sel_start[:, None] + SEL_BLOCK)
          & (cmp_start[None, :] + CMP_LEN > sel_start[:, None]))
    ov[:, nc - 1] = False
    out = np.zeros((t // SEL_BLOCK, CMP_FRONT + nc), np.float32)
    out[:, CMP_FRONT:] = ov
    return jnp.asarray(out, BF16)


def _s5_tables(a_re, a_im, log_dt, b_re, b_im, c_re, c_im, d_skip, batch):
    dt = jnp.exp(log_dt)[:, None]
    lam_re = jnp.minimum(a_re, -1e-4)
    lam_im = a_im
    mag = jnp.exp(lam_re * dt)
    ab_re = mag * jnp.cos(lam_im * dt)
    ab_im = mag * jnp.sin(lam_im * dt)
    den = lam_re * lam_re + lam_im * lam_im
    n_re = ab_re - 1.0
    n_im = ab_im
    co_re = (n_re * lam_re + n_im * lam_im) / den
    co_im = (n_im * lam_re - n_re * lam_im) / den
    bb_re = co_re[..., None] * b_re - co_im[..., None] * b_im
    bb_im = co_re[..., None] * b_im + co_im[..., None] * b_re
    sel = jnp.asarray(np.eye(4, dtype=np.float32)[np.arange(16) % 4])
    eye2 = jnp.eye(2, dtype=F32)
    bb = jnp.stack([bb_re, bb_im], 0).reshape(2, 2, 16, 2, SSM_STATE, SSM_GROUP)
    wbu = jnp.einsum('rhjgpc,jm,gk->jhmgcrkp', bb, sel, eye2).reshape(16, 256, 256)
    cc = jnp.stack([c_re, -c_im], 0).reshape(2, 2, 16, 2, SSM_GROUP, SSM_STATE)
    wc = jnp.einsum('rhjgcp,jm,gk->jrkphmgc', cc, sel, eye2).reshape(16, 256, 256)

    def lanes(a):
        a = a.reshape(2, 1, 16 * 2 * SSM_STATE)
        return jnp.broadcast_to(a.transpose(1, 0, 2), (batch, 2, 2048)).reshape(2 * batch, 2048)

    dmat = jnp.broadcast_to(d_skip.reshape(1, 2, 512), (batch, 2, 512)).reshape(2 * batch, 512)
    return wbu.astype(BF16), wc.astype(BF16), lanes(ab_re), lanes(ab_im), dmat


def kernel(x, ffn1_pre_g, ffn1_w_gate, ffn1_w_up, ffn1_w_down, ffn1_post_g, mix_pre_g, w_in, ssm_a_re, ssm_a_im, ssm_log_dt, ssm_b_re, ssm_b_im, ssm_c_re, ssm_c_im, ssm_d, ssm_glu_w1, ssm_glu_w2, cmp_pos, cmp_k_w1, cmp_k_w2, cmp_v_w1, cmp_v_w2, nsa_w_o, w_out, mix_post_g, ffn2_pre_g, ffn2_w_gate, ffn2_w_up, ffn2_w_down, ffn2_post_g, rel_bias):
    b, t, d = x.shape
    n = b * t
    assert b == 4 and t % TK == 0 and TK % TQ == 0 and t >= MAX_DIST + TK and d == D_MODEL
    bf = lambda a: a.astype(BF16)
    h = x.reshape(n, d)
    for l in range(ffn1_pre_g.shape[0]):
        h, u = _ffn(h, ffn1_pre_g[l][None], bf(ffn1_w_gate[l]), bf(ffn1_w_up[l]), bf(ffn1_w_down[l]),
                    ffn1_post_g[l][None], mix_pre_g[l][None], True)

        w = w_in[l]
        o_q = SSM_WIDTH
        o_kv = o_q + N_HEADS * HEAD_DIM
        kvw = N_KV * HEAD_DIM
        o_gn = o_kv + 6 * kvw
        o_ga = o_gn + 3 * N_HEADS
        kv = lambda i: w[:, o_kv + i * kvw:o_kv + (i + 1) * kvw]
        wn = bf(jnp.concatenate([w[:, :o_q], kv(0), kv(1), kv(2), kv(4)], axis=1))
        gate_cols = np.full((N_KV, 4, HPG), -1)
        for g in range(N_KV):
            for br in range(3):
                for hh in range(HPG):
                    gate_cols[g, br, hh] = o_gn + 3 * HPG * g + 3 * hh + br
        gate_cols = gate_cols.reshape(-1)
        wg = jnp.where((gate_cols >= 0)[None, :], w[:, np.maximum(gate_cols, 0)], 0.0)
        w_q = w[:, o_q:o_kv] * (HEAD_DIM ** -0.5 * LOG2E)
        wt = bf(jnp.concatenate([w_q, kv(3), kv(5), wg], axis=1).T)
        us, pn, pt = _proj(u.reshape(b, t, d), wn, wt)

        wbu, wc, a_re, a_im, dmat = _s5_tables(ssm_a_re[l], ssm_a_im[l], ssm_log_dt[l], ssm_b_re[l],
                                               ssm_b_im[l], ssm_c_re[l], ssm_c_im[l], ssm_d[l], b)
        ys = _s5(us.reshape(t * 2 * b, 512), wbu, wc, a_re, a_im, dmat).reshape(t, b * SSM_WIDTH)

        def rows16(a):
            a = a.reshape(b, t // CMP_STRIDE, CMP_STRIDE, N_KV, HEAD_DIM)
            return a.transpose(0, 3, 1, 2, 4).reshape(b, N_KV, t // CMP_STRIDE, CMP_STRIDE * HEAD_DIM)

        half = CMP_STRIDE * HEAD_DIM
        w1cat = lambda w1: bf(jnp.concatenate([w1[:half], w1[half:]], axis=1))
        pos8 = jnp.zeros((8, half), F32).at[0].set(cmp_pos[l][:CMP_STRIDE].reshape(-1))
        pos8 = bf(pos8.at[1].set(cmp_pos[l][CMP_STRIDE:].reshape(-1)))
        kc, vct = _compress(rows16(pn[:, :, :kvw]), rows16(pn[:, :, kvw:2 * kvw]), pos8,
                            w1cat(cmp_k_w1[l]), bf(cmp_k_w2[l]), w1cat(cmp_v_w1[l]), bf(cmp_v_w2[l].T))
        bsel, bwin, bcmp, cvec = _bias_tables(rel_bias)
        o_nsa = _nsa(pn, pt, kc, vct, bsel, bwin, bcmp, cvec, _overlap_t(t))

        z = _merge(ys, o_nsa.reshape(n, N_HEADS * HEAD_DIM), u, bf(ssm_glu_w1[l]), bf(ssm_glu_w2[l]),
                   bf(nsa_w_o[l]), bf(w[:, o_ga:o_ga + d]), bf(w[:, o_ga + d:o_ga + 2 * d]))
        h = _outproj(z, h, bf(w_out[l]), mix_post_g[l][None])

        h, _ = _ffn(h, ffn2_pre_g[l][None], bf(ffn2_w_gate[l]), bf(ffn2_w_up[l]), bf(ffn2_w_down[l]),
                    ffn2_post_g[l][None], ffn2_post_g[l][None], False)
    return h.reshape(b, t, d)
```

```python
import functools
import math

import numpy as np
import jax
import jax.numpy as jnp
from jax import lax
from jax.experimental import pallas as pl
from jax.experimental.pallas import tpu as pltpu

F32 = jnp.float32
BF16 = jnp.bfloat16

D_MODEL = 2048
D_FF = 5632
EPS = 1e-6
SSM_WIDTH = 1024
SSM_GROUP = 16
SSM_GROUPS = 64
SSM_STATE = 64
N_HEADS = 16
N_KV = 4
HPG = 4
HEAD_DIM = 64
CMP_LEN = 32
CMP_STRIDE = 16
CMP_HID = 256
SEL_BLOCK = 64
N_SEL = 16
WINDOW = 512
SEL_FORCE = 1e4
NEG_INF = -1e30
LOG2E = 1.4426950408889634
N_BUCKETS = 32
MAX_DIST = 1024

V7X_VMEM_BYTES = 64 * 1024 * 1024
VMEM_LIMIT = V7X_VMEM_BYTES - 8 * 1024 * 1024

FFN_TM = 512
FFN_TF = 512
PROJ_TM = 512
LANES = 128
SUBLANES = 8
MXU_DEPTH = 256
S5_LC = 64
S5_CW = 512
S5_HALF = SSM_WIDTH // 2
S5_LANES = SSM_GROUPS * SSM_STATE // 2
S5_TILES = S5_LANES // LANES
S5_SLABS = S5_HALF // LANES
TQ = 256
TK = 512
LQ = HPG * TQ
CMP_FRONT = 128
CMP_NEAR = 72
CMP_BAND = CMP_NEAR + TQ // CMP_STRIDE
N_WIN = WINDOW // TQ + 1
MERGE_TM = 1024
MERGE_TN = 512
OUT_TM = 512


def _cparams(sem):
    return pltpu.CompilerParams(dimension_semantics=sem, vmem_limit_bytes=VMEM_LIMIT)


def _rms(x):
    ms = jnp.mean(x * x, axis=-1, keepdims=True)
    return x * lax.rsqrt(ms + EPS)


def _sigmoid(x):
    return 1.0 / (1.0 + jnp.exp(-x))


def _gelu(x):
    return 0.5 * x * (1.0 + jnp.tanh(math.sqrt(2.0 / math.pi) * (x + 0.044715 * (x * x * x))))


def _dot(a, b):
    return jnp.dot(a, b, preferred_element_type=F32)


def _dot_nt(a, b):
    return lax.dot_general(a, b, (((1,), (1,)), ((), ())), preferred_element_type=F32)


def _ffn_kernel(x_ref, pre_ref, wg_ref, wu_ref, wd_ref, post_ref, nxt_ref, *rest, emit_next):
    if emit_next:
        o_ref, u_ref, xn_ref, acc_ref = rest
    else:
        o_ref, xn_ref, acc_ref = rest
    f = pl.program_id(1)

    @pl.when(f == 0)
    def _():
        xn_ref[...] = (_rms(x_ref[...]) * pre_ref[...]).astype(BF16)
        acc_ref[...] = jnp.zeros_like(acc_ref)

    xn = xn_ref[...]
    g = _dot(xn, wg_ref[...])
    u = _dot(xn, wu_ref[...])
    a = (g * _sigmoid(g) * u).astype(BF16)
    acc_ref[...] += _dot(a, wd_ref[...])

    @pl.when(f == pl.num_programs(1) - 1)
    def _():
        h = x_ref[...] + 0.5 * (_rms(acc_ref[...]) * post_ref[...])
        o_ref[...] = h
        if emit_next:
            u_ref[...] = (_rms(h) * nxt_ref[...]).astype(BF16)


def _ffn(x, pre_g, wg, wu, wd, post_g, nxt_g, emit_next):
    n, d = x.shape
    nf = wg.shape[1]
    tm, tf = min(FFN_TM, n), FFN_TF
    out_shape = [jax.ShapeDtypeStruct((n, d), F32)]
    out_specs = [pl.BlockSpec((tm, d), lambda i, f: (i, 0))]
    if emit_next:
        out_shape.append(jax.ShapeDtypeStruct((n, d), BF16))
        out_specs.append(pl.BlockSpec((tm, d), lambda i, f: (i, 0)))
    vec = pl.BlockSpec((1, d), lambda i, f: (0, 0))
    res = pl.pallas_call(
        functools.partial(_ffn_kernel, emit_next=emit_next),
        out_shape=out_shape,
        grid=(n // tm, nf // tf),
        in_specs=[
            pl.BlockSpec((tm, d), lambda i, f: (i, 0)),
            vec,
            pl.BlockSpec((d, tf), lambda i, f: (0, f)),
            pl.BlockSpec((d, tf), lambda i, f: (0, f)),
            pl.BlockSpec((tf, d), lambda i, f: (f, 0)),
            vec,
            vec,
        ],
        out_specs=out_specs,
        scratch_shapes=[pltpu.VMEM((tm, d), BF16), pltpu.VMEM((tm, d), F32)],
        compiler_params=_cparams(("parallel", "arbitrary")),
        name="ffn",
    )(x, pre_g, wg, wu, wd, post_g, nxt_g)
    return res if emit_next else (res[0], None)


def _proj_kernel(u_ref, wn_ref, wt_ref, us_ref, pn_ref, pt_ref):
    u = u_ref[0]
    nat = _dot(u, wn_ref[...])
    us_ref[...] = nat[:, :SSM_WIDTH].astype(us_ref.dtype)
    pn_ref[0] = nat[:, SSM_WIDTH:].astype(pn_ref.dtype)
    pt_ref[0] = _dot_nt(wt_ref[...], u).astype(pt_ref.dtype)


def _proj(u, wn, wt):
    b, t, d = u.shape
    tm = min(PROJ_TM, t)
    nk = wn.shape[1] - SSM_WIDTH
    return pl.pallas_call(
        _proj_kernel,
        out_shape=[jax.ShapeDtypeStruct((t, b * SSM_WIDTH), BF16),
                   jax.ShapeDtypeStruct((b, t, nk), BF16),
                   jax.ShapeDtypeStruct((b, wt.shape[0], t), BF16)],
        grid=(b, t // tm),
        in_specs=[
            pl.BlockSpec((1, tm, d), lambda i, j: (i, j, 0)),
            pl.BlockSpec(wn.shape, lambda i, j: (0, 0)),
            pl.BlockSpec(wt.shape, lambda i, j: (0, 0)),
        ],
        out_specs=[
            pl.BlockSpec((tm, SSM_WIDTH), lambda i, j: (j, i)),
            pl.BlockSpec((1, tm, nk), lambda i, j: (i, j, 0)),
            pl.BlockSpec((1, wt.shape[0], tm), lambda i, j: (i, 0, j)),
        ],
        compiler_params=_cparams(("parallel", "parallel")),
        name="proj",
    )(u, wn, wt)


def _s5_kernel(lhs_ref, wbu_ref, wc_ref, are_ref, aim_ref, dmat_ref, y_ref,
               xre_ref, xim_ref, bre_ref, bim_ref):
    lc = lhs_ref.shape[0] // SUBLANES

    @pl.when(pl.program_id(0) == 0)
    def _():
        xre_ref[...] = jnp.zeros_like(xre_ref)
        xim_ref[...] = jnp.zeros_like(xim_ref)

    row_odd = (lax.broadcasted_iota(jnp.int32, (lhs_ref.shape[0], 1), 0) % 2) == 1
    tiles_per_slab = S5_TILES // S5_SLABS
    for j in range(S5_TILES):
        jq = j // tiles_per_slab
        slab = lhs_ref[:, LANES * jq:LANES * (jq + 1)]
        zero = jnp.zeros_like(slab)
        lj = jnp.concatenate([jnp.where(row_odd, zero, slab), jnp.where(row_odd, slab, zero)], axis=1)
        out = _dot(lj, wbu_ref[j])
        bre_ref[:, LANES * j:LANES * (j + 1)] = out[:, :LANES]
        bim_ref[:, LANES * j:LANES * (j + 1)] = out[:, LANES:]

    for cc in range(bre_ref.shape[1] // S5_CW):
        cs = slice(cc * S5_CW, (cc + 1) * S5_CW)
        ar = are_ref[:, cs]
        ai = aim_ref[:, cs]

        def step(i, carry, cs=cs, ar=ar, ai=ai):
            xr, xi = carry
            rows = pl.ds(pl.multiple_of(i * SUBLANES, SUBLANES), SUBLANES)
            nr = ar * xr - ai * xi + bre_ref[rows, cs]
            ni = ar * xi + ai * xr + bim_ref[rows, cs]
            bre_ref[rows, cs] = nr
            bim_ref[rows, cs] = ni
            return nr, ni

        xr, xi = lax.fori_loop(0, lc, step, (xre_ref[:, cs], xim_ref[:, cs]))
        xre_ref[:, cs] = xr
        xim_ref[:, cs] = xi

    odd = (lax.broadcasted_iota(jnp.int32, (1, SUBLANES, 1), 1) % 2) == 1
    for jq in range(S5_SLABS):
        acc = None
        for jm in range(tiles_per_slab):
            j = tiles_per_slab * jq + jm
            xj = jnp.concatenate([bre_ref[:, LANES * j:LANES * (j + 1)],
                                  bim_ref[:, LANES * j:LANES * (j + 1)]], axis=1).astype(BF16)
            part = _dot(xj, wc_ref[j])
            acc = part if acc is None else acc + part
        acc3 = acc.reshape(lc, SUBLANES, 2 * LANES)
        y = jnp.where(odd, acc3[:, :, LANES:], acc3[:, :, :LANES])
        u_own = lhs_ref[:, LANES * jq:LANES * (jq + 1)].astype(F32).reshape(lc, SUBLANES, LANES)
        y = y + dmat_ref[:, LANES * jq:LANES * (jq + 1)][None] * u_own
        y_ref[:, LANES * jq:LANES * (jq + 1)] = _gelu(y).reshape(lc * SUBLANES, LANES).astype(y_ref.dtype)


def _s5(lhs, wbu, wc, a_re, a_im, dmat):
    rows = lhs.shape[0]
    r = SUBLANES * S5_LC
    nl = a_re.shape[1]
    return pl.pallas_call(
        _s5_kernel,
        out_shape=jax.ShapeDtypeStruct((rows, S5_HALF), BF16),
        grid=(rows // r,),
        in_specs=[
            pl.BlockSpec((r, S5_HALF), lambda c: (c, 0)),
            pl.BlockSpec(wbu.shape, lambda c: (0, 0, 0)),
            pl.BlockSpec(wc.shape, lambda c: (0, 0, 0)),
            pl.BlockSpec(a_re.shape, lambda c: (0, 0)),
            pl.BlockSpec(a_im.shape, lambda c: (0, 0)),
            pl.BlockSpec(dmat.shape, lambda c: (0, 0)),
        ],
        out_specs=pl.BlockSpec((r, S5_HALF), lambda c: (c, 0)),
        scratch_shapes=[pltpu.VMEM((SUBLANES, nl), F32), pltpu.VMEM((SUBLANES, nl), F32),
                        pltpu.VMEM((r, nl), F32), pltpu.VMEM((r, nl), F32)],
        compiler_params=_cparams(("arbitrary",)),
        name="s5",
    )(lhs, wbu, wc, a_re, a_im, dmat)


def _compress_kernel(rk_ref, rv_ref, pos_ref, w1k_ref, w2k_ref, w1v_ref, w2vt_ref, k_ref, vt_ref):
    m = rk_ref.shape[2]

    def hidden(r_ref, w1_ref):
        z = _dot(r_ref[0, 0], w1_ref[...])
        pz = _dot(pos_ref[...], w1_ref[...])
        posvec = pz[0:1, :CMP_HID] + pz[1:2, CMP_HID:]
        zhi = pltpu.roll(z[:, CMP_HID:], m - 1, 0)
        return _gelu(z[:, :CMP_HID] + zhi + posvec).astype(BF16)

    hk = hidden(rk_ref, w1k_ref)
    k_ref[0, 0, :CMP_FRONT, :] = jnp.zeros((CMP_FRONT, HEAD_DIM), k_ref.dtype)
    k_ref[0, 0, CMP_FRONT:, :] = _dot(hk, w2k_ref[...]).astype(k_ref.dtype)
    hv = hidden(rv_ref, w1v_ref)
    vt_ref[0, 0, :, :CMP_FRONT] = jnp.zeros((HEAD_DIM, CMP_FRONT), vt_ref.dtype)
    vt_ref[0, 0, :, CMP_FRONT:] = _dot_nt(w2vt_ref[...], hv).astype(vt_ref.dtype)


def _compress(rk, rv, pos8, w1k, w2k, w1v, w2vt):
    b, g, m, kd = rk.shape
    ncp = CMP_FRONT + m
    full = lambda a: pl.BlockSpec(a.shape, lambda i, j: (0,) * a.ndim)
    return pl.pallas_call(
        _compress_kernel,
        out_shape=[jax.ShapeDtypeStruct((b, g, ncp, HEAD_DIM), BF16),
                   jax.ShapeDtypeStruct((b, g, HEAD_DIM, ncp), BF16)],
        grid=(b, g),
        in_specs=[
            pl.BlockSpec((1, 1, m, kd), lambda i, j: (i, j, 0, 0)),
            pl.BlockSpec((1, 1, m, kd), lambda i, j: (i, j, 0, 0)),
            full(pos8), full(w1k), full(w2k), full(w1v), full(w2vt),
        ],
        out_specs=[
            pl.BlockSpec((1, 1, ncp, HEAD_DIM), lambda i, j: (i, j, 0, 0)),
            pl.BlockSpec((1, 1, HEAD_DIM, ncp), lambda i, j: (i, j, 0, 0)),
        ],
        compiler_params=_cparams(("parallel", "parallel")),
        name="compress",
    )(rk, rv, pos8, w1k, w2k, w1v, w2vt)


def _nsa_kernel(qt_ref, gt_ref, kc_ref, vct_ref, ksl_ref, vslt_ref,
                kw0_ref, kw1_ref, kw2_ref, vw0_ref, vw1_ref, vw2_ref,
                bsel_ref, bwin_ref, bcmp_ref, cvec_ref, ovl_ref,
                o_ref,
                qpad_ref, sc_ref, nm_ref, m_ref, l_ref, acc_ref, oc_ref, ow_ref,
                sa_ref, sb_ref, ma_ref, mb_ref, imp_ref):
    g = pl.program_id(1)
    nq = pl.program_id(2)
    t0 = nq * TQ
    ncp = kc_ref.shape[2]
    ns = nm_ref.shape[0]
    cpq = TQ // CMP_STRIDE
    bpt = TQ // SEL_BLOCK

    q4 = qt_ref[0]

    @pl.when(nq == 0)
    def _():
        qpad_ref[...] = jnp.zeros_like(qpad_ref)

    grow = pl.ds(pl.multiple_of(g * HEAD_DIM, HEAD_DIM), HEAD_DIM)
    for h in range(HPG):
        qpad_ref[grow, TQ * h:TQ * (h + 1)] = q4[HEAD_DIM * h:HEAD_DIM * (h + 1)]
    heads = [slice(TQ * h, TQ * (h + 1)) for h in range(HPG)]
    t_tok = t0 + lax.broadcasted_iota(jnp.int32, (1, TQ), 1)

    ch = CMP_FRONT
    hi_row = cpq * nq + CMP_FRONT + cpq
    n_live = (hi_row + ch - 1) // ch

    def chunk(c):
        return pl.ds(pl.multiple_of(c * ch, ch), ch)

    def cmp_scores(c, carry):
        row = c * ch + lax.broadcasted_iota(jnp.int32, (ch, 1), 0)
        for h, hs in enumerate(heads):
            s = _dot(kc_ref[0, 0, chunk(c), :], q4[HEAD_DIM * h:HEAD_DIM * (h + 1)])
            sc_ref[chunk(c), hs] = s + jnp.where(row < hi_row, cvec_ref[0, :, hs], NEG_INF)
        return carry

    band_row = cpq * nq + CMP_FRONT - CMP_NEAR

    @pl.when(band_row < ch)
    def _():
        sc_ref[0:ch, :] = jnp.full((ch, LQ), NEG_INF, F32)

    lax.fori_loop(1, n_live, cmp_scores, 0)
    band = pl.ds(pl.multiple_of(band_row, 8), CMP_BAND)
    sc_ref[band, :] = sc_ref[band, :] + bcmp_ref[0]

    def cmp_max(c, m8):
        return jnp.maximum(m8, jnp.max(sc_ref[chunk(c), :].reshape(ch // 8, 8, LQ), axis=0))

    m_cmp = jnp.max(lax.fori_loop(1, n_live, cmp_max, jnp.full((8, LQ), -3e38, F32)), axis=0, keepdims=True)

    def cmp_exp(c, l8):
        e = jnp.exp2(sc_ref[chunk(c), :] - m_cmp)
        sc_ref[chunk(c), :] = e
        return l8 + jnp.sum(e.reshape(ch // 8, 8, LQ), axis=0)

    l_cmp = jnp.sum(lax.fori_loop(1, n_live, cmp_exp, jnp.zeros((8, LQ), F32)), axis=0, keepdims=True)
    has_cmp = (t_tok >= CMP_LEN - 1).astype(F32)
    p_scale = jnp.concatenate([has_cmp] * HPG, axis=1) / l_cmp
    oc_ref[...] = jnp.zeros_like(oc_ref)
    imp_ref[...] = jnp.zeros_like(imp_ref)

    def cmp_out(c, carry):
        p = sc_ref[chunk(c), :] * p_scale
        oc_ref[...] += _dot(vct_ref[0, 0, :, chunk(c)], p.astype(BF16))
        ps = p[:, heads[0]] + p[:, heads[1]] + p[:, heads[2]] + p[:, heads[3]]
        ps_hi = ps.astype(BF16)
        ps_lo = (ps - ps_hi.astype(F32)).astype(BF16)
        ovl = ovl_ref[:, chunk(c)]
        imp_ref[...] += _dot(ovl, ps_hi) + _dot(ovl, ps_lo)
        return carry

    lax.fori_loop(1, n_live, cmp_out, 0)

    def select_blocks():
        imp = imp_ref[...]
        blk = lax.broadcasted_iota(jnp.int32, (ns, TQ), 0)
        tt = t0 + lax.broadcasted_iota(jnp.int32, (ns, TQ), 1)
        valid_b = blk * SEL_BLOCK <= tt
        cur = tt // SEL_BLOCK
        forced = valid_b & ((blk == 0) | (blk == cur) | (blk == cur - 1))
        taken = -3e38
        sc = jnp.where(forced, taken, jnp.where(valid_b, imp, -SEL_FORCE))
        blkf = blk.astype(F32)
        n_forced = 3
        for _ in range(min(N_SEL, ns) - n_forced):
            mx = jnp.max(sc, axis=0, keepdims=True)
            first = jnp.min(jnp.where(sc == mx, blkf, float(ns)), axis=0, keepdims=True)
            sc = jnp.where(blkf == first, taken, sc)
        nm_ref[...] = jnp.where(sc == taken, 0.0, NEG_INF)

    def reset():
        m_ref[...] = jnp.full_like(m_ref, NEG_INF)
        l_ref[...] = jnp.zeros_like(l_ref)
        acc_ref[...] = jnp.zeros_like(acc_ref)

    def scores(dst, keys, bias_fn):
        s_dst, bm_dst = dst
        for hs in heads:
            s = _dot(keys, qpad_ref[:, hs])
            for jj in range(keys.shape[0] // SEL_BLOCK):
                rows = slice(SEL_BLOCK * jj, SEL_BLOCK * (jj + 1))
                sb = s[rows] + bias_fn(rows, hs)
                s_dst[rows, hs] = sb
                bm_dst[8 * jj:8 * (jj + 1), hs] = jnp.max(sb.reshape(SEL_BLOCK // 8, 8, TQ), axis=0)

    def online(src, vt, masks):
        s_src, bm_src = src
        m_old = m_ref[...]
        vt1 = jnp.concatenate([vt, jnp.ones((16, vt.shape[1]), BF16)], axis=0)
        caps = [jnp.where(mk < 0.5 * NEG_INF, 0.0, 3e38).astype(BF16) for mk in masks]
        m_new, pv = [], []
        for hs in heads:
            smax = None
            for jj, mk in enumerate(masks):
                bm = bm_src[8 * jj:8 * (jj + 1), hs] + mk
                smax = bm if smax is None else jnp.maximum(smax, bm)
            mh = jnp.maximum(m_old[:, hs], jnp.max(smax, axis=0, keepdims=True))
            ps = [jnp.minimum(jnp.exp2(s_src[SEL_BLOCK * jj:SEL_BLOCK * (jj + 1), hs] - mh).astype(BF16), cap)
                  for jj, cap in enumerate(caps)]
            m_new.append(mh)
            pv.append(_dot(vt1, jnp.concatenate(ps, axis=0)))
        m_new = jnp.concatenate(m_new, axis=1)
        pv = jnp.concatenate(pv, axis=1)
        alpha = jnp.exp2(m_old - m_new)
        m_ref[...] = m_new
        l_ref[...] = alpha * l_ref[...] + pv[HEAD_DIM:HEAD_DIM + 1]
        acc_ref[...] = alpha * acc_ref[...] + pv[:HEAD_DIM]

    last = (nq * TQ) // TK

    def sel_scores(dst, kt):
        r0 = jnp.maximum(MAX_DIST + TK + TK * kt - t0, 0)
        scores(dst, ksl_ref[0, pl.ds(pl.multiple_of(kt * TK, TK), TK), :],
               lambda rows, hs: bsel_ref[0, pl.ds(pl.multiple_of(r0 + rows.start, SEL_BLOCK), SEL_BLOCK), hs])

    def sel_tile(cur, nxt, kt):
        sel_scores(nxt, jnp.minimum(kt + 1, last))
        gone = jnp.where(kt <= last, 0.0, NEG_INF)
        kt = jnp.minimum(kt, last)
        masks = [nm_ref[pl.ds(kt * (TK // SEL_BLOCK) + jj, 1), :] + gone for jj in range(TK // SEL_BLOCK)]
        online(cur, vslt_ref[0, :, pl.ds(pl.multiple_of(kt * TK, TK), TK)], masks)

    buf_a = (sa_ref, ma_ref)
    buf_b = (sb_ref, mb_ref)

    reset()
    kws = (kw0_ref, kw1_ref, kw2_ref)
    vws = (vw0_ref, vw1_ref, vw2_ref)

    def win_scores(dst, d):
        base = TQ * (N_WIN - 1 - d)
        scores(dst, kws[d][0], lambda rows, hs: bwin_ref[0, base + rows.start:base + rows.stop, hs])

    def win_online(src, d):
        gone = jnp.where(nq >= d, 0.0, NEG_INF) + jnp.zeros((1, TQ), F32)
        online(src, vws[d][0], [gone] * bpt)

    bufs = (buf_a, buf_b)
    win_scores(bufs[0], 0)
    for d in range(N_WIN):
        if d + 1 < N_WIN:
            win_scores(bufs[(d + 1) % 2], d + 1)
        win_online(bufs[d % 2], d)
    ow_ref[...] = acc_ref[...] / l_ref[...]
    reset()
    sel_scores(buf_a, 0)
    select_blocks()


    def tile_pair(i, c):
        sel_tile(buf_a, buf_b, 2 * i)
        sel_tile(buf_b, buf_a, 2 * i + 1)
        return c

    lax.fori_loop(0, (last + 2) // 2, tile_pair, 0)
    o_s = acc_ref[...] / l_ref[...]

    sg = _sigmoid(gt_ref[0].astype(F32))

    def gate(br):
        return jnp.concatenate([sg[HPG * br + h:HPG * br + h + 1] for h in range(HPG)], axis=1)

    ot = gate(0) * oc_ref[...] + gate(1) * o_s + gate(2) * ow_ref[...]
    stacked = jnp.concatenate([ot[:, TQ * h:TQ * (h + 1)] for h in range(HPG)], axis=0)
    o_ref[0] = stacked.T.astype(o_ref.dtype)


def _nsa(pn, pt, kc, vct, bsel, bwin, bcmp, cvec, ovl):
    b, t, _ = pn.shape
    nq = t // TQ
    ncp = kc.shape[2]
    ns = t // SEL_BLOCK
    q_rows = N_HEADS * HEAD_DIM
    gate_blk0 = (q_rows + 2 * N_KV * HEAD_DIM) // 16
    vsl_blk0 = q_rows // HEAD_DIM
    vw_blk0 = vsl_blk0 + N_KV
    ksl_col, kw_col = 2, 3

    def kw_spec(d):
        return pl.BlockSpec((1, TQ, 256), lambda i, j, n: (i, jnp.maximum(n - d, 0), kw_col))

    def vw_spec(d):
        return pl.BlockSpec((1, HEAD_DIM, TQ), lambda i, j, n: (i, vw_blk0 + j, jnp.maximum(n - d, 0)))

    per_g = lambda a: pl.BlockSpec((1,) + a.shape[1:], lambda i, j, n: (j,) + (0,) * (a.ndim - 1))
    return pl.pallas_call(
        _nsa_kernel,
        out_shape=jax.ShapeDtypeStruct((b, t, q_rows), BF16),
        grid=(b, N_KV, nq),
        in_specs=[
            pl.BlockSpec((1, HPG * HEAD_DIM, TQ), lambda i, j, n: (i, j, n)),
            pl.BlockSpec((1, 16, TQ), lambda i, j, n: (i, gate_blk0 + j, n)),
            pl.BlockSpec((1, 1, ncp, HEAD_DIM), lambda i, j, n: (i, j, 0, 0)),
            pl.BlockSpec((1, 1, HEAD_DIM, ncp), lambda i, j, n: (i, j, 0, 0)),
            pl.BlockSpec((1, t, 256), lambda i, j, n: (i, 0, ksl_col)),
            pl.BlockSpec((1, HEAD_DIM, t), lambda i, j, n: (i, vsl_blk0 + j, 0)),
            kw_spec(0), kw_spec(1), kw_spec(2),
            vw_spec(0), vw_spec(1), vw_spec(2),
            per_g(bsel), per_g(bwin), per_g(bcmp), per_g(cvec),
            pl.BlockSpec(ovl.shape, lambda i, j, n: (0, 0)),
        ],
        out_specs=pl.BlockSpec((1, TQ, HPG * HEAD_DIM), lambda i, j, n: (i, n, j)),
        scratch_shapes=[
            pltpu.VMEM((N_KV * HEAD_DIM, LQ), BF16),
            pltpu.VMEM((ncp, LQ), F32),
            pltpu.VMEM((ns, TQ), F32),
            pltpu.VMEM((1, LQ), F32), pltpu.VMEM((1, LQ), F32), pltpu.VMEM((HEAD_DIM, LQ), F32),
            pltpu.VMEM((HEAD_DIM, LQ), F32), pltpu.VMEM((HEAD_DIM, LQ), F32),
            pltpu.VMEM((TK, LQ), F32), pltpu.VMEM((TK, LQ), F32),
            pltpu.VMEM((8 * TK // SEL_BLOCK, LQ), F32), pltpu.VMEM((8 * TK // SEL_BLOCK, LQ), F32),
            pltpu.VMEM((ns, TQ), F32),
        ],
        compiler_params=_cparams(("parallel", "parallel", "arbitrary")),
        name="nsa",
    )(pt, pt, kc, vct, pn, pt, pn, pn, pn, pt, pt, pt, bsel, bwin, bcmp, cvec, ovl)


def _merge_kernel(ys_ref, o_ref, u_ref, w1_ref, w2_ref, wo_ref, wga_ref, wgb_ref, z_ref):
    ys = ys_ref[...]
    ya = _dot(ys, w1_ref[...]) * _sigmoid(_dot(ys, w2_ref[...]))
    yb = _dot(o_ref[...], wo_ref[...])
    u = u_ref[...]
    z = _sigmoid(_dot(u, wga_ref[...])) * ya + _sigmoid(_dot(u, wgb_ref[...])) * yb
    z_ref[...] = z.astype(z_ref.dtype)


def _merge(ys, o, u, w1, w2, wo, wga, wgb):
    t = ys.shape[0]
    n = o.shape[0]
    d = w1.shape[1]
    tm, tn = min(MERGE_TM, t), MERGE_TN
    tpb = t // tm
    row = lambda a: pl.BlockSpec((tm, a.shape[1]), lambda i, j: (i, 0))
    col = lambda a: pl.BlockSpec((a.shape[0], tn), lambda i, j: (0, j))
    return pl.pallas_call(
        _merge_kernel,
        out_shape=jax.ShapeDtypeStruct((n, d), BF16),
        grid=(n // tm, d // tn),
        in_specs=[pl.BlockSpec((tm, w1.shape[0]), lambda i, j: (i % tpb, i // tpb)),
                  row(o), row(u), col(w1), col(w2), col(wo), col(wga), col(wgb)],
        out_specs=pl.BlockSpec((tm, tn), lambda i, j: (i, j)),
        compiler_params=_cparams(("parallel", "parallel")),
        name="merge",
    )(ys, o, u, w1, w2, wo, wga, wgb)


def _outproj_kernel(z_ref, h_ref, w_ref, g_ref, o_ref):
    mixed = _dot(z_ref[...], w_ref[...])
    o_ref[...] = h_ref[...] + _rms(mixed) * g_ref[...]


def _outproj(z, h, w, g):
    n, d = h.shape
    tm = min(OUT_TM, n)
    return pl.pallas_call(
        _outproj_kernel,
        out_shape=jax.ShapeDtypeStruct((n, d), F32),
        grid=(n // tm,),
        in_specs=[
            pl.BlockSpec((tm, d), lambda i: (i, 0)),
            pl.BlockSpec((tm, d), lambda i: (i, 0)),
            pl.BlockSpec(w.shape, lambda i: (0, 0)),
            pl.BlockSpec((1, d), lambda i: (0, 0)),
        ],
        out_specs=pl.BlockSpec((tm, d), lambda i: (i, 0)),
        compiler_params=_cparams(("parallel",)),
        name="outproj",
    )(z, h, w, g)


def _rel_bucket(dist):
    dist = jnp.maximum(dist, 0)
    max_exact = N_BUCKETS // 2
    d_f = jnp.maximum(dist, 1).astype(jnp.float32)
    large = max_exact + (jnp.log(d_f / max_exact) / math.log(MAX_DIST / max_exact)
                         * (N_BUCKETS - max_exact)).astype(jnp.int32)
    large = jnp.minimum(large, N_BUCKETS - 1)
    return jnp.where(dist < max_exact, dist, large)


def _bias_tables(rel_bias):
    table = (LOG2E * rel_bias).reshape(N_BUCKETS, N_KV, HPG)
    far = LOG2E * rel_bias[_rel_bucket(jnp.asarray(MAX_DIST))]
    cvec = jnp.broadcast_to(far.reshape(N_KV, 1, HPG, 1), (N_KV, 1, HPG, TQ)).reshape(N_KV, 1, LQ)
    tok = jnp.arange(TQ)[None, :]
    n_sel, n_win = MAX_DIST + 2 * TK, WINDOW + TQ
    d_sel = tok + MAX_DIST + TK - jnp.arange(n_sel)[:, None]
    d_win = tok + WINDOW - jnp.arange(n_win)[:, None]
    d_cmp = tok - CMP_STRIDE * (jnp.arange(CMP_BAND)[:, None] - CMP_NEAR) - (CMP_LEN - 1)
    dist = jnp.concatenate([d_sel, d_win, d_cmp], axis=0)
    ok = jnp.concatenate([d_sel >= 0, (d_win >= 0) & (d_win < WINDOW), d_cmp >= 0], axis=0)
    onehot = jax.nn.one_hot(_rel_bucket(dist), N_BUCKETS, dtype=F32)
    v = jnp.einsum('rtb,bgh->grht', onehot, table, precision=lax.Precision.HIGHEST)
    v = jnp.where(ok[None, :, None, :], v, NEG_INF).reshape(N_KV, dist.shape[0], LQ)
    bsel, bwin, bcmp = v[:, :n_sel], v[:, n_sel:n_sel + n_win], v[:, n_sel + n_win:]
    bcmp = jnp.where(bcmp > 0.5 * NEG_INF, bcmp - cvec, NEG_INF)
    return bsel, bwin, bcmp, cvec


def _overlap_t(t):
    nc = t // CMP_STRIDE
    cmp_start = np.arange(nc) * CMP_STRIDE
    sel_start = np.arange(t // SEL_BLOCK) * SEL_BLOCK
    ov = ((cmp_start[None, :] < sel_start[:, None] + SEL_BLOCK)
          & (cmp_start[None, :] + CMP_LEN > sel_start[:, None]))
    ov[:, nc - 1] = False
    out = np.zeros((t // SEL_BLOCK, CMP_FRONT + nc), np.float32)
    out[:, CMP_FRONT:] = ov
    return jnp.asarray(out, BF16)


def _s5_tables(a_re, a_im, log_dt, b_re, b_im, c_re, c_im, d_skip, batch):
    dt = jnp.exp(log_dt)[:, None]
    lam_re = jnp.minimum(a_re, -1e-4)
    lam_im = a_im
    mag = jnp.exp(lam_re * dt)
    ab_re = mag * jnp.cos(lam_im * dt)
    ab_im = mag * jnp.sin(lam_im * dt)
    den = lam_re * lam_re + lam_im * lam_im
    n_re = ab_re - 1.0
    n_im = ab_im
    co_re = (n_re * lam_re + n_im * lam_im) / den
    co_im = (n_im * lam_re - n_re * lam_im) / den
    bb_re = co_re[..., None] * b_re - co_im[..., None] * b_im
    bb_im = co_re[..., None] * b_im + co_im[..., None] * b_re
    per_slab = S5_TILES // S5_SLABS
    sel = jnp.asarray(np.eye(per_slab, dtype=np.float32)[np.arange(S5_TILES) % per_slab])
    eye2 = jnp.eye(2, dtype=F32)
    bb = jnp.stack([bb_re, bb_im], 0).reshape(2, 2, S5_TILES, 2, SSM_STATE, SSM_GROUP)
    wbu = jnp.einsum('rhjgpc,jm,gk->jhmgcrkp', bb, sel, eye2).reshape(S5_TILES, MXU_DEPTH, 2 * LANES)
    cc = jnp.stack([c_re, -c_im], 0).reshape(2, 2, S5_TILES, 2, SSM_GROUP, SSM_STATE)
    wc = jnp.einsum('rhjgcp,jm,gk->jrkphmgc', cc, sel, eye2).reshape(S5_TILES, MXU_DEPTH, 2 * LANES)

    def lanes(a):
        a = a.reshape(2, 1, S5_LANES)
        return jnp.broadcast_to(a.transpose(1, 0, 2), (batch, 2, S5_LANES)).reshape(2 * batch, S5_LANES)

    dmat = jnp.broadcast_to(d_skip.reshape(1, 2, S5_HALF), (batch, 2, S5_HALF)).reshape(2 * batch, S5_HALF)
    return wbu.astype(BF16), wc.astype(BF16), lanes(ab_re), lanes(ab_im), dmat


def kernel(x, ffn1_pre_g, ffn1_w_gate, ffn1_w_up, ffn1_w_down, ffn1_post_g, mix_pre_g, w_in, ssm_a_re, ssm_a_im, ssm_log_dt, ssm_b_re, ssm_b_im, ssm_c_re, ssm_c_im, ssm_d, ssm_glu_w1, ssm_glu_w2, cmp_pos, cmp_k_w1, cmp_k_w2, cmp_v_w1, cmp_v_w2, nsa_w_o, w_out, mix_post_g, ffn2_pre_g, ffn2_w_gate, ffn2_w_up, ffn2_w_down, ffn2_post_g, rel_bias):
    b, t, d = x.shape
    n = b * t
    assert 2 * b == SUBLANES and t % TK == 0 and TK % TQ == 0 and t >= MAX_DIST + TK and d == D_MODEL
    bf = lambda a: a.astype(BF16)
    h = x.reshape(n, d)
    for l in range(ffn1_pre_g.shape[0]):
        h, u = _ffn(h, ffn1_pre_g[l][None], bf(ffn1_w_gate[l]), bf(ffn1_w_up[l]), bf(ffn1_w_down[l]),
                    ffn1_post_g[l][None], mix_pre_g[l][None], True)

        w = w_in[l]
        o_q = SSM_WIDTH
        o_kv = o_q + N_HEADS * HEAD_DIM
        kvw = N_KV * HEAD_DIM
        o_gn = o_kv + 6 * kvw
        o_ga = o_gn + 3 * N_HEADS
        kv = lambda i: w[:, o_kv + i * kvw:o_kv + (i + 1) * kvw]
        wn = bf(jnp.concatenate([w[:, :o_q], kv(0), kv(1), kv(2), kv(4)], axis=1))
        gate_cols = np.full((N_KV, 4, HPG), -1)
        for g in range(N_KV):
            for br in range(3):
                for hh in range(HPG):
                    gate_cols[g, br, hh] = o_gn + 3 * HPG * g + 3 * hh + br
        gate_cols = gate_cols.reshape(-1)
        wg = jnp.where((gate_cols >= 0)[None, :], w[:, np.maximum(gate_cols, 0)], 0.0)
        w_q = w[:, o_q:o_kv] * (HEAD_DIM ** -0.5 * LOG2E)
        wt = bf(jnp.concatenate([w_q, kv(3), kv(5), wg], axis=1).T)
        us, pn, pt = _proj(u.reshape(b, t, d), wn, wt)

        wbu, wc, a_re, a_im, dmat = _s5_tables(ssm_a_re[l], ssm_a_im[l], ssm_log_dt[l], ssm_b_re[l],
                                               ssm_b_im[l], ssm_c_re[l], ssm_c_im[l], ssm_d[l], b)
        ys = _s5(us.reshape(t * 2 * b, S5_HALF), wbu, wc, a_re, a_im, dmat).reshape(t, b * SSM_WIDTH)

        def rows16(a):
            a = a.reshape(b, t // CMP_STRIDE, CMP_STRIDE, N_KV, HEAD_DIM)
            return a.transpose(0, 3, 1, 2, 4).reshape(b, N_KV, t // CMP_STRIDE, CMP_STRIDE * HEAD_DIM)

        half = CMP_STRIDE * HEAD_DIM
        w1cat = lambda w1: bf(jnp.concatenate([w1[:half], w1[half:]], axis=1))
        pos8 = jnp.zeros((8, half), F32).at[0].set(cmp_pos[l][:CMP_STRIDE].reshape(-1))
        pos8 = bf(pos8.at[1].set(cmp_pos[l][CMP_STRIDE:].reshape(-1)))
        kc, vct = _compress(rows16(pn[:, :, :kvw]), rows16(pn[:, :, kvw:2 * kvw]), pos8,
                            w1cat(cmp_k_w1[l]), bf(cmp_k_w2[l]), w1cat(cmp_v_w1[l]), bf(cmp_v_w2[l].T))
        bsel, bwin, bcmp, cvec = _bias_tables(rel_bias)
        o_nsa = _nsa(pn, pt, kc, vct, bsel, bwin, bcmp, cvec, _overlap_t(t))

        z = _merge(ys, o_nsa.reshape(n, N_HEADS * HEAD_DIM), u, bf(ssm_glu_w1[l]), bf(ssm_glu_w2[l]),
                   bf(nsa_w_o[l]), bf(w[:, o_ga:o_ga + d]), bf(w[:, o_ga + d:o_ga + 2 * d]))
        h = _outproj(z, h, bf(w_out[l]), mix_post_g[l][None])

        h, _ = _ffn(h, ffn2_pre_g[l][None], bf(ffn2_w_gate[l]), bf(ffn2_w_up[l]), bf(ffn2_w_down[l]),
                    ffn2_post_g[l][None], ffn2_post_g[l][None], False)
    return h.reshape(b, t, d)
```

```python
import functools
import math

import numpy as np
import jax
import jax.numpy as jnp
from jax import lax
from jax.experimental import pallas as pl
from jax.experimental.pallas import tpu as pltpu

F32 = jnp.float32
BF16 = jnp.bfloat16

D_MODEL = 2048
D_FF = 5632
EPS = 1e-6
SSM_WIDTH = 1024
SSM_GROUP = 16
SSM_GROUPS = 64
SSM_STATE = 64
N_HEADS = 16
N_KV = 4
HPG = 4
HEAD_DIM = 64
CMP_LEN = 32
CMP_STRIDE = 16
CMP_HID = 256
SEL_BLOCK = 64
N_SEL = 16
WINDOW = 512
SEL_FORCE = 1e4
NEG_INF = -1e30
LOG2E = 1.4426950408889634
N_BUCKETS = 32
MAX_DIST = 1024

V7X_VMEM_BYTES = 64 * 1024 * 1024
VMEM_LIMIT = V7X_VMEM_BYTES - 8 * 1024 * 1024

FFN_TM = 512
FFN_TF = 512
PROJ_TM = 512
LANES = 128
SUBLANES = 8
MXU_DEPTH = 256
S5_LC = 64
S5_CW = 512
S5_HALF = SSM_WIDTH // 2
S5_LANES = SSM_GROUPS * SSM_STATE // 2
S5_TILES = S5_LANES // LANES
S5_SLABS = S5_HALF // LANES
TQ = 256
TK = 512
LQ = HPG * TQ
CMP_FRONT = 128
CMP_NEAR = 72
CMP_BAND = CMP_NEAR + TQ // CMP_STRIDE
N_WIN = WINDOW // TQ + 1
MERGE_TM = 512
MERGE_TN = 512


def _cparams(sem):
    return pltpu.CompilerParams(dimension_semantics=sem, vmem_limit_bytes=VMEM_LIMIT)


def _rms(x):
    ms = jnp.mean(x * x, axis=-1, keepdims=True)
    return x * lax.rsqrt(ms + EPS)


def _sigmoid(x):
    return 1.0 / (1.0 + jnp.exp(-x))


def _gelu(x):
    return 0.5 * x * (1.0 + jnp.tanh(math.sqrt(2.0 / math.pi) * (x + 0.044715 * (x * x * x))))


def _dot(a, b):
    return jnp.dot(a, b, preferred_element_type=F32)


def _dot_nt(a, b):
    return lax.dot_general(a, b, (((1,), (1,)), ((), ())), preferred_element_type=F32)


def _ffn_kernel(x_ref, pre_ref, wg_ref, wu_ref, wd_ref, post_ref, nxt_ref, *rest, emit_next):
    if emit_next:
        o_ref, u_ref, xn_ref, acc_ref = rest
    else:
        o_ref, xn_ref, acc_ref = rest
    f = pl.program_id(1)

    @pl.when(f == 0)
    def _():
        xn_ref[...] = (_rms(x_ref[...]) * pre_ref[...]).astype(BF16)
        acc_ref[...] = jnp.zeros_like(acc_ref)

    xn = xn_ref[...]
    g = _dot(xn, wg_ref[...])
    u = _dot(xn, wu_ref[...])
    a = (g * _sigmoid(g) * u).astype(BF16)
    acc_ref[...] += _dot(a, wd_ref[...])

    @pl.when(f == pl.num_programs(1) - 1)
    def _():
        h = x_ref[...] + 0.5 * (_rms(acc_ref[...]) * post_ref[...])
        o_ref[...] = h
        if emit_next:
            u_ref[...] = (_rms(h) * nxt_ref[...]).astype(BF16)


def _ffn(x, pre_g, wg, wu, wd, post_g, nxt_g, emit_next):
    n, d = x.shape
    nf = wg.shape[1]
    tm, tf = min(FFN_TM, n), FFN_TF
    out_shape = [jax.ShapeDtypeStruct((n, d), F32)]
    out_specs = [pl.BlockSpec((tm, d), lambda i, f: (i, 0))]
    if emit_next:
        out_shape.append(jax.ShapeDtypeStruct((n, d), BF16))
        out_specs.append(pl.BlockSpec((tm, d), lambda i, f: (i, 0)))
    vec = pl.BlockSpec((1, d), lambda i, f: (0, 0))
    res = pl.pallas_call(
        functools.partial(_ffn_kernel, emit_next=emit_next),
        out_shape=out_shape,
        grid=(n // tm, nf // tf),
        in_specs=[
            pl.BlockSpec((tm, d), lambda i, f: (i, 0)),
            vec,
            pl.BlockSpec((d, tf), lambda i, f: (0, f)),
            pl.BlockSpec((d, tf), lambda i, f: (0, f)),
            pl.BlockSpec((tf, d), lambda i, f: (f, 0)),
            vec,
            vec,
        ],
        out_specs=out_specs,
        scratch_shapes=[pltpu.VMEM((tm, d), BF16), pltpu.VMEM((tm, d), F32)],
        compiler_params=_cparams(("parallel", "arbitrary")),
        name="ffn",
    )(x, pre_g, wg, wu, wd, post_g, nxt_g)
    return res if emit_next else (res[0], None)


def _proj_kernel(u_ref, wn_ref, wt_ref, us_ref, pn_ref, pt_ref):
    u = u_ref[0]
    nat = _dot(u, wn_ref[...])
    us_ref[...] = nat[:, :SSM_WIDTH].astype(us_ref.dtype)
    pn_ref[0] = nat[:, SSM_WIDTH:].astype(pn_ref.dtype)
    pt_ref[0] = _dot_nt(wt_ref[...], u).astype(pt_ref.dtype)


def _proj(u, wn, wt):
    b, t, d = u.shape
    tm = min(PROJ_TM, t)
    nk = wn.shape[1] - SSM_WIDTH
    return pl.pallas_call(
        _proj_kernel,
        out_shape=[jax.ShapeDtypeStruct((t, b * SSM_WIDTH), BF16),
                   jax.ShapeDtypeStruct((b, t, nk), BF16),
                   jax.ShapeDtypeStruct((b, wt.shape[0], t), BF16)],
        grid=(b, t // tm),
        in_specs=[
            pl.BlockSpec((1, tm, d), lambda i, j: (i, j, 0)),
            pl.BlockSpec(wn.shape, lambda i, j: (0, 0)),
            pl.BlockSpec(wt.shape, lambda i, j: (0, 0)),
        ],
        out_specs=[
            pl.BlockSpec((tm, SSM_WIDTH), lambda i, j: (j, i)),
            pl.BlockSpec((1, tm, nk), lambda i, j: (i, j, 0)),
            pl.BlockSpec((1, wt.shape[0], tm), lambda i, j: (i, 0, j)),
        ],
        compiler_params=_cparams(("parallel", "parallel")),
        name="proj",
    )(u, wn, wt)


def _s5_kernel(lhs_ref, wbu_ref, wc_ref, are_ref, aim_ref, dmat_ref, y_ref,
               xre_ref, xim_ref, bre_ref, bim_ref):
    lc = lhs_ref.shape[0] // SUBLANES

    @pl.when(pl.program_id(0) == 0)
    def _():
        xre_ref[...] = jnp.zeros_like(xre_ref)
        xim_ref[...] = jnp.zeros_like(xim_ref)

    row_odd = (lax.broadcasted_iota(jnp.int32, (lhs_ref.shape[0], 1), 0) % 2) == 1
    tiles_per_slab = S5_TILES // S5_SLABS
    for j in range(S5_TILES):
        jq = j // tiles_per_slab
        slab = lhs_ref[:, LANES * jq:LANES * (jq + 1)]
        zero = jnp.zeros_like(slab)
        lj = jnp.concatenate([jnp.where(row_odd, zero, slab), jnp.where(row_odd, slab, zero)], axis=1)
        out = _dot(lj, wbu_ref[j])
        bre_ref[:, LANES * j:LANES * (j + 1)] = out[:, :LANES]
        bim_ref[:, LANES * j:LANES * (j + 1)] = out[:, LANES:]

    for cc in range(bre_ref.shape[1] // S5_CW):
        cs = slice(cc * S5_CW, (cc + 1) * S5_CW)
        ar = are_ref[:, cs]
        ai = aim_ref[:, cs]

        def step(i, carry, cs=cs, ar=ar, ai=ai):
            xr, xi = carry
            rows = pl.ds(pl.multiple_of(i * SUBLANES, SUBLANES), SUBLANES)
            nr = ar * xr - ai * xi + bre_ref[rows, cs]
            ni = ar * xi + ai * xr + bim_ref[rows, cs]
            bre_ref[rows, cs] = nr
            bim_ref[rows, cs] = ni
            return nr, ni

        xr, xi = lax.fori_loop(0, lc, step, (xre_ref[:, cs], xim_ref[:, cs]))
        xre_ref[:, cs] = xr
        xim_ref[:, cs] = xi

    odd = (lax.broadcasted_iota(jnp.int32, (1, SUBLANES, 1), 1) % 2) == 1
    for jq in range(S5_SLABS):
        acc = None
        for jm in range(tiles_per_slab):
            j = tiles_per_slab * jq + jm
            xj = jnp.concatenate([bre_ref[:, LANES * j:LANES * (j + 1)],
                                  bim_ref[:, LANES * j:LANES * (j + 1)]], axis=1).astype(BF16)
            part = _dot(xj, wc_ref[j])
            acc = part if acc is None else acc + part
        acc3 = acc.reshape(lc, SUBLANES, 2 * LANES)
        y = jnp.where(odd, acc3[:, :, LANES:], acc3[:, :, :LANES])
        u_own = lhs_ref[:, LANES * jq:LANES * (jq + 1)].astype(F32).reshape(lc, SUBLANES, LANES)
        y = y + dmat_ref[:, LANES * jq:LANES * (jq + 1)][None] * u_own
        y_ref[:, LANES * jq:LANES * (jq + 1)] = _gelu(y).reshape(lc * SUBLANES, LANES).astype(y_ref.dtype)


def _s5(lhs, wbu, wc, a_re, a_im, dmat):
    rows = lhs.shape[0]
    r = SUBLANES * S5_LC
    nl = a_re.shape[1]
    return pl.pallas_call(
        _s5_kernel,
        out_shape=jax.ShapeDtypeStruct((rows, S5_HALF), BF16),
        grid=(rows // r,),
        in_specs=[
            pl.BlockSpec((r, S5_HALF), lambda c: (c, 0)),
            pl.BlockSpec(wbu.shape, lambda c: (0, 0, 0)),
            pl.BlockSpec(wc.shape, lambda c: (0, 0, 0)),
            pl.BlockSpec(a_re.shape, lambda c: (0, 0)),
            pl.BlockSpec(a_im.shape, lambda c: (0, 0)),
            pl.BlockSpec(dmat.shape, lambda c: (0, 0)),
        ],
        out_specs=pl.BlockSpec((r, S5_HALF), lambda c: (c, 0)),
        scratch_shapes=[pltpu.VMEM((SUBLANES, nl), F32), pltpu.VMEM((SUBLANES, nl), F32),
                        pltpu.VMEM((r, nl), F32), pltpu.VMEM((r, nl), F32)],
        compiler_params=_cparams(("arbitrary",)),
        name="s5",
    )(lhs, wbu, wc, a_re, a_im, dmat)


def _compress_kernel(rk_ref, rv_ref, pos_ref, w1k_ref, w2k_ref, w1v_ref, w2vt_ref, k_ref, vt_ref):
    m = rk_ref.shape[2]

    def hidden(r_ref, w1_ref):
        z = _dot(r_ref[0, 0], w1_ref[...])
        pz = _dot(pos_ref[...], w1_ref[...])
        posvec = pz[0:1, :CMP_HID] + pz[1:2, CMP_HID:]
        zhi = pltpu.roll(z[:, CMP_HID:], m - 1, 0)
        return _gelu(z[:, :CMP_HID] + zhi + posvec).astype(BF16)

    hk = hidden(rk_ref, w1k_ref)
    k_ref[0, 0, :CMP_FRONT, :] = jnp.zeros((CMP_FRONT, HEAD_DIM), k_ref.dtype)
    k_ref[0, 0, CMP_FRONT:, :] = _dot(hk, w2k_ref[...]).astype(k_ref.dtype)
    hv = hidden(rv_ref, w1v_ref)
    vt_ref[0, 0, :, :CMP_FRONT] = jnp.zeros((HEAD_DIM, CMP_FRONT), vt_ref.dtype)
    vt_ref[0, 0, :, CMP_FRONT:] = _dot_nt(w2vt_ref[...], hv).astype(vt_ref.dtype)


def _compress(rk, rv, pos8, w1k, w2k, w1v, w2vt):
    b, g, m, kd = rk.shape
    ncp = CMP_FRONT + m
    full = lambda a: pl.BlockSpec(a.shape, lambda i, j: (0,) * a.ndim)
    return pl.pallas_call(
        _compress_kernel,
        out_shape=[jax.ShapeDtypeStruct((b, g, ncp, HEAD_DIM), BF16),
                   jax.ShapeDtypeStruct((b, g, HEAD_DIM, ncp), BF16)],
        grid=(b, g),
        in_specs=[
            pl.BlockSpec((1, 1, m, kd), lambda i, j: (i, j, 0, 0)),
            pl.BlockSpec((1, 1, m, kd), lambda i, j: (i, j, 0, 0)),
            full(pos8), full(w1k), full(w2k), full(w1v), full(w2vt),
        ],
        out_specs=[
            pl.BlockSpec((1, 1, ncp, HEAD_DIM), lambda i, j: (i, j, 0, 0)),
            pl.BlockSpec((1, 1, HEAD_DIM, ncp), lambda i, j: (i, j, 0, 0)),
        ],
        compiler_params=_cparams(("parallel", "parallel")),
        name="compress",
    )(rk, rv, pos8, w1k, w2k, w1v, w2vt)


def _nsa_kernel(qt_ref, gt_ref, kc_ref, vct_ref, ksl_ref, vslt_ref,
                kw0_ref, kw1_ref, kw2_ref, vw0_ref, vw1_ref, vw2_ref,
                bsel_ref, bwin_ref, bcmp_ref, cvec_ref, ovl_ref,
                o_ref,
                qpad_ref, sc_ref, nm_ref, m_ref, l_ref, acc_ref, oc_ref, ow_ref,
                sa_ref, sb_ref, ma_ref, mb_ref, imp_ref):
    g = pl.program_id(1)
    nq = pl.program_id(2)
    t0 = nq * TQ
    ncp = kc_ref.shape[2]
    ns = nm_ref.shape[0]
    cpq = TQ // CMP_STRIDE
    bpt = TQ // SEL_BLOCK

    q4 = qt_ref[0]

    @pl.when(nq == 0)
    def _():
        qpad_ref[...] = jnp.zeros_like(qpad_ref)

    grow = pl.ds(pl.multiple_of(g * HEAD_DIM, HEAD_DIM), HEAD_DIM)
    for h in range(HPG):
        qpad_ref[grow, TQ * h:TQ * (h + 1)] = q4[HEAD_DIM * h:HEAD_DIM * (h + 1)]
    heads = [slice(TQ * h, TQ * (h + 1)) for h in range(HPG)]
    t_tok = t0 + lax.broadcasted_iota(jnp.int32, (1, TQ), 1)

    ch = CMP_FRONT
    hi_row = cpq * nq + CMP_FRONT + cpq
    n_live = (hi_row + ch - 1) // ch

    def chunk(c):
        return pl.ds(pl.multiple_of(c * ch, ch), ch)

    def cmp_scores(c, carry):
        row = c * ch + lax.broadcasted_iota(jnp.int32, (ch, 1), 0)
        for h, hs in enumerate(heads):
            s = _dot(kc_ref[0, 0, chunk(c), :], q4[HEAD_DIM * h:HEAD_DIM * (h + 1)])
            sc_ref[chunk(c), hs] = s + jnp.where(row < hi_row, cvec_ref[0, :, hs], NEG_INF)
        return carry

    band_row = cpq * nq + CMP_FRONT - CMP_NEAR

    @pl.when(band_row < ch)
    def _():
        sc_ref[0:ch, :] = jnp.full((ch, LQ), NEG_INF, F32)

    lax.fori_loop(1, n_live, cmp_scores, 0)
    band = pl.ds(pl.multiple_of(band_row, 8), CMP_BAND)
    sc_ref[band, :] = sc_ref[band, :] + bcmp_ref[0]

    def cmp_max(c, m8):
        return jnp.maximum(m8, jnp.max(sc_ref[chunk(c), :].reshape(ch // 8, 8, LQ), axis=0))

    m_cmp = jnp.max(lax.fori_loop(1, n_live, cmp_max, jnp.full((8, LQ), -3e38, F32)), axis=0, keepdims=True)

    def cmp_exp(c, l8):
        e = jnp.exp2(sc_ref[chunk(c), :] - m_cmp)
        sc_ref[chunk(c), :] = e
        return l8 + jnp.sum(e.reshape(ch // 8, 8, LQ), axis=0)

    l_cmp = jnp.sum(lax.fori_loop(1, n_live, cmp_exp, jnp.zeros((8, LQ), F32)), axis=0, keepdims=True)
    has_cmp = (t_tok >= CMP_LEN - 1).astype(F32)
    p_scale = jnp.concatenate([has_cmp] * HPG, axis=1) / l_cmp
    oc_ref[...] = jnp.zeros_like(oc_ref)
    imp_ref[...] = jnp.zeros_like(imp_ref)

    def cmp_out(c, carry):
        p = sc_ref[chunk(c), :] * p_scale
        oc_ref[...] += _dot(vct_ref[0, 0, :, chunk(c)], p.astype(BF16))
        ps = p[:, heads[0]] + p[:, heads[1]] + p[:, heads[2]] + p[:, heads[3]]
        ps_hi = ps.astype(BF16)
        ps_lo = (ps - ps_hi.astype(F32)).astype(BF16)
        ovl = ovl_ref[:, chunk(c)]
        imp_ref[...] += _dot(ovl, ps_hi) + _dot(ovl, ps_lo)
        return carry

    lax.fori_loop(1, n_live, cmp_out, 0)

    def select_blocks():
        imp = imp_ref[...]
        blk = lax.broadcasted_iota(jnp.int32, (ns, TQ), 0)
        tt = t0 + lax.broadcasted_iota(jnp.int32, (ns, TQ), 1)
        valid_b = blk * SEL_BLOCK <= tt
        cur = tt // SEL_BLOCK
        forced = valid_b & ((blk == 0) | (blk == cur) | (blk == cur - 1))
        taken = -3e38
        sc = jnp.where(forced, taken, jnp.where(valid_b, imp, -SEL_FORCE))
        blkf = blk.astype(F32)
        n_forced = 3
        for _ in range(min(N_SEL, ns) - n_forced):
            mx = jnp.max(sc, axis=0, keepdims=True)
            first = jnp.min(jnp.where(sc == mx, blkf, float(ns)), axis=0, keepdims=True)
            sc = jnp.where(blkf == first, taken, sc)
        nm_ref[...] = jnp.where(sc == taken, 0.0, NEG_INF)

    def reset():
        m_ref[...] = jnp.full_like(m_ref, NEG_INF)
        l_ref[...] = jnp.zeros_like(l_ref)
        acc_ref[...] = jnp.zeros_like(acc_ref)

    def scores(dst, keys, bias_fn):
        s_dst, bm_dst = dst
        for hs in heads:
            s = _dot(keys, qpad_ref[:, hs])
            for jj in range(keys.shape[0] // SEL_BLOCK):
                rows = slice(SEL_BLOCK * jj, SEL_BLOCK * (jj + 1))
                sb = s[rows] + bias_fn(rows, hs)
                s_dst[rows, hs] = sb
                bm_dst[8 * jj:8 * (jj + 1), hs] = jnp.max(sb.reshape(SEL_BLOCK // 8, 8, TQ), axis=0)

    def online(src, vt, masks):
        s_src, bm_src = src
        m_old = m_ref[...]
        vt1 = jnp.concatenate([vt, jnp.ones((16, vt.shape[1]), BF16)], axis=0)
        caps = [jnp.where(mk < 0.5 * NEG_INF, 0.0, 3e38).astype(BF16) for mk in masks]
        m_new, pv = [], []
        for hs in heads:
            smax = None
            for jj, mk in enumerate(masks):
                bm = bm_src[8 * jj:8 * (jj + 1), hs] + mk
                smax = bm if smax is None else jnp.maximum(smax, bm)
            mh = jnp.maximum(m_old[:, hs], jnp.max(smax, axis=0, keepdims=True))
            ps = [jnp.minimum(jnp.exp2(s_src[SEL_BLOCK * jj:SEL_BLOCK * (jj + 1), hs] - mh).astype(BF16), cap)
                  for jj, cap in enumerate(caps)]
            m_new.append(mh)
            pv.append(_dot(vt1, jnp.concatenate(ps, axis=0)))
        m_new = jnp.concatenate(m_new, axis=1)
        pv = jnp.concatenate(pv, axis=1)
        alpha = jnp.exp2(m_old - m_new)
        m_ref[...] = m_new
        l_ref[...] = alpha * l_ref[...] + pv[HEAD_DIM:HEAD_DIM + 1]
        acc_ref[...] = alpha * acc_ref[...] + pv[:HEAD_DIM]

    last = (nq * TQ) // TK

    def sel_scores(dst, kt):
        r0 = jnp.maximum(MAX_DIST + TK + TK * kt - t0, 0)
        scores(dst, ksl_ref[0, pl.ds(pl.multiple_of(kt * TK, TK), TK), :],
               lambda rows, hs: bsel_ref[0, pl.ds(pl.multiple_of(r0 + rows.start, SEL_BLOCK), SEL_BLOCK), hs])

    def sel_tile(cur, nxt, kt):
        sel_scores(nxt, jnp.minimum(kt + 1, last))
        gone = jnp.where(kt <= last, 0.0, NEG_INF)
        kt = jnp.minimum(kt, last)
        masks = [nm_ref[pl.ds(kt * (TK // SEL_BLOCK) + jj, 1), :] + gone for jj in range(TK // SEL_BLOCK)]
        online(cur, vslt_ref[0, :, pl.ds(pl.multiple_of(kt * TK, TK), TK)], masks)

    buf_a = (sa_ref, ma_ref)
    buf_b = (sb_ref, mb_ref)

    reset()
    kws = (kw0_ref, kw1_ref, kw2_ref)
    vws = (vw0_ref, vw1_ref, vw2_ref)

    def win_scores(dst, d):
        base = TQ * (N_WIN - 1 - d)
        scores(dst, kws[d][0], lambda rows, hs: bwin_ref[0, base + rows.start:base + rows.stop, hs])

    def win_online(src, d):
        gone = jnp.where(nq >= d, 0.0, NEG_INF) + jnp.zeros((1, TQ), F32)
        online(src, vws[d][0], [gone] * bpt)

    bufs = (buf_a, buf_b)
    win_scores(bufs[0], 0)
    for d in range(N_WIN):
        if d + 1 < N_WIN:
            win_scores(bufs[(d + 1) % 2], d + 1)
        win_online(bufs[d % 2], d)
    ow_ref[...] = acc_ref[...] / l_ref[...]
    reset()
    sel_scores(buf_a, 0)
    select_blocks()


    def tile_pair(i, c):
        sel_tile(buf_a, buf_b, 2 * i)
        sel_tile(buf_b, buf_a, 2 * i + 1)
        return c

    lax.fori_loop(0, (last + 2) // 2, tile_pair, 0)
    o_s = acc_ref[...] / l_ref[...]

    sg = _sigmoid(gt_ref[0].astype(F32))

    def gate(br):
        return jnp.concatenate([sg[HPG * br + h:HPG * br + h + 1] for h in range(HPG)], axis=1)

    ot = gate(0) * oc_ref[...] + gate(1) * o_s + gate(2) * ow_ref[...]
    stacked = jnp.concatenate([ot[:, TQ * h:TQ * (h + 1)] for h in range(HPG)], axis=0)
    o_ref[0] = stacked.T.astype(o_ref.dtype)


def _nsa(pn, pt, kc, vct, bsel, bwin, bcmp, cvec, ovl):
    b, t, _ = pn.shape
    nq = t // TQ
    ncp = kc.shape[2]
    ns = t // SEL_BLOCK
    q_rows = N_HEADS * HEAD_DIM
    gate_blk0 = (q_rows + 2 * N_KV * HEAD_DIM) // 16
    vsl_blk0 = q_rows // HEAD_DIM
    vw_blk0 = vsl_blk0 + N_KV
    ksl_col, kw_col = 2, 3

    def kw_spec(d):
        return pl.BlockSpec((1, TQ, 256), lambda i, j, n: (i, jnp.maximum(n - d, 0), kw_col))

    def vw_spec(d):
        return pl.BlockSpec((1, HEAD_DIM, TQ), lambda i, j, n: (i, vw_blk0 + j, jnp.maximum(n - d, 0)))

    per_g = lambda a: pl.BlockSpec((1,) + a.shape[1:], lambda i, j, n: (j,) + (0,) * (a.ndim - 1))
    return pl.pallas_call(
        _nsa_kernel,
        out_shape=jax.ShapeDtypeStruct((b, t, q_rows), BF16),
        grid=(b, N_KV, nq),
        in_specs=[
            pl.BlockSpec((1, HPG * HEAD_DIM, TQ), lambda i, j, n: (i, j, n)),
            pl.BlockSpec((1, 16, TQ), lambda i, j, n: (i, gate_blk0 + j, n)),
            pl.BlockSpec((1, 1, ncp, HEAD_DIM), lambda i, j, n: (i, j, 0, 0)),
            pl.BlockSpec((1, 1, HEAD_DIM, ncp), lambda i, j, n: (i, j, 0, 0)),
            pl.BlockSpec((1, t, 256), lambda i, j, n: (i, 0, ksl_col)),
            pl.BlockSpec((1, HEAD_DIM, t), lambda i, j, n: (i, vsl_blk0 + j, 0)),
            kw_spec(0), kw_spec(1), kw_spec(2),
            vw_spec(0), vw_spec(1), vw_spec(2),
            per_g(bsel), per_g(bwin), per_g(bcmp), per_g(cvec),
            pl.BlockSpec(ovl.shape, lambda i, j, n: (0, 0)),
        ],
        out_specs=pl.BlockSpec((1, TQ, HPG * HEAD_DIM), lambda i, j, n: (i, n, j)),
        scratch_shapes=[
            pltpu.VMEM((N_KV * HEAD_DIM, LQ), BF16),
            pltpu.VMEM((ncp, LQ), F32),
            pltpu.VMEM((ns, TQ), F32),
            pltpu.VMEM((1, LQ), F32), pltpu.VMEM((1, LQ), F32), pltpu.VMEM((HEAD_DIM, LQ), F32),
            pltpu.VMEM((HEAD_DIM, LQ), F32), pltpu.VMEM((HEAD_DIM, LQ), F32),
            pltpu.VMEM((TK, LQ), F32), pltpu.VMEM((TK, LQ), F32),
            pltpu.VMEM((8 * TK // SEL_BLOCK, LQ), F32), pltpu.VMEM((8 * TK // SEL_BLOCK, LQ), F32),
            pltpu.VMEM((ns, TQ), F32),
        ],
        compiler_params=_cparams(("parallel", "parallel", "arbitrary")),
        name="nsa",
    )(pt, pt, kc, vct, pn, pt, pn, pn, pn, pt, pt, pt, bsel, bwin, bcmp, cvec, ovl)


def _merge_kernel(ys_ref, o_ref, u_ref, h_ref, w1_ref, w2_ref, wo_ref, wga_ref, wgb_ref, wout_ref, g_ref,
                  out_ref, acc_ref):
    j = pl.program_id(1)

    @pl.when(j == 0)
    def _():
        acc_ref[...] = jnp.zeros_like(acc_ref)

    ys = ys_ref[...]
    ya = _dot(ys, w1_ref[...]) * _sigmoid(_dot(ys, w2_ref[...]))
    yb = _dot(o_ref[...], wo_ref[...])
    u = u_ref[...]
    z = _sigmoid(_dot(u, wga_ref[...])) * ya + _sigmoid(_dot(u, wgb_ref[...])) * yb
    acc_ref[...] += _dot(z.astype(BF16), wout_ref[...])

    @pl.when(j == pl.num_programs(1) - 1)
    def _():
        out_ref[...] = h_ref[...] + _rms(acc_ref[...]) * g_ref[...]


def _merge(ys, o, u, h, w1, w2, wo, wga, wgb, wout, g):
    t = ys.shape[0]
    n, d = h.shape
    tm, tn = min(MERGE_TM, t), MERGE_TN
    tpb = t // tm
    row = lambda a: pl.BlockSpec((tm, a.shape[1]), lambda i, j: (i, 0))
    col = lambda a: pl.BlockSpec((a.shape[0], tn), lambda i, j: (0, j))
    return pl.pallas_call(
        _merge_kernel,
        out_shape=jax.ShapeDtypeStruct((n, d), F32),
        grid=(n // tm, d // tn),
        in_specs=[pl.BlockSpec((tm, w1.shape[0]), lambda i, j: (i % tpb, i // tpb)),
                  row(o), row(u), row(h), col(w1), col(w2), col(wo), col(wga), col(wgb),
                  pl.BlockSpec((tn, d), lambda i, j: (j, 0)),
                  pl.BlockSpec((1, d), lambda i, j: (0, 0))],
        out_specs=pl.BlockSpec((tm, d), lambda i, j: (i, 0)),
        scratch_shapes=[pltpu.VMEM((tm, d), F32)],
        compiler_params=_cparams(("parallel", "arbitrary")),
        name="merge",
    )(ys, o, u, h, w1, w2, wo, wga, wgb, wout, g)


def _rel_bucket(dist):
    dist = jnp.maximum(dist, 0)
    max_exact = N_BUCKETS // 2
    d_f = jnp.maximum(dist, 1).astype(jnp.float32)
    large = max_exact + (jnp.log(d_f / max_exact) / math.log(MAX_DIST / max_exact)
                         * (N_BUCKETS - max_exact)).astype(jnp.int32)
    large = jnp.minimum(large, N_BUCKETS - 1)
    return jnp.where(dist < max_exact, dist, large)


def _bias_tables(rel_bias):
    table = (LOG2E * rel_bias).reshape(N_BUCKETS, N_KV, HPG)
    far = LOG2E * rel_bias[_rel_bucket(jnp.asarray(MAX_DIST))]
    cvec = jnp.broadcast_to(far.reshape(N_KV, 1, HPG, 1), (N_KV, 1, HPG, TQ)).reshape(N_KV, 1, LQ)
    tok = jnp.arange(TQ)[None, :]
    n_sel, n_win = MAX_DIST + 2 * TK, WINDOW + TQ
    d_sel = tok + MAX_DIST + TK - jnp.arange(n_sel)[:, None]
    d_win = tok + WINDOW - jnp.arange(n_win)[:, None]
    d_cmp = tok - CMP_STRIDE * (jnp.arange(CMP_BAND)[:, None] - CMP_NEAR) - (CMP_LEN - 1)
    dist = jnp.concatenate([d_sel, d_win, d_cmp], axis=0)
    ok = jnp.concatenate([d_sel >= 0, (d_win >= 0) & (d_win < WINDOW), d_cmp >= 0], axis=0)
    onehot = jax.nn.one_hot(_rel_bucket(dist), N_BUCKETS, dtype=F32)
    v = jnp.einsum('rtb,bgh->grht', onehot, table, precision=lax.Precision.HIGHEST)
    v = jnp.where(ok[None, :, None, :], v, NEG_INF).reshape(N_KV, dist.shape[0], LQ)
    bsel, bwin, bcmp = v[:, :n_sel], v[:, n_sel:n_sel + n_win], v[:, n_sel + n_win:]
    bcmp = jnp.where(bcmp > 0.5 * NEG_INF, bcmp - cvec, NEG_INF)
    return bsel, bwin, bcmp, cvec


def _overlap_t(t):
    nc = t // CMP_STRIDE
    cmp_start = np.arange(nc) * CMP_STRIDE
    sel_start = np.arange(t // SEL_BLOCK) * SEL_BLOCK
    ov = ((cmp_start[None, :] < sel_start[:, None] + SEL_BLOCK)
          & (cmp_start[None, :] + CMP_LEN > sel_start[:, None]))
    ov[:, nc - 1] = False
    out = np.zeros((t // SEL_BLOCK, CMP_FRONT + nc), np.float32)
    out[:, CMP_FRONT:] = ov
    return jnp.asarray(out, BF16)


def _s5_tables(a_re, a_im, log_dt, b_re, b_im, c_re, c_im, d_skip, batch):
    dt = jnp.exp(log_dt)[:, None]
    lam_re = jnp.minimum(a_re, -1e-4)
    lam_im = a_im
    mag = jnp.exp(lam_re * dt)
    ab_re = mag * jnp.cos(lam_im * dt)
    ab_im = mag * jnp.sin(lam_im * dt)
    den = lam_re * lam_re + lam_im * lam_im
    n_re = ab_re - 1.0
    n_im = ab_im
    co_re = (n_re * lam_re + n_im * lam_im) / den
    co_im = (n_im * lam_re - n_re * lam_im) / den
    bb_re = co_re[..., None] * b_re - co_im[..., None] * b_im
    bb_im = co_re[..., None] * b_im + co_im[..., None] * b_re
    per_slab = S5_TILES // S5_SLABS
    sel = jnp.asarray(np.eye(per_slab, dtype=np.float32)[np.arange(S5_TILES) % per_slab])
    eye2 = jnp.eye(2, dtype=F32)
    bb = jnp.stack([bb_re, bb_im], 0).reshape(2, 2, S5_TILES, 2, SSM_STATE, SSM_GROUP)
    wbu = jnp.einsum('rhjgpc,jm,gk->jhmgcrkp', bb, sel, eye2).reshape(S5_TILES, MXU_DEPTH, 2 * LANES)
    cc = jnp.stack([c_re, -c_im], 0).reshape(2, 2, S5_TILES, 2, SSM_GROUP, SSM_STATE)
    wc = jnp.einsum('rhjgcp,jm,gk->jrkphmgc', cc, sel, eye2).reshape(S5_TILES, MXU_DEPTH, 2 * LANES)

    def lanes(a):
        a = a.reshape(2, 1, S5_LANES)
        return jnp.broadcast_to(a.transpose(1, 0, 2), (batch, 2, S5_LANES)).reshape(2 * batch, S5_LANES)

    dmat = jnp.broadcast_to(d_skip.reshape(1, 2, S5_HALF), (batch, 2, S5_HALF)).reshape(2 * batch, S5_HALF)
    return wbu.astype(BF16), wc.astype(BF16), lanes(ab_re), lanes(ab_im), dmat


def kernel(x, ffn1_pre_g, ffn1_w_gate, ffn1_w_up, ffn1_w_down, ffn1_post_g, mix_pre_g, w_in, ssm_a_re, ssm_a_im, ssm_log_dt, ssm_b_re, ssm_b_im, ssm_c_re, ssm_c_im, ssm_d, ssm_glu_w1, ssm_glu_w2, cmp_pos, cmp_k_w1, cmp_k_w2, cmp_v_w1, cmp_v_w2, nsa_w_o, w_out, mix_post_g, ffn2_pre_g, ffn2_w_gate, ffn2_w_up, ffn2_w_down, ffn2_post_g, rel_bias):
    b, t, d = x.shape
    n = b * t
    assert 2 * b == SUBLANES and t % TK == 0 and TK % TQ == 0 and t >= MAX_DIST + TK and d == D_MODEL
    bf = lambda a: a.astype(BF16)
    h = x.reshape(n, d)
    for l in range(ffn1_pre_g.shape[0]):
        h, u = _ffn(h, ffn1_pre_g[l][None], bf(ffn1_w_gate[l]), bf(ffn1_w_up[l]), bf(ffn1_w_down[l]),
                    ffn1_post_g[l][None], mix_pre_g[l][None], True)

        w = w_in[l]
        o_q = SSM_WIDTH
        o_kv = o_q + N_HEADS * HEAD_DIM
        kvw = N_KV * HEAD_DIM
        o_gn = o_kv + 6 * kvw
        o_ga = o_gn + 3 * N_HEADS
        kv = lambda i: w[:, o_kv + i * kvw:o_kv + (i + 1) * kvw]
        wn = bf(jnp.concatenate([w[:, :o_q], kv(0), kv(1), kv(2), kv(4)], axis=1))
        gate_cols = np.full((N_KV, 4, HPG), -1)
        for g in range(N_KV):
            for br in range(3):
                for hh in range(HPG):
                    gate_cols[g, br, hh] = o_gn + 3 * HPG * g + 3 * hh + br
        gate_cols = gate_cols.reshape(-1)
        wg = jnp.where((gate_cols >= 0)[None, :], w[:, np.maximum(gate_cols, 0)], 0.0)
        w_q = w[:, o_q:o_kv] * (HEAD_DIM ** -0.5 * LOG2E)
        wt = bf(jnp.concatenate([w_q, kv(3), kv(5), wg], axis=1).T)
        us, pn, pt = _proj(u.reshape(b, t, d), wn, wt)

        wbu, wc, a_re, a_im, dmat = _s5_tables(ssm_a_re[l], ssm_a_im[l], ssm_log_dt[l], ssm_b_re[l],
                                               ssm_b_im[l], ssm_c_re[l], ssm_c_im[l], ssm_d[l], b)
        ys = _s5(us.reshape(t * 2 * b, S5_HALF), wbu, wc, a_re, a_im, dmat).reshape(t, b * SSM_WIDTH)

        def rows16(a):
            a = a.reshape(b, t // CMP_STRIDE, CMP_STRIDE, N_KV, HEAD_DIM)
            return a.transpose(0, 3, 1, 2, 4).reshape(b, N_KV, t // CMP_STRIDE, CMP_STRIDE * HEAD_DIM)

        half = CMP_STRIDE * HEAD_DIM
        w1cat = lambda w1: bf(jnp.concatenate([w1[:half], w1[half:]], axis=1))
        pos8 = jnp.zeros((8, half), F32).at[0].set(cmp_pos[l][:CMP_STRIDE].reshape(-1))
        pos8 = bf(pos8.at[1].set(cmp_pos[l][CMP_STRIDE:].reshape(-1)))
        kc, vct = _compress(rows16(pn[:, :, :kvw]), rows16(pn[:, :, kvw:2 * kvw]), pos8,
                            w1cat(cmp_k_w1[l]), bf(cmp_k_w2[l]), w1cat(cmp_v_w1[l]), bf(cmp_v_w2[l].T))
        bsel, bwin, bcmp, cvec = _bias_tables(rel_bias)
        o_nsa = _nsa(pn, pt, kc, vct, bsel, bwin, bcmp, cvec, _overlap_t(t))

        h = _merge(ys, o_nsa.reshape(n, N_HEADS * HEAD_DIM), u, h, bf(ssm_glu_w1[l]), bf(ssm_glu_w2[l]),
                   bf(nsa_w_o[l]), bf(w[:, o_ga:o_ga + d]), bf(w[:, o_ga + d:o_ga + 2 * d]),
                   bf(w_out[l]), mix_post_g[l][None])

        h, _ = _ffn(h, ffn2_pre_g[l][None], bf(ffn2_w_gate[l]), bf(ffn2_w_up[l]), bf(ffn2_w_down[l]),
                    ffn2_post_g[l][None], ffn2_post_g[l][None], False)
    return h.reshape(b, t, d)
```

```python
import functools
import math

import numpy as np
import jax
import jax.numpy as jnp
from jax import lax
from jax.experimental import pallas as pl
from jax.experimental.pallas import tpu as pltpu

F32 = jnp.float32
BF16 = jnp.bfloat16

D_MODEL = 2048
D_FF = 5632
EPS = 1e-6
SSM_WIDTH = 1024
SSM_GROUP = 16
SSM_GROUPS = 64
SSM_STATE = 64
N_HEADS = 16
N_KV = 4
HPG = 4
HEAD_DIM = 64
CMP_LEN = 32
CMP_STRIDE = 16
CMP_HID = 256
SEL_BLOCK = 64
N_SEL = 16
WINDOW = 512
SEL_FORCE = 1e4
NEG_INF = -1e30
LOG2E = 1.4426950408889634
N_BUCKETS = 32
MAX_DIST = 1024

V7X_VMEM_BYTES = 64 * 1024 * 1024
VMEM_LIMIT = V7X_VMEM_BYTES - 8 * 1024 * 1024

FFN_TM = 512
FFN_TF = 512
PROJ_TM = 512
LANES = 128
SUBLANES = 8
MXU_DEPTH = 256
S5_LC = 128
S5_CW = 512
S5_HALF = SSM_WIDTH // 2
S5_LANES = SSM_GROUPS * SSM_STATE // 2
S5_TILES = S5_LANES // LANES
S5_SLABS = S5_HALF // LANES
TQ = 256
TK = 512
LQ = HPG * TQ
CMP_FRONT = 128
CMP_NEAR = 72
CMP_BAND = CMP_NEAR + TQ // CMP_STRIDE
N_WIN = WINDOW // TQ + 1
MERGE_TM = 1024
MERGE_TN = 512
OUT_TM = 512


def _cparams(sem):
    return pltpu.CompilerParams(dimension_semantics=sem, vmem_limit_bytes=VMEM_LIMIT)


def _rms(x):
    ms = jnp.mean(x * x, axis=-1, keepdims=True)
    return x * lax.rsqrt(ms + EPS)


def _sigmoid(x):
    return 1.0 / (1.0 + jnp.exp(-x))


def _gelu(x):
    return 0.5 * x * (1.0 + jnp.tanh(math.sqrt(2.0 / math.pi) * (x + 0.044715 * (x * x * x))))


def _dot(a, b):
    return jnp.dot(a, b, preferred_element_type=F32)


def _dot_nt(a, b):
    return lax.dot_general(a, b, (((1,), (1,)), ((), ())), preferred_element_type=F32)


def _ffn_kernel(x_ref, pre_ref, wg_ref, wu_ref, wd_ref, post_ref, nxt_ref, *rest, emit_next):
    if emit_next:
        o_ref, u_ref, xn_ref, acc_ref = rest
    else:
        o_ref, xn_ref, acc_ref = rest
    f = pl.program_id(1)

    @pl.when(f == 0)
    def _():
        xn_ref[...] = (_rms(x_ref[...]) * pre_ref[...]).astype(BF16)
        acc_ref[...] = jnp.zeros_like(acc_ref)

    xn = xn_ref[...]
    g = _dot(xn, wg_ref[...])
    u = _dot(xn, wu_ref[...])
    a = (g * _sigmoid(g) * u).astype(BF16)
    acc_ref[...] += _dot(a, wd_ref[...])

    @pl.when(f == pl.num_programs(1) - 1)
    def _():
        h = x_ref[...] + 0.5 * (_rms(acc_ref[...]) * post_ref[...])
        o_ref[...] = h
        if emit_next:
            u_ref[...] = (_rms(h) * nxt_ref[...]).astype(BF16)


def _ffn(x, pre_g, wg, wu, wd, post_g, nxt_g, emit_next):
    n, d = x.shape
    nf = wg.shape[1]
    tm, tf = min(FFN_TM, n), FFN_TF
    out_shape = [jax.ShapeDtypeStruct((n, d), F32)]
    out_specs = [pl.BlockSpec((tm, d), lambda i, f: (i, 0))]
    if emit_next:
        out_shape.append(jax.ShapeDtypeStruct((n, d), BF16))
        out_specs.append(pl.BlockSpec((tm, d), lambda i, f: (i, 0)))
    vec = pl.BlockSpec((1, d), lambda i, f: (0, 0))
    res = pl.pallas_call(
        functools.partial(_ffn_kernel, emit_next=emit_next),
        out_shape=out_shape,
        grid=(n // tm, nf // tf),
        in_specs=[
            pl.BlockSpec((tm, d), lambda i, f: (i, 0)),
            vec,
            pl.BlockSpec((d, tf), lambda i, f: (0, f)),
            pl.BlockSpec((d, tf), lambda i, f: (0, f)),
            pl.BlockSpec((tf, d), lambda i, f: (f, 0)),
            vec,
            vec,
        ],
        out_specs=out_specs,
        scratch_shapes=[pltpu.VMEM((tm, d), BF16), pltpu.VMEM((tm, d), F32)],
        compiler_params=_cparams(("parallel", "arbitrary")),
        name="ffn",
    )(x, pre_g, wg, wu, wd, post_g, nxt_g)
    return res if emit_next else (res[0], None)


def _proj_kernel(u_ref, wn_ref, wt_ref, us_ref, pn_ref, pt_ref):
    u = u_ref[0]
    nat = _dot(u, wn_ref[...])
    us_ref[...] = nat[:, :SSM_WIDTH].astype(us_ref.dtype)
    pn_ref[0] = nat[:, SSM_WIDTH:].astype(pn_ref.dtype)
    pt_ref[0] = _dot_nt(wt_ref[...], u).astype(pt_ref.dtype)


def _proj(u, wn, wt):
    b, t, d = u.shape
    tm = min(PROJ_TM, t)
    nk = wn.shape[1] - SSM_WIDTH
    return pl.pallas_call(
        _proj_kernel,
        out_shape=[jax.ShapeDtypeStruct((t, b * SSM_WIDTH), BF16),
                   jax.ShapeDtypeStruct((b, t, nk), BF16),
                   jax.ShapeDtypeStruct((b, wt.shape[0], t), BF16)],
        grid=(b, t // tm),
        in_specs=[
            pl.BlockSpec((1, tm, d), lambda i, j: (i, j, 0)),
            pl.BlockSpec(wn.shape, lambda i, j: (0, 0)),
            pl.BlockSpec(wt.shape, lambda i, j: (0, 0)),
        ],
        out_specs=[
            pl.BlockSpec((tm, SSM_WIDTH), lambda i, j: (j, i)),
            pl.BlockSpec((1, tm, nk), lambda i, j: (i, j, 0)),
            pl.BlockSpec((1, wt.shape[0], tm), lambda i, j: (i, 0, j)),
        ],
        compiler_params=_cparams(("parallel", "parallel")),
        name="proj",
    )(u, wn, wt)


def _s5_kernel(lhs_ref, wbu_ref, wc_ref, are_ref, aim_ref, dmat_ref, y_ref,
               xre_ref, xim_ref, bre_ref, bim_ref):
    lc = lhs_ref.shape[0] // SUBLANES

    @pl.when(pl.program_id(0) == 0)
    def _():
        xre_ref[...] = jnp.zeros_like(xre_ref)
        xim_ref[...] = jnp.zeros_like(xim_ref)

    row_odd = (lax.broadcasted_iota(jnp.int32, (lhs_ref.shape[0], 1), 0) % 2) == 1
    tiles_per_slab = S5_TILES // S5_SLABS
    for j in range(S5_TILES):
        jq = j // tiles_per_slab
        slab = lhs_ref[:, LANES * jq:LANES * (jq + 1)]
        zero = jnp.zeros_like(slab)
        lj = jnp.concatenate([jnp.where(row_odd, zero, slab), jnp.where(row_odd, slab, zero)], axis=1)
        out = _dot(lj, wbu_ref[j])
        bre_ref[:, LANES * j:LANES * (j + 1)] = out[:, :LANES]
        bim_ref[:, LANES * j:LANES * (j + 1)] = out[:, LANES:]

    for cc in range(bre_ref.shape[1] // S5_CW):
        cs = slice(cc * S5_CW, (cc + 1) * S5_CW)
        ar = are_ref[:, cs]
        ai = aim_ref[:, cs]

        def step(i, carry, cs=cs, ar=ar, ai=ai):
            xr, xi = carry
            rows = pl.ds(pl.multiple_of(i * SUBLANES, SUBLANES), SUBLANES)
            nr = ar * xr - ai * xi + bre_ref[rows, cs]
            ni = ar * xi + ai * xr + bim_ref[rows, cs]
            bre_ref[rows, cs] = nr
            bim_ref[rows, cs] = ni
            return nr, ni

        xr, xi = lax.fori_loop(0, lc, step, (xre_ref[:, cs], xim_ref[:, cs]), unroll=4)
        xre_ref[:, cs] = xr
        xim_ref[:, cs] = xi

    odd = (lax.broadcasted_iota(jnp.int32, (1, SUBLANES, 1), 1) % 2) == 1
    for jq in range(S5_SLABS):
        acc = None
        for jm in range(tiles_per_slab):
            j = tiles_per_slab * jq + jm
            xj = jnp.concatenate([bre_ref[:, LANES * j:LANES * (j + 1)],
                                  bim_ref[:, LANES * j:LANES * (j + 1)]], axis=1).astype(BF16)
            part = _dot(xj, wc_ref[j])
            acc = part if acc is None else acc + part
        acc3 = acc.reshape(lc, SUBLANES, 2 * LANES)
        y = jnp.where(odd, acc3[:, :, LANES:], acc3[:, :, :LANES])
        u_own = lhs_ref[:, LANES * jq:LANES * (jq + 1)].astype(F32).reshape(lc, SUBLANES, LANES)
        y = y + dmat_ref[:, LANES * jq:LANES * (jq + 1)][None] * u_own
        y_ref[:, LANES * jq:LANES * (jq + 1)] = _gelu(y).reshape(lc * SUBLANES, LANES).astype(y_ref.dtype)


def _s5(lhs, wbu, wc, a_re, a_im, dmat):
    rows = lhs.shape[0]
    r = SUBLANES * S5_LC
    nl = a_re.shape[1]
    return pl.pallas_call(
        _s5_kernel,
        out_shape=jax.ShapeDtypeStruct((rows, S5_HALF), BF16),
        grid=(rows // r,),
        in_specs=[
            pl.BlockSpec((r, S5_HALF), lambda c: (c, 0)),
            pl.BlockSpec(wbu.shape, lambda c: (0, 0, 0)),
            pl.BlockSpec(wc.shape, lambda c: (0, 0, 0)),
            pl.BlockSpec(a_re.shape, lambda c: (0, 0)),
            pl.BlockSpec(a_im.shape, lambda c: (0, 0)),
            pl.BlockSpec(dmat.shape, lambda c: (0, 0)),
        ],
        out_specs=pl.BlockSpec((r, S5_HALF), lambda c: (c, 0)),
        scratch_shapes=[pltpu.VMEM((SUBLANES, nl), F32), pltpu.VMEM((SUBLANES, nl), F32),
                        pltpu.VMEM((r, nl), F32), pltpu.VMEM((r, nl), F32)],
        compiler_params=_cparams(("arbitrary",)),
        name="s5",
    )(lhs, wbu, wc, a_re, a_im, dmat)


def _compress_kernel(rk_ref, rv_ref, pos_ref, w1k_ref, w2k_ref, w1v_ref, w2vt_ref, k_ref, vt_ref):
    m = rk_ref.shape[2]

    def hidden(r_ref, w1_ref):
        z = _dot(r_ref[0, 0], w1_ref[...])
        pz = _dot(pos_ref[...], w1_ref[...])
        posvec = pz[0:1, :CMP_HID] + pz[1:2, CMP_HID:]
        zhi = pltpu.roll(z[:, CMP_HID:], m - 1, 0)
        return _gelu(z[:, :CMP_HID] + zhi + posvec).astype(BF16)

    hk = hidden(rk_ref, w1k_ref)
    k_ref[0, 0, :CMP_FRONT, :] = jnp.zeros((CMP_FRONT, HEAD_DIM), k_ref.dtype)
    k_ref[0, 0, CMP_FRONT:, :] = _dot(hk, w2k_ref[...]).astype(k_ref.dtype)
    hv = hidden(rv_ref, w1v_ref)
    vt_ref[0, 0, :, :CMP_FRONT] = jnp.zeros((HEAD_DIM, CMP_FRONT), vt_ref.dtype)
    vt_ref[0, 0, :, CMP_FRONT:] = _dot_nt(w2vt_ref[...], hv).astype(vt_ref.dtype)


def _compress(rk, rv, pos8, w1k, w2k, w1v, w2vt):
    b, g, m, kd = rk.shape
    ncp = CMP_FRONT + m
    full = lambda a: pl.BlockSpec(a.shape, lambda i, j: (0,) * a.ndim)
    return pl.pallas_call(
        _compress_kernel,
        out_shape=[jax.ShapeDtypeStruct((b, g, ncp, HEAD_DIM), BF16),
                   jax.ShapeDtypeStruct((b, g, HEAD_DIM, ncp), BF16)],
        grid=(b, g),
        in_specs=[
            pl.BlockSpec((1, 1, m, kd), lambda i, j: (i, j, 0, 0)),
            pl.BlockSpec((1, 1, m, kd), lambda i, j: (i, j, 0, 0)),
            full(pos8), full(w1k), full(w2k), full(w1v), full(w2vt),
        ],
        out_specs=[
            pl.BlockSpec((1, 1, ncp, HEAD_DIM), lambda i, j: (i, j, 0, 0)),
            pl.BlockSpec((1, 1, HEAD_DIM, ncp), lambda i, j: (i, j, 0, 0)),
        ],
        compiler_params=_cparams(("parallel", "parallel")),
        name="compress",
    )(rk, rv, pos8, w1k, w2k, w1v, w2vt)


def _nsa_kernel(qt_ref, gt_ref, kc_ref, vct_ref, ksl_ref, vslt_ref,
                kw0_ref, kw1_ref, kw2_ref, vw0_ref, vw1_ref, vw2_ref,
                bsel_ref, bwin_ref, bcmp_ref, cvec_ref, ovl_ref,
                o_ref,
                qpad_ref, sc_ref, nm_ref, m_ref, l_ref, acc_ref, oc_ref, ow_ref,
                sa_ref, sb_ref, ma_ref, mb_ref, imp_ref):
    g = pl.program_id(1)
    nq = pl.program_id(2)
    t0 = nq * TQ
    ncp = kc_ref.shape[2]
    ns = nm_ref.shape[0]
    cpq = TQ // CMP_STRIDE
    bpt = TQ // SEL_BLOCK

    q4 = qt_ref[0]

    @pl.when(nq == 0)
    def _():
        qpad_ref[...] = jnp.zeros_like(qpad_ref)

    grow = pl.ds(pl.multiple_of(g * HEAD_DIM, HEAD_DIM), HEAD_DIM)
    for h in range(HPG):
        qpad_ref[grow, TQ * h:TQ * (h + 1)] = q4[HEAD_DIM * h:HEAD_DIM * (h + 1)]
    heads = [slice(TQ * h, TQ * (h + 1)) for h in range(HPG)]
    t_tok = t0 + lax.broadcasted_iota(jnp.int32, (1, TQ), 1)

    ch = CMP_FRONT
    hi_row = cpq * nq + CMP_FRONT + cpq
    n_live = (hi_row + ch - 1) // ch

    def chunk(c):
        return pl.ds(pl.multiple_of(c * ch, ch), ch)

    def cmp_scores(c, carry):
        row = c * ch + lax.broadcasted_iota(jnp.int32, (ch, 1), 0)
        for h, hs in enumerate(heads):
            s = _dot(kc_ref[0, 0, chunk(c), :], q4[HEAD_DIM * h:HEAD_DIM * (h + 1)])
            sc_ref[chunk(c), hs] = s + jnp.where(row < hi_row, cvec_ref[0, :, hs], NEG_INF)
        return carry

    band_row = cpq * nq + CMP_FRONT - CMP_NEAR

    @pl.when(band_row < ch)
    def _():
        sc_ref[0:ch, :] = jnp.full((ch, LQ), NEG_INF, F32)

    lax.fori_loop(1, n_live, cmp_scores, 0)
    band = pl.ds(pl.multiple_of(band_row, 8), CMP_BAND)
    sc_ref[band, :] = sc_ref[band, :] + bcmp_ref[0]

    def cmp_max(c, m8):
        return jnp.maximum(m8, jnp.max(sc_ref[chunk(c), :].reshape(ch // 8, 8, LQ), axis=0))

    m_cmp = jnp.max(lax.fori_loop(1, n_live, cmp_max, jnp.full((8, LQ), -3e38, F32)), axis=0, keepdims=True)

    def cmp_exp(c, l8):
        e = jnp.exp2(sc_ref[chunk(c), :] - m_cmp)
        sc_ref[chunk(c), :] = e
        return l8 + jnp.sum(e.reshape(ch // 8, 8, LQ), axis=0)

    l_cmp = jnp.sum(lax.fori_loop(1, n_live, cmp_exp, jnp.zeros((8, LQ), F32)), axis=0, keepdims=True)
    has_cmp = (t_tok >= CMP_LEN - 1).astype(F32)
    p_scale = jnp.concatenate([has_cmp] * HPG, axis=1) / l_cmp
    oc_ref[...] = jnp.zeros_like(oc_ref)
    imp_ref[...] = jnp.zeros_like(imp_ref)

    def cmp_out(c, carry):
        p = sc_ref[chunk(c), :] * p_scale
        oc_ref[...] += _dot(vct_ref[0, 0, :, chunk(c)], p.astype(BF16))
        ps = p[:, heads[0]] + p[:, heads[1]] + p[:, heads[2]] + p[:, heads[3]]
        ps_hi = ps.astype(BF16)
        ps_lo = (ps - ps_hi.astype(F32)).astype(BF16)
        ovl = ovl_ref[:, chunk(c)]
        imp_ref[...] += _dot(ovl, ps_hi) + _dot(ovl, ps_lo)
        return carry

    lax.fori_loop(1, n_live, cmp_out, 0)

    def select_blocks():
        imp = imp_ref[...]
        blk = lax.broadcasted_iota(jnp.int32, (ns, TQ), 0)
        tt = t0 + lax.broadcasted_iota(jnp.int32, (ns, TQ), 1)
        valid_b = blk * SEL_BLOCK <= tt
        cur = tt // SEL_BLOCK
        forced = valid_b & ((blk == 0) | (blk == cur) | (blk == cur - 1))
        taken = -3e38
        sc = jnp.where(forced, taken, jnp.where(valid_b, imp, -SEL_FORCE))
        blkf = blk.astype(F32)
        n_forced = 3
        for _ in range(min(N_SEL, ns) - n_forced):
            mx = jnp.max(sc, axis=0, keepdims=True)
            first = jnp.min(jnp.where(sc == mx, blkf, float(ns)), axis=0, keepdims=True)
            sc = jnp.where(blkf == first, taken, sc)
        nm_ref[...] = jnp.where(sc == taken, 0.0, NEG_INF)

    def reset():
        m_ref[...] = jnp.full_like(m_ref, NEG_INF)
        l_ref[...] = jnp.zeros_like(l_ref)
        acc_ref[...] = jnp.zeros_like(acc_ref)

    def scores(dst, keys, bias_fn):
        s_dst, bm_dst = dst
        for hs in heads:
            s = _dot(keys, qpad_ref[:, hs])
            for jj in range(keys.shape[0] // SEL_BLOCK):
                rows = slice(SEL_BLOCK * jj, SEL_BLOCK * (jj + 1))
                sb = s[rows] + bias_fn(rows, hs)
                s_dst[rows, hs] = sb
                bm_dst[8 * jj:8 * (jj + 1), hs] = jnp.max(sb.reshape(SEL_BLOCK // 8, 8, TQ), axis=0)

    def online(src, vt, masks):
        s_src, bm_src = src
        m_old = m_ref[...]
        vt1 = jnp.concatenate([vt, jnp.ones((16, vt.shape[1]), BF16)], axis=0)
        caps = [jnp.where(mk < 0.5 * NEG_INF, 0.0, 3e38).astype(BF16) for mk in masks]
        m_new, pv = [], []
        for hs in heads:
            smax = None
            for jj, mk in enumerate(masks):
                bm = bm_src[8 * jj:8 * (jj + 1), hs] + mk
                smax = bm if smax is None else jnp.maximum(smax, bm)
            mh = jnp.maximum(m_old[:, hs], jnp.max(smax, axis=0, keepdims=True))
            ps = [jnp.minimum(jnp.exp2(s_src[SEL_BLOCK * jj:SEL_BLOCK * (jj + 1), hs] - mh).astype(BF16), cap)
                  for jj, cap in enumerate(caps)]
            m_new.append(mh)
            pv.append(_dot(vt1, jnp.concatenate(ps, axis=0)))
        m_new = jnp.concatenate(m_new, axis=1)
        pv = jnp.concatenate(pv, axis=1)
        alpha = jnp.exp2(m_old - m_new)
        m_ref[...] = m_new
        l_ref[...] = alpha * l_ref[...] + pv[HEAD_DIM:HEAD_DIM + 1]
        acc_ref[...] = alpha * acc_ref[...] + pv[:HEAD_DIM]

    last = (nq * TQ) // TK

    def sel_scores(dst, kt):
        r0 = jnp.maximum(MAX_DIST + TK + TK * kt - t0, 0)
        scores(dst, ksl_ref[0, pl.ds(pl.multiple_of(kt * TK, TK), TK), :],
               lambda rows, hs: bsel_ref[0, pl.ds(pl.multiple_of(r0 + rows.start, SEL_BLOCK), SEL_BLOCK), hs])

    def sel_tile(cur, nxt, kt):
        sel_scores(nxt, jnp.minimum(kt + 1, last))
        gone = jnp.where(kt <= last, 0.0, NEG_INF)
        kt = jnp.minimum(kt, last)
        masks = [nm_ref[pl.ds(kt * (TK // SEL_BLOCK) + jj, 1), :] + gone for jj in range(TK // SEL_BLOCK)]
        online(cur, vslt_ref[0, :, pl.ds(pl.multiple_of(kt * TK, TK), TK)], masks)

    buf_a = (sa_ref, ma_ref)
    buf_b = (sb_ref, mb_ref)

    reset()
    kws = (kw0_ref, kw1_ref, kw2_ref)
    vws = (vw0_ref, vw1_ref, vw2_ref)

    def win_scores(dst, d):
        base = TQ * (N_WIN - 1 - d)
        scores(dst, kws[d][0], lambda rows, hs: bwin_ref[0, base + rows.start:base + rows.stop, hs])

    def win_online(src, d):
        gone = jnp.where(nq >= d, 0.0, NEG_INF) + jnp.zeros((1, TQ), F32)
        online(src, vws[d][0], [gone] * bpt)

    bufs = (buf_a, buf_b)
    win_scores(bufs[0], 0)
    for d in range(N_WIN):
        if d + 1 < N_WIN:
            win_scores(bufs[(d + 1) % 2], d + 1)
        win_online(bufs[d % 2], d)
    ow_ref[...] = acc_ref[...] / l_ref[...]
    reset()
    sel_scores(buf_a, 0)
    select_blocks()


    def tile_pair(i, c):
        sel_tile(buf_a, buf_b, 2 * i)
        sel_tile(buf_b, buf_a, 2 * i + 1)
        return c

    lax.fori_loop(0, (last + 2) // 2, tile_pair, 0)
    o_s = acc_ref[...] / l_ref[...]

    sg = _sigmoid(gt_ref[0].astype(F32))

    def gate(br):
        return jnp.concatenate([sg[HPG * br + h:HPG * br + h + 1] for h in range(HPG)], axis=1)

    ot = gate(0) * oc_ref[...] + gate(1) * o_s + gate(2) * ow_ref[...]
    stacked = jnp.concatenate([ot[:, TQ * h:TQ * (h + 1)] for h in range(HPG)], axis=0)
    o_ref[0] = stacked.T.astype(o_ref.dtype)


def _nsa(pn, pt, kc, vct, bsel, bwin, bcmp, cvec, ovl):
    b, t, _ = pn.shape
    nq = t // TQ
    ncp = kc.shape[2]
    ns = t // SEL_BLOCK
    q_rows = N_HEADS * HEAD_DIM
    gate_blk0 = (q_rows + 2 * N_KV * HEAD_DIM) // 16
    vsl_blk0 = q_rows // HEAD_DIM
    vw_blk0 = vsl_blk0 + N_KV
    ksl_col, kw_col = 2, 3

    def kw_spec(d):
        return pl.BlockSpec((1, TQ, 256), lambda i, j, n: (i, jnp.maximum(n - d, 0), kw_col))

    def vw_spec(d):
        return pl.BlockSpec((1, HEAD_DIM, TQ), lambda i, j, n: (i, vw_blk0 + j, jnp.maximum(n - d, 0)))

    per_g = lambda a: pl.BlockSpec((1,) + a.shape[1:], lambda i, j, n: (j,) + (0,) * (a.ndim - 1))
    return pl.pallas_call(
        _nsa_kernel,
        out_shape=jax.ShapeDtypeStruct((b, t, q_rows), BF16),
        grid=(b, N_KV, nq),
        in_specs=[
            pl.BlockSpec((1, HPG * HEAD_DIM, TQ), lambda i, j, n: (i, j, n)),
            pl.BlockSpec((1, 16, TQ), lambda i, j, n: (i, gate_blk0 + j, n)),
            pl.BlockSpec((1, 1, ncp, HEAD_DIM), lambda i, j, n: (i, j, 0, 0)),
            pl.BlockSpec((1, 1, HEAD_DIM, ncp), lambda i, j, n: (i, j, 0, 0)),
            pl.BlockSpec((1, t, 256), lambda i, j, n: (i, 0, ksl_col)),
            pl.BlockSpec((1, HEAD_DIM, t), lambda i, j, n: (i, vsl_blk0 + j, 0)),
            kw_spec(0), kw_spec(1), kw_spec(2),
            vw_spec(0), vw_spec(1), vw_spec(2),
            per_g(bsel), per_g(bwin), per_g(bcmp), per_g(cvec),
            pl.BlockSpec(ovl.shape, lambda i, j, n: (0, 0)),
        ],
        out_specs=pl.BlockSpec((1, TQ, HPG * HEAD_DIM), lambda i, j, n: (i, n, j)),
        scratch_shapes=[
            pltpu.VMEM((N_KV * HEAD_DIM, LQ), BF16),
            pltpu.VMEM((ncp, LQ), F32),
            pltpu.VMEM((ns, TQ), F32),
            pltpu.VMEM((1, LQ), F32), pltpu.VMEM((1, LQ), F32), pltpu.VMEM((HEAD_DIM, LQ), F32),
            pltpu.VMEM((HEAD_DIM, LQ), F32), pltpu.VMEM((HEAD_DIM, LQ), F32),
            pltpu.VMEM((TK, LQ), F32), pltpu.VMEM((TK, LQ), F32),
            pltpu.VMEM((8 * TK // SEL_BLOCK, LQ), F32), pltpu.VMEM((8 * TK // SEL_BLOCK, LQ), F32),
            pltpu.VMEM((ns, TQ), F32),
        ],
        compiler_params=_cparams(("parallel", "parallel", "arbitrary")),
        name="nsa",
    )(pt, pt, kc, vct, pn, pt, pn, pn, pn, pt, pt, pt, bsel, bwin, bcmp, cvec, ovl)


def _merge_kernel(ys_ref, o_ref, u_ref, w1_ref, w2_ref, wo_ref, wga_ref, wgb_ref, z_ref):
    ys = ys_ref[...]
    ya = _dot(ys, w1_ref[...]) * _sigmoid(_dot(ys, w2_ref[...]))
    yb = _dot(o_ref[...], wo_ref[...])
    u = u_ref[...]
    z = _sigmoid(_dot(u, wga_ref[...])) * ya + _sigmoid(_dot(u, wgb_ref[...])) * yb
    z_ref[...] = z.astype(z_ref.dtype)


def _merge(ys, o, u, w1, w2, wo, wga, wgb):
    t = ys.shape[0]
    n = o.shape[0]
    d = w1.shape[1]
    tm, tn = min(MERGE_TM, t), MERGE_TN
    tpb = t // tm
    row = lambda a: pl.BlockSpec((tm, a.shape[1]), lambda i, j: (i, 0))
    col = lambda a: pl.BlockSpec((a.shape[0], tn), lambda i, j: (0, j))
    return pl.pallas_call(
        _merge_kernel,
        out_shape=jax.ShapeDtypeStruct((n, d), BF16),
        grid=(n // tm, d // tn),
        in_specs=[pl.BlockSpec((tm, w1.shape[0]), lambda i, j: (i % tpb, i // tpb)),
                  row(o), row(u), col(w1), col(w2), col(wo), col(wga), col(wgb)],
        out_specs=pl.BlockSpec((tm, tn), lambda i, j: (i, j)),
        compiler_params=_cparams(("parallel", "parallel")),
        name="merge",
    )(ys, o, u, w1, w2, wo, wga, wgb)


def _outproj_kernel(z_ref, h_ref, w_ref, g_ref, o_ref):
    mixed = _dot(z_ref[...], w_ref[...])
    o_ref[...] = h_ref[...] + _rms(mixed) * g_ref[...]


def _outproj(z, h, w, g):
    n, d = h.shape
    tm = min(OUT_TM, n)
    return pl.pallas_call(
        _outproj_kernel,
        out_shape=jax.ShapeDtypeStruct((n, d), F32),
        grid=(n // tm,),
        in_specs=[
            pl.BlockSpec((tm, d), lambda i: (i, 0)),
            pl.BlockSpec((tm, d), lambda i: (i, 0)),
            pl.BlockSpec(w.shape, lambda i: (0, 0)),
            pl.BlockSpec((1, d), lambda i: (0, 0)),
        ],
        out_specs=pl.BlockSpec((tm, d), lambda i: (i, 0)),
        compiler_params=_cparams(("parallel",)),
        name="outproj",
    )(z, h, w, g)


def _rel_bucket(dist):
    dist = jnp.maximum(dist, 0)
    max_exact = N_BUCKETS // 2
    d_f = jnp.maximum(dist, 1).astype(jnp.float32)
    large = max_exact + (jnp.log(d_f / max_exact) / math.log(MAX_DIST / max_exact)
                         * (N_BUCKETS - max_exact)).astype(jnp.int32)
    large = jnp.minimum(large, N_BUCKETS - 1)
    return jnp.where(dist < max_exact, dist, large)


def _bias_tables(rel_bias):
    table = (LOG2E * rel_bias).reshape(N_BUCKETS, N_KV, HPG)
    far = LOG2E * rel_bias[_rel_bucket(jnp.asarray(MAX_DIST))]
    cvec = jnp.broadcast_to(far.reshape(N_KV, 1, HPG, 1), (N_KV, 1, HPG, TQ)).reshape(N_KV, 1, LQ)
    tok = jnp.arange(TQ)[None, :]
    n_sel, n_win = MAX_DIST + 2 * TK, WINDOW + TQ
    d_sel = tok + MAX_DIST + TK - jnp.arange(n_sel)[:, None]
    d_win = tok + WINDOW - jnp.arange(n_win)[:, None]
    d_cmp = tok - CMP_STRIDE * (jnp.arange(CMP_BAND)[:, None] - CMP_NEAR) - (CMP_LEN - 1)
    dist = jnp.concatenate([d_sel, d_win, d_cmp], axis=0)
    ok = jnp.concatenate([d_sel >= 0, (d_win >= 0) & (d_win < WINDOW), d_cmp >= 0], axis=0)
    onehot = jax.nn.one_hot(_rel_bucket(dist), N_BUCKETS, dtype=F32)
    v = jnp.einsum('rtb,bgh->grht', onehot, table, precision=lax.Precision.HIGHEST)
    v = jnp.where(ok[None, :, None, :], v, NEG_INF).reshape(N_KV, dist.shape[0], LQ)
    bsel, bwin, bcmp = v[:, :n_sel], v[:, n_sel:n_sel + n_win], v[:, n_sel + n_win:]
    bcmp = jnp.where(bcmp > 0.5 * NEG_INF, bcmp - cvec, NEG_INF)
    return bsel, bwin, bcmp, cvec


def _overlap_t(t):
    nc = t // CMP_STRIDE
    cmp_start = np.arange(nc) * CMP_STRIDE
    sel_start = np.arange(t // SEL_BLOCK) * SEL_BLOCK
    ov = ((cmp_start[None, :] < sel_start[:, None] + SEL_BLOCK)
          & (cmp_start[None, :] + CMP_LEN > sel_start[:, None]))
    ov[:, nc - 1] = False
    out = np.zeros((t // SEL_BLOCK, CMP_FRONT + nc), np.float32)
    out[:, CMP_FRONT:] = ov
    return jnp.asarray(out, BF16)


def _s5_tables(a_re, a_im, log_dt, b_re, b_im, c_re, c_im, d_skip, batch):
    dt = jnp.exp(log_dt)[:, None]
    lam_re = jnp.minimum(a_re, -1e-4)
    lam_im = a_im
    mag = jnp.exp(lam_re * dt)
    ab_re = mag * jnp.cos(lam_im * dt)
    ab_im = mag * jnp.sin(lam_im * dt)
    den = lam_re * lam_re + lam_im * lam_im
    n_re = ab_re - 1.0
    n_im = ab_im
    co_re = (n_re * lam_re + n_im * lam_im) / den
    co_im = (n_im * lam_re - n_re * lam_im) / den
    bb_re = co_re[..., None] * b_re - co_im[..., None] * b_im
    bb_im = co_re[..., None] * b_im + co_im[..., None] * b_re
    per_slab = S5_TILES // S5_SLABS
    sel = jnp.asarray(np.eye(per_slab, dtype=np.float32)[np.arange(S5_TILES) % per_slab])
    eye2 = jnp.eye(2, dtype=F32)
    bb = jnp.stack([bb_re, bb_im], 0).reshape(2, 2, S5_TILES, 2, SSM_STATE, SSM_GROUP)
    wbu = jnp.einsum('rhjgpc,jm,gk->jhmgcrkp', bb, sel, eye2).reshape(S5_TILES, MXU_DEPTH, 2 * LANES)
    cc = jnp.stack([c_re, -c_im], 0).reshape(2, 2, S5_TILES, 2, SSM_GROUP, SSM_STATE)
    wc = jnp.einsum('rhjgcp,jm,gk->jrkphmgc', cc, sel, eye2).reshape(S5_TILES, MXU_DEPTH, 2 * LANES)

    def lanes(a):
        a = a.reshape(2, 1, S5_LANES)
        return jnp.broadcast_to(a.transpose(1, 0, 2), (batch, 2, S5_LANES)).reshape(2 * batch, S5_LANES)

    dmat = jnp.broadcast_to(d_skip.reshape(1, 2, S5_HALF), (batch, 2, S5_HALF)).reshape(2 * batch, S5_HALF)
    return wbu.astype(BF16), wc.astype(BF16), lanes(ab_re), lanes(ab_im), dmat


def kernel(x, ffn1_pre_g, ffn1_w_gate, ffn1_w_up, ffn1_w_down, ffn1_post_g, mix_pre_g, w_in, ssm_a_re, ssm_a_im, ssm_log_dt, ssm_b_re, ssm_b_im, ssm_c_re, ssm_c_im, ssm_d, ssm_glu_w1, ssm_glu_w2, cmp_pos, cmp_k_w1, cmp_k_w2, cmp_v_w1, cmp_v_w2, nsa_w_o, w_out, mix_post_g, ffn2_pre_g, ffn2_w_gate, ffn2_w_up, ffn2_w_down, ffn2_post_g, rel_bias):
    b, t, d = x.shape
    n = b * t
    assert 2 * b == SUBLANES and t % TK == 0 and TK % TQ == 0 and t >= MAX_DIST + TK and d == D_MODEL
    bf = lambda a: a.astype(BF16)
    h = x.reshape(n, d)
    for l in range(ffn1_pre_g.shape[0]):
        h, u = _ffn(h, ffn1_pre_g[l][None], bf(ffn1_w_gate[l]), bf(ffn1_w_up[l]), bf(ffn1_w_down[l]),
                    ffn1_post_g[l][None], mix_pre_g[l][None], True)

        w = w_in[l]
        o_q = SSM_WIDTH
        o_kv = o_q + N_HEADS * HEAD_DIM
        kvw = N_KV * HEAD_DIM
        o_gn = o_kv + 6 * kvw
        o_ga = o_gn + 3 * N_HEADS
        kv = lambda i: w[:, o_kv + i * kvw:o_kv + (i + 1) * kvw]
        wn = bf(jnp.concatenate([w[:, :o_q], kv(0), kv(1), kv(2), kv(4)], axis=1))
        gate_cols = np.full((N_KV, 4, HPG), -1)
        for g in range(N_KV):
            for br in range(3):
                for hh in range(HPG):
                    gate_cols[g, br, hh] = o_gn + 3 * HPG * g + 3 * hh + br
        gate_cols = gate_cols.reshape(-1)
        wg = jnp.where((gate_cols >= 0)[None, :], w[:, np.maximum(gate_cols, 0)], 0.0)
        w_q = w[:, o_q:o_kv] * (HEAD_DIM ** -0.5 * LOG2E)
        wt = bf(jnp.concatenate([w_q, kv(3), kv(5), wg], axis=1).T)
        us, pn, pt = _proj(u.reshape(b, t, d), wn, wt)

        wbu, wc, a_re, a_im, dmat = _s5_tables(ssm_a_re[l], ssm_a_im[l], ssm_log_dt[l], ssm_b_re[l],
                                               ssm_b_im[l], ssm_c_re[l], ssm_c_im[l], ssm_d[l], b)
        ys = _s5(us.reshape(t * 2 * b, S5_HALF), wbu, wc, a_re, a_im, dmat).reshape(t, b * SSM_WIDTH)

        def rows16(a):
            a = a.reshape(b, t // CMP_STRIDE, CMP_STRIDE, N_KV, HEAD_DIM)
            return a.transpose(0, 3, 1, 2, 4).reshape(b, N_KV, t // CMP_STRIDE, CMP_STRIDE * HEAD_DIM)

        half = CMP_STRIDE * HEAD_DIM
        w1cat = lambda w1: bf(jnp.concatenate([w1[:half], w1[half:]], axis=1))
        pos8 = jnp.zeros((8, half), F32).at[0].set(cmp_pos[l][:CMP_STRIDE].reshape(-1))
        pos8 = bf(pos8.at[1].set(cmp_pos[l][CMP_STRIDE:].reshape(-1)))
        kc, vct = _compress(rows16(pn[:, :, :kvw]), rows16(pn[:, :, kvw:2 * kvw]), pos8,
                            w1cat(cmp_k_w1[l]), bf(cmp_k_w2[l]), w1cat(cmp_v_w1[l]), bf(cmp_v_w2[l].T))
        bsel, bwin, bcmp, cvec = _bias_tables(rel_bias)
        o_nsa = _nsa(pn, pt, kc, vct, bsel, bwin, bcmp, cvec, _overlap_t(t))

        z = _merge(ys, o_nsa.reshape(n, N_HEADS * HEAD_DIM), u, bf(ssm_glu_w1[l]), bf(ssm_glu_w2[l]),
                   bf(nsa_w_o[l]), bf(w[:, o_ga:o_ga + d]), bf(w[:, o_ga + d:o_ga + 2 * d]))
        h = _outproj(z, h, bf(w_out[l]), mix_post_g[l][None])

        h, _ = _ffn(h, ffn2_pre_g[l][None], bf(ffn2_w_gate[l]), bf(ffn2_w_up[l]), bf(ffn2_w_down[l]),
                    ffn2_post_g[l][None], ffn2_post_g[l][None], False)
    return h.reshape(b, t, d)
```
